```python
import jax, jax.numpy as jnp
from jax import lax
import numpy as np

D_MODEL = 1024
BATCH = 8
SEQ = 2048
DEPTH = 1

D_MIX = D_MODEL
HG_HEADS = 4
HG_KEY = 128
HG_VAL = 128
HG_FDIM = HG_HEADS * HG_KEY
HG_VDIM = HG_HEADS * HG_VAL
HG_CHUNK = 64
MLA_HEADS = 4
MLA_Q_RANK = 384
MLA_KV_RANK = 128
MLA_NOPE = 128
MLA_ROPE = 64
MLA_VHEAD = 128
MLA_VDIM = MLA_HEADS * MLA_VHEAD
ROPE_THETA = 10000.0
Q_BLOCK = 128
IN_COLS = 3 * HG_FDIM + 2 * HG_VDIM + MLA_Q_RANK + MLA_KV_RANK + MLA_ROPE
D_FF = 2816
CONV_W = 3
DN_ALPHA = (2.0 * DEPTH) ** 0.25
DN_BETA = (8.0 * DEPTH) ** -0.25
NORM_EPS = 1e-5

kernel_name = "hymba_hgrn2_mla_convglu_deepnorm_encoder"


def _layernorm(x, g, b):
    xf = x.astype(jnp.float32)
    mu = jnp.mean(xf, -1, keepdims=True)
    var = jnp.mean(jnp.square(xf - mu), -1, keepdims=True)
    return ((xf - mu) * lax.rsqrt(var + NORM_EPS) * g + b).astype(x.dtype)


def _rmsnorm(x, g):
    xf = x.astype(jnp.float32)
    y = xf * lax.rsqrt(jnp.mean(xf * xf, -1, keepdims=True) + NORM_EPS)
    return (y * g).astype(x.dtype)


def _rope(x, cos, sin):
    half = MLA_ROPE // 2
    xf = x.astype(jnp.float32)
    x1, x2 = xf[..., :half], xf[..., half:]
    return jnp.concatenate([x1 * cos - x2 * sin, x2 * cos + x1 * sin], -1).astype(x.dtype)


def _gla_chunk_scan(q, k, v, g):
    B, H, S, K = q.shape
    V = v.shape[-1]
    n = S // HG_CHUNK

    def to_chunks(t):
        return t.astype(jnp.float32).reshape(B, H, n, HG_CHUNK, t.shape[-1]).transpose(2, 0, 1, 3, 4)

    qc, kc, vc, gc = to_chunks(q), to_chunks(k), to_chunks(v), to_chunks(g)
    mask = jnp.tril(jnp.ones((HG_CHUNK, HG_CHUNK), dtype=bool))[:, :, None]

    def step(state, inp):
        qb, kb, vb, gb = inp
        G = jnp.cumsum(gb, axis=2)
        o_inter = jnp.einsum('bhck,bhkv->bhcv', qb * jnp.exp(G), state)
        diff = G[:, :, :, None, :] - G[:, :, None, :, :]
        decay = jnp.exp(jnp.where(mask, diff, -jnp.inf))
        A = jnp.einsum('bhtk,bhsk,bhtsk->bhts', qb, kb, decay)
        o = o_inter + jnp.einsum('bhts,bhsv->bhtv', A, vb)
        G_last = G[:, :, -1:, :]
        new_state = jnp.exp(G_last[:, :, 0, :])[..., None] * state + jnp.einsum(
            'bhck,bhcv->bhkv', kb * jnp.exp(G_last - G), vb)
        return new_state, o

    s0 = jnp.zeros((B, H, K, V), jnp.float32)
    _, o = lax.scan(step, s0, (qc, kc, vc, gc))
    return o.transpose(1, 2, 0, 3, 4).reshape(B, H, S, V)


def _hgrn2_group(q_raw, i_raw, ff_raw, fb_raw, g_raw, lb_f, lb_b, norm_g):
    B, S, _ = q_raw.shape

    def heads(t, d):
        return t.reshape(B, S, HG_HEADS, d).transpose(0, 2, 1, 3)

    q = heads(jax.nn.silu(q_raw.astype(jnp.float32)), HG_KEY)
    v = heads(i_raw.astype(jnp.float32), HG_VAL)

    def gates(f_raw, lb):
        f = lb + (1.0 - lb) * jax.nn.sigmoid(f_raw.astype(jnp.float32))
        return heads(1.0 - f, HG_KEY), heads(jnp.log(f), HG_KEY)

    k_f, g_f = gates(ff_raw, lb_f)
    k_b, g_b = gates(fb_raw, lb_b)
    o_f = _gla_chunk_scan(q, k_f, v, g_f)
    flip = lambda t: jnp.flip(t, axis=2)
    o_b = flip(_gla_chunk_scan(flip(q), flip(k_b), flip(v), flip(g_b)))
    o = (o_f + o_b).transpose(0, 2, 1, 3)
    gate = g_raw.astype(jnp.float32).reshape(B, S, HG_HEADS, HG_VAL)
    o = _rmsnorm(o, norm_g) * jax.nn.silu(gate)
    return o.reshape(B, S, HG_VDIM).astype(q_raw.dtype)


def _block_attention(q_nope, q_pe, k_nope, k_pe, v):
    B, H, S, N = q_nope.shape
    nb = S // Q_BLOCK
    scale = (MLA_NOPE + MLA_ROPE) ** -0.5

    def blocks(t):
        return t.reshape(B, H, nb, Q_BLOCK, t.shape[-1]).transpose(2, 0, 1, 3, 4)

    def one(blk):
        qn, qr = blk
        s = (jnp.einsum('bhqn,bhkn->bhqk', qn, k_nope)
             + jnp.einsum('bhqr,bkr->bhqk', qr, k_pe)).astype(jnp.float32) * scale
        p = jax.nn.softmax(s, axis=-1)
        return jnp.einsum('bhqk,bhkd->bhqd', p.astype(v.dtype), v)

    o = lax.map(one, (blocks(q_nope), blocks(q_pe)))
    return o.transpose(1, 2, 0, 3, 4).reshape(B, H, S, v.shape[-1])


def _mla_group(qa, kva, kr, cos, sin, q_a_norm_g, w_q_b, kv_a_norm_g, w_kv_b, attn_norm_g):
    B, S, _ = qa.shape
    qh = (_rmsnorm(qa, q_a_norm_g) @ w_q_b).reshape(B, S, MLA_HEADS, MLA_NOPE + MLA_ROPE)
    q_nope = qh[..., :MLA_NOPE]
    q_pe = _rope(qh[..., MLA_NOPE:], cos[:, :, None, :], sin[:, :, None, :])
    kvh = (_rmsnorm(kva, kv_a_norm_g) @ w_kv_b).reshape(B, S, MLA_HEADS, MLA_NOPE + MLA_VHEAD)
    k_nope, v = kvh[..., :MLA_NOPE], kvh[..., MLA_NOPE:]
    k_pe = _rope(kr, cos, sin)
    t = lambda a: a.transpose(0, 2, 1, 3)
    o = _block_attention(t(q_nope), t(q_pe), t(k_nope), k_pe, t(v))
    o = _rmsnorm(t(o), attn_norm_g)
    return o.reshape(B, S, MLA_VDIM)


def _conv_glu_ffn(h, w_up, conv_w, conv_b, w_down):
    u = h @ w_up
    up = jnp.pad(u, ((0, 0), (1, 1), (0, 0)))
    c = up[:, :-2] * conv_w[0] + up[:, 1:-1] * conv_w[1] + up[:, 2:] * conv_w[2] + conv_b
    a, b = c[..., :D_FF], c[..., D_FF:]
    return (jax.nn.gelu(a) * b) @ w_down


def setup_inputs(seed: int = 0) -> dict:
    key = jax.random.key(seed)
    ks = jax.random.split(key, 24)
    f32 = jnp.float32
    nrm = lambda k, shape, s: jax.random.normal(k, shape, f32) * s
    gain = lambda k, shape: 1.0 + 0.02 * jax.random.normal(k, shape, f32)
    return {
        "x": jax.random.normal(ks[0], (BATCH, SEQ, D_MODEL), f32),
        "positions": jnp.broadcast_to(jnp.arange(SEQ, dtype=jnp.int32), (BATCH, SEQ)),
        "ln_in_g": gain(ks[1], (D_MODEL,)),
        "ln_in_b": nrm(ks[2], (D_MODEL,), 0.02),
        "w_in": nrm(ks[3], (DEPTH, D_MODEL, IN_COLS), D_MODEL ** -0.5),
        "lb_fwd": nrm(ks[4], (DEPTH + 1, HG_FDIM), 0.1),
        "lb_bwd": nrm(ks[5], (DEPTH + 1, HG_FDIM), 0.1),
        "hg_norm_g": gain(ks[6], (DEPTH, HG_VAL)),
        "q_a_norm_g": gain(ks[7], (DEPTH, MLA_Q_RANK)),
        "w_q_b": nrm(ks[8], (DEPTH, MLA_Q_RANK, MLA_HEADS * (MLA_NOPE + MLA_ROPE)), MLA_Q_RANK ** -0.5),
        "kv_a_norm_g": gain(ks[9], (DEPTH, MLA_KV_RANK)),
        "w_kv_b": nrm(ks[10], (DEPTH, MLA_KV_RANK, MLA_HEADS * (MLA_NOPE + MLA_VHEAD)), MLA_KV_RANK ** -0.5),
        "attn_norm_g": gain(ks[11], (DEPTH, MLA_VHEAD)),
        "w_out": nrm(ks[12], (DEPTH, D_MIX, D_MODEL), DN_BETA * D_MIX ** -0.5),
        "ln1_g": gain(ks[13], (DEPTH, D_MODEL)),
        "ln1_b": nrm(ks[14], (DEPTH, D_MODEL), 0.02),
        "w_up": nrm(ks[15], (DEPTH, D_MODEL, 2 * D_FF), D_MODEL ** -0.5),
        "conv_w": nrm(ks[16], (DEPTH, CONV_W, 2 * D_FF), CONV_W ** -0.5),
        "conv_b": nrm(ks[17], (DEPTH, 2 * D_FF), 0.02),
        "w_down": nrm(ks[18], (DEPTH, D_FF, D_MODEL), DN_BETA * D_FF ** -0.5),
        "ln2_g": gain(ks[19], (DEPTH, D_MODEL)),
        "ln2_b": nrm(ks[20], (DEPTH, D_MODEL), 0.02),
    }


def reference(x, positions, ln_in_g, ln_in_b, w_in, lb_fwd, lb_bwd, hg_norm_g,
              q_a_norm_g, w_q_b, kv_a_norm_g, w_kv_b, attn_norm_g, w_out, ln1_g, ln1_b,
              w_up, conv_w, conv_b, w_down, ln2_g, ln2_b):
    half = MLA_ROPE // 2
    inv_freq = 1.0 / (ROPE_THETA ** (jnp.arange(half, dtype=jnp.float32) / half))
    ang = positions.astype(jnp.float32)[..., None] * inv_freq
    cos, sin = jnp.cos(ang), jnp.sin(ang)
    lbs_f = jnp.cumsum(jax.nn.softmax(lb_fwd.astype(jnp.float32), axis=0), axis=0)
    lbs_b = jnp.cumsum(jax.nn.softmax(lb_bwd.astype(jnp.float32), axis=0), axis=0)
    split_at = np.cumsum([HG_FDIM, HG_VDIM, HG_FDIM, HG_FDIM, HG_VDIM,
                          MLA_Q_RANK, MLA_KV_RANK]).tolist()

    h = _layernorm(x, ln_in_g, ln_in_b)
    for l in range(DEPTH):
        proj = h @ w_in[l]
        q_raw, i_raw, ff_raw, fb_raw, g_raw, qa, kva, kr = jnp.split(proj, split_at, axis=-1)
        hg = _hgrn2_group(q_raw, i_raw, ff_raw, fb_raw, g_raw, lbs_f[l], lbs_b[l], hg_norm_g[l])
        at = _mla_group(qa, kva, kr, cos, sin, q_a_norm_g[l], w_q_b[l], kv_a_norm_g[l],
                        w_kv_b[l], attn_norm_g[l])
        mix = jnp.concatenate([hg, at.astype(hg.dtype)], axis=-1) @ w_out[l]
        h = _layernorm(DN_ALPHA * h + mix, ln1_g[l], ln1_b[l])
        ffn = _conv_glu_ffn(h, w_up[l], conv_w[l], conv_b[l], w_down[l])
        h = _layernorm(DN_ALPHA * h + ffn, ln2_g[l], ln2_b[l])
    return h
```

```python
import functools

import jax
import jax.numpy as jnp
from jax import lax
from jax.experimental import pallas as pl
from jax.experimental.pallas import tpu as pltpu

D_MODEL = 1024
HG_HEADS = 4
HG_DIM = 128
HG_W = HG_HEADS * HG_DIM
MLA_HEADS = 4
MLA_Q_RANK = 384
MLA_KV_RANK = 128
MLA_NOPE = 128
MLA_ROPE = 64
MLA_VHEAD = 128
MLA_QK_PAD = 256
ROPE_THETA = 10000.0
D_FF = 2816
DN_ALPHA = 2.0 ** 0.25
NORM_EPS = 1e-5

LANES = 128
SUBLANES = 8
VMEM_LIMIT = 56 * 1024 * 1024

PROJ_TM = 512
HG_CHUNK = 128
ATT_TQ = 512
FFN_TM = 512
FFN_FC = 256

F32 = jnp.float32
BF16 = jnp.bfloat16


def _ln(x, g, b):
    mu = jnp.mean(x, axis=-1, keepdims=True)
    xc = x - mu
    var = jnp.mean(xc * xc, axis=-1, keepdims=True)
    return xc * lax.rsqrt(var + NORM_EPS) * g + b


def _rms(x, g):
    return x * lax.rsqrt(jnp.mean(x * x, axis=-1, keepdims=True) + NORM_EPS) * g


def _dot(a, b):
    return jnp.dot(a, b, preferred_element_type=F32)


def _dot_nt(a, b):
    return lax.dot_general(a, b, (((1,), (1,)), ((), ())), preferred_element_type=F32)


def _proj_kernel(x_ref, pos_ref, lng_ref, lnb_ref, w_hg_ref, w_lat_ref, lbf_ref, lbb_ref,
                 qan_ref, wqb_ref, kvn_ref, wkvb_ref, invf_ref, sgn_ref,
                 q_ref, ff_ref, fb_ref, gate_ref, v_ref, qa_ref, ka_ref, va_ref):
    h = _ln(x_ref[...], lng_ref[...], lnb_ref[...]).astype(BF16)

    def lower_bound(ref):
        r0, r1 = ref[0:1, :], ref[1:2, :]
        m = jnp.maximum(r0, r1)
        e0, e1 = jnp.exp(r0 - m), jnp.exp(r1 - m)
        return e0 / (e0 + e1)

    def hg_cols(j):
        return _dot(h, w_hg_ref[:, j * HG_W:(j + 1) * HG_W])

    q_ref[...] = jax.nn.silu(hg_cols(0))
    v_ref[...] = hg_cols(1).astype(BF16)
    lbf = lower_bound(lbf_ref)
    ff_ref[...] = lbf + (1.0 - lbf) * jax.nn.sigmoid(hg_cols(2))
    lbb = lower_bound(lbb_ref)
    fb_ref[...] = lbb + (1.0 - lbb) * jax.nn.sigmoid(hg_cols(3))
    gate_ref[...] = hg_cols(4)

    ang = pos_ref[...].astype(F32) * invf_ref[...]
    sgn = sgn_ref[...]
    cos_t = jnp.cos(ang) * jnp.abs(sgn)
    sin_t = jnp.sin(ang) * sgn

    def rope(grp):
        return grp * cos_t + pltpu.roll(grp, MLA_ROPE, 1) * sin_t

    lat = _dot(h, w_lat_ref[...])
    qa = _rms(lat[:, :MLA_Q_RANK], qan_ref[...]).astype(BF16)
    kva = _rms(lat[:, MLA_Q_RANK:MLA_Q_RANK + MLA_KV_RANK], kvn_ref[...]).astype(BF16)
    k_pe = rope(lat[:, MLA_Q_RANK + MLA_KV_RANK:]).astype(BF16)

    scale = (MLA_NOPE + MLA_ROPE) ** -0.5
    qh = _dot(qa, wqb_ref[...])
    kvh = _dot(kva, wkvb_ref[...])
    for hd in range(MLA_HEADS):
        qa_ref[0, hd, :, 0:LANES] = (qh[:, hd * LANES:(hd + 1) * LANES] * scale).astype(BF16)
        pe = rope(qh[:, (MLA_HEADS + hd) * LANES:(MLA_HEADS + hd + 1) * LANES])
        qa_ref[0, hd, :, LANES:2 * LANES] = (pe * scale).astype(BF16)
        ka_ref[0, hd, :, 0:LANES] = kvh[:, 2 * hd * LANES:(2 * hd + 1) * LANES].astype(BF16)
        ka_ref[0, hd, :, LANES:2 * LANES] = k_pe
        va_ref[0, hd, :, :] = kvh[:, (2 * hd + 1) * LANES:(2 * hd + 2) * LANES].astype(BF16)


def _proj(x2, pos2, ln_g, ln_b, w_hg, w_lat, lb_f, lb_b, qan, wqb, kvn, wkvb, invf, sgn, B, S):
    N = B * S
    tm = PROJ_TM
    spb = S // tm
    row = lambda i: (i, 0)
    const = lambda i: (0, 0)
    head = lambda i: (i // spb, 0, i % spb, 0)
    full = lambda a: pl.BlockSpec(a.shape, const)
    out_shape = (
        jax.ShapeDtypeStruct((N, HG_W), F32),
        jax.ShapeDtypeStruct((N, HG_W), F32),
        jax.ShapeDtypeStruct((N, HG_W), F32),
        jax.ShapeDtypeStruct((N, HG_W), F32),
        jax.ShapeDtypeStruct((N, HG_W), BF16),
        jax.ShapeDtypeStruct((B, MLA_HEADS, S, MLA_QK_PAD), BF16),
        jax.ShapeDtypeStruct((B, MLA_HEADS, S, MLA_QK_PAD), BF16),
        jax.ShapeDtypeStruct((B, MLA_HEADS, S, MLA_VHEAD), BF16),
    )
    tok = pl.BlockSpec((tm, HG_W), row)
    return pl.pallas_call(
        _proj_kernel,
        grid=(N // tm,),
        in_specs=[pl.BlockSpec((tm, D_MODEL), row), pl.BlockSpec((tm, 1), row),
                  full(ln_g), full(ln_b), full(w_hg), full(w_lat), full(lb_f), full(lb_b),
                  full(qan), full(wqb), full(kvn), full(wkvb), full(invf), full(sgn)],
        out_specs=(tok, tok, tok, tok, tok,
                   pl.BlockSpec((1, MLA_HEADS, tm, MLA_QK_PAD), head),
                   pl.BlockSpec((1, MLA_HEADS, tm, MLA_QK_PAD), head),
                   pl.BlockSpec((1, MLA_HEADS, tm, MLA_VHEAD), head)),
        out_shape=out_shape,
        compiler_params=pltpu.CompilerParams(dimension_semantics=("arbitrary",),
                                             vmem_limit_bytes=VMEM_LIMIT),
        name="proj",
    )(x2, pos2, ln_g, ln_b, w_hg, w_lat, lb_f, lb_b, qan, wqb, kvn, wkvb, invf, sgn)


def _hgrn_chunk(q, f, v, rev):
    C = q.shape[0]
    g = jnp.log(f)
    k = 1.0 - f
    row = lax.broadcasted_iota(jnp.int32, (C, HG_DIM), 0)
    ri = lax.broadcasted_iota(jnp.int32, (C, C), 0)
    ci = lax.broadcasted_iota(jnp.int32, (C, C), 1)
    P = g
    T = g
    A = jnp.zeros((C, C), F32)
    lvl, b = 0, 1
    while b < C:
        bit = ((row >> lvl) & 1) == 1
        later = jnp.logical_not(bit) if rev else bit
        w = jnp.exp(jnp.where(later, P, T - P))
        qt = jnp.where(later, q * w, 0.0).astype(BF16)
        kt = jnp.where(later, 0.0, k * w).astype(BF16)
        a_l = _dot_nt(qt, kt)
        if 2 * b < C:
            a_l = jnp.where((ri >> (lvl + 1)) == (ci >> (lvl + 1)), a_l, 0.0)
        A = A + a_l
        other = jnp.where(bit, pltpu.roll(T, b, 0), pltpu.roll(T, C - b, 0))
        P = P + jnp.where(later, other, 0.0)
        T = T + other
        lvl, b = lvl + 1, 2 * b
    vf = v.astype(F32)
    o = _dot(A.astype(BF16), v) + jnp.sum(q * k, axis=1, keepdims=True) * vf
    return o, (q * jnp.exp(P)).astype(BF16), (k * jnp.exp(T - P)).astype(BF16), jnp.exp(T)


def _hgrn_kernel(q_ref, ff_ref, fb_ref, v_ref, gate_ref, ng_ref, out_ref,
                 o_acc, qt_s, kt_s, vt_s, d_s):
    S = q_ref.shape[0]
    C = HG_CHUNK
    n = S // C

    def phase1(c, carry):
        r = pl.multiple_of(c * C, C)
        q = q_ref[pl.ds(r, C), :]
        v = v_ref[pl.ds(r, C), :]
        vt_s[c] = v.astype(F32).T.astype(BF16)
        o_f, qt, kt, d = _hgrn_chunk(q, ff_ref[pl.ds(r, C), :], v, False)
        qt_s[0, pl.ds(r, C), :] = qt
        kt_s[0, pl.ds(r, C), :] = kt
        d_s[0, c] = d[0:SUBLANES, :]
        o_b, qt, kt, d = _hgrn_chunk(q, fb_ref[pl.ds(r, C), :], v, True)
        qt_s[1, pl.ds(r, C), :] = qt
        kt_s[1, pl.ds(r, C), :] = kt
        d_s[1, c] = d[0:SUBLANES, :]
        o_acc[pl.ds(r, C), :] = o_f + o_b
        return carry

    lax.fori_loop(0, n, phase1, 0)

    def scan_step(i, carry):
        st_f, st_b = carry
        new = []
        for d, c, st in ((0, i, st_f), (1, n - 1 - i, st_b)):
            r = pl.multiple_of(c * C, C)
            o_acc[pl.ds(r, C), :] += _dot_nt(qt_s[d, pl.ds(r, C), :], st.astype(BF16))
            new.append(st * d_s[d, c][0:1, :] + _dot(vt_s[c], kt_s[d, pl.ds(r, C), :]))
        return tuple(new)

    z = jnp.zeros((HG_DIM, HG_DIM), F32)
    lax.fori_loop(0, n, scan_step, (z, z))

    def finish(c, carry):
        r = pl.multiple_of(c * C, C)
        o = _rms(o_acc[pl.ds(r, C), :], ng_ref[...])
        out_ref[pl.ds(r, C), :] = (o * jax.nn.silu(gate_ref[pl.ds(r, C), :])).astype(BF16)
        return carry

    lax.fori_loop(0, n, finish, 0)


def _hgrn(q, ff, fb, v, gate, ng, B, S):
    N = B * S
    C = HG_CHUNK
    blk = pl.BlockSpec((S, HG_DIM), lambda b, h: (b, h))
    return pl.pallas_call(
        _hgrn_kernel,
        grid=(B, HG_HEADS),
        in_specs=[blk, blk, blk, blk, blk, pl.BlockSpec((1, HG_DIM), lambda b, h: (0, 0))],
        out_specs=blk,
        out_shape=jax.ShapeDtypeStruct((N, HG_W), BF16),
        scratch_shapes=[pltpu.VMEM((S, HG_DIM), F32),
                        pltpu.VMEM((2, S, HG_DIM), BF16),
                        pltpu.VMEM((2, S, HG_DIM), BF16),
                        pltpu.VMEM((S // C, HG_DIM, C), BF16),
                        pltpu.VMEM((2, S // C, SUBLANES, HG_DIM), F32)],
        compiler_params=pltpu.CompilerParams(dimension_semantics=("arbitrary", "arbitrary"),
                                             vmem_limit_bytes=VMEM_LIMIT),
        name="hgrn",
    )(q, ff, fb, v, gate, ng)


def _attn_kernel(q_ref, k_ref, v_ref, ng_ref, out_ref):
    s = _dot_nt(q_ref[0, 0], k_ref[0, 0])
    p = jnp.exp(s - jnp.max(s, axis=-1, keepdims=True))
    l = jnp.sum(p, axis=-1, keepdims=True)
    o = _dot(p.astype(BF16), v_ref[0, 0]) / l
    out_ref[...] = _rms(o, ng_ref[...]).astype(BF16)


def _attn(qa, ka, va, ng, B, S):
    tq = ATT_TQ
    nq = S // tq
    return pl.pallas_call(
        _attn_kernel,
        grid=(B, MLA_HEADS, nq),
        in_specs=[pl.BlockSpec((1, 1, tq, MLA_QK_PAD), lambda b, h, i: (b, h, i, 0)),
                  pl.BlockSpec((1, 1, S, MLA_QK_PAD), lambda b, h, i: (b, h, 0, 0)),
                  pl.BlockSpec((1, 1, S, MLA_VHEAD), lambda b, h, i: (b, h, 0, 0)),
                  pl.BlockSpec((1, MLA_VHEAD), lambda b, h, i: (0, 0))],
        out_specs=pl.BlockSpec((tq, MLA_VHEAD), lambda b, h, i: (b * nq + i, h)),
        out_shape=jax.ShapeDtypeStruct((B * S, MLA_HEADS * MLA_VHEAD), BF16),
        compiler_params=pltpu.CompilerParams(
            dimension_semantics=("arbitrary", "arbitrary", "arbitrary"),
            vmem_limit_bytes=VMEM_LIMIT),
        name="attn",
    )(qa, ka, va, ng)


def _mix_kernel(x_ref, hg_ref, at_ref, lng_ref, lnb_ref, wo_ref, g1_ref, b1_ref, out_ref):
    h = _ln(x_ref[...], lng_ref[...], lnb_ref[...])
    mix = _dot(hg_ref[...], wo_ref[0:HG_W, :]) + _dot(at_ref[...], wo_ref[HG_W:, :])
    out_ref[...] = _ln(DN_ALPHA * h + mix, g1_ref[...], b1_ref[...])


def _mix(x2, hg, at, ln_g, ln_b, wo, g1, b1):
    N = x2.shape[0]
    tm = PROJ_TM
    row = lambda i: (i, 0)
    const = lambda i: (0, 0)
    full = lambda a: pl.BlockSpec(a.shape, const)
    return pl.pallas_call(
        _mix_kernel,
        grid=(N // tm,),
        in_specs=[pl.BlockSpec((tm, D_MODEL), row), pl.BlockSpec((tm, HG_W), row),
                  pl.BlockSpec((tm, HG_W), row), full(ln_g), full(ln_b), full(wo),
                  full(g1), full(b1)],
        out_specs=pl.BlockSpec((tm, D_MODEL), row),
        out_shape=jax.ShapeDtypeStruct((N, D_MODEL), F32),
        compiler_params=pltpu.CompilerParams(dimension_semantics=("arbitrary",),
                                             vmem_limit_bytes=VMEM_LIMIT),
        name="mix",
    )(x2, hg, at, ln_g, ln_b, wo, g1, b1)


def _ffn_kernel(h_ref, hp_ref, hn_ref, wu_ref, cw_ref, cb_ref, wd_ref, g2_ref, b2_ref, out_ref,
                ext_s, u_s, acc_s, *, tiles_per_seq):
    tm = FFN_TM
    i = pl.program_id(0)
    has_prev = jnp.where(i % tiles_per_seq != 0, 1.0, 0.0)
    has_next = jnp.where(i % tiles_per_seq != tiles_per_seq - 1, 1.0, 0.0)
    ext_s[0:SUBLANES, :] = (hp_ref[...] * has_prev).astype(BF16)
    ext_s[SUBLANES:SUBLANES + tm, :] = h_ref[...].astype(BF16)
    ext_s[SUBLANES + tm:, :] = (hn_ref[...] * has_next).astype(BF16)
    acc_s[...] = DN_ALPHA * h_ref[...]

    def step(j, carry):
        u_s[...] = _dot(ext_s[...], wu_ref[j])
        cw = cw_ref[j]
        c = (u_s[SUBLANES - 1:SUBLANES - 1 + tm, :] * cw[0:1, :]
             + u_s[SUBLANES:SUBLANES + tm, :] * cw[1:2, :]
             + u_s[SUBLANES + 1:SUBLANES + 1 + tm, :] * cw[2:3, :] + cb_ref[j])
        act = (jax.nn.gelu(c[:, :FFN_FC]) * c[:, FFN_FC:]).astype(BF16)
        acc_s[...] += _dot(act, wd_ref[j])
        return carry

    lax.fori_loop(0, wu_ref.shape[0], step, 0)
    out_ref[...] = _ln(acc_s[...], g2_ref[...], b2_ref[...])


def _ffn(h1, wu, cw, cb, wd, g2, b2, S):
    N = h1.shape[0]
    tm = FFN_TM
    nb = tm // SUBLANES
    last_blk = N // SUBLANES - 1
    row = lambda i: (i, 0)
    full = lambda a: pl.BlockSpec(a.shape, lambda i: (0,) * a.ndim)
    return pl.pallas_call(
        functools.partial(_ffn_kernel, tiles_per_seq=S // tm),
        grid=(N // tm,),
        in_specs=[pl.BlockSpec((tm, D_MODEL), row),
                  pl.BlockSpec((SUBLANES, D_MODEL), lambda i: (jnp.maximum(i * nb - 1, 0), 0)),
                  pl.BlockSpec((SUBLANES, D_MODEL), lambda i: (jnp.minimum((i + 1) * nb, last_blk), 0)),
                  full(wu), full(cw), full(cb), full(wd), full(g2), full(b2)],
        out_specs=pl.BlockSpec((tm, D_MODEL), row),
        out_shape=jax.ShapeDtypeStruct((N, D_MODEL), F32),
        scratch_shapes=[pltpu.VMEM((tm + 2 * SUBLANES, D_MODEL), BF16),
                        pltpu.VMEM((tm + 2 * SUBLANES, 2 * FFN_FC), F32),
                        pltpu.VMEM((tm, D_MODEL), F32)],
        compiler_params=pltpu.CompilerParams(dimension_semantics=("arbitrary",),
                                             vmem_limit_bytes=VMEM_LIMIT),
        name="ffn",
    )(h1, h1, h1, wu, cw, cb, wd, g2, b2)


def _swap_halves(w):
    half = MLA_ROPE // 2
    return jnp.concatenate([w[..., half:], w[..., :half]], axis=-1)


def kernel(x, positions, ln_in_g, ln_in_b, w_in, lb_fwd, lb_bwd, hg_norm_g, q_a_norm_g, w_q_b,
           kv_a_norm_g, w_kv_b, attn_norm_g, w_out, ln1_g, ln1_b, w_up, conv_w, conv_b, w_down,
           ln2_g, ln2_b):
    B, S, D = x.shape
    assert D == D_MODEL and w_in.shape[0] == 1 and lb_fwd.shape == (2, HG_W)
    assert S % HG_CHUNK == 0 and S % ATT_TQ == 0 and S % FFN_TM == 0 and D_FF % FFN_FC == 0
    N = B * S
    r1 = lambda a: a.reshape(1, -1)

    w_in0 = w_in[0]
    c_lat = 5 * HG_W
    c_kr = c_lat + MLA_Q_RANK + MLA_KV_RANK
    w_hg = w_in0[:, :c_lat].astype(BF16)
    w_kr = w_in0[:, c_kr:]
    w_lat = jnp.concatenate([w_in0[:, c_lat:c_kr], w_kr, _swap_halves(w_kr)], axis=1).astype(BF16)
    wq = w_q_b[0].reshape(MLA_Q_RANK, MLA_HEADS, MLA_NOPE + MLA_ROPE)
    wq_pe = wq[:, :, MLA_NOPE:]
    wqb = jnp.concatenate(
        [wq[:, :, :MLA_NOPE].reshape(MLA_Q_RANK, -1),
         jnp.concatenate([wq_pe, _swap_halves(wq_pe)], axis=-1).reshape(MLA_Q_RANK, -1)],
        axis=1).astype(BF16)
    wkvb = w_kv_b[0].astype(BF16)
    half = MLA_ROPE // 2
    inv_freq = 1.0 / (ROPE_THETA ** (jnp.arange(half, dtype=F32) / half))
    zeros = jnp.zeros((LANES - MLA_ROPE,), F32)
    invf = r1(jnp.concatenate([inv_freq, inv_freq, zeros]))
    sgn = r1(jnp.concatenate([-jnp.ones((half,), F32), jnp.ones((half,), F32), zeros]))
    nfc = D_FF // FFN_FC
    glu = lambda a: jnp.concatenate(
        [a[..., :D_FF].reshape(a.shape[:-1] + (nfc, FFN_FC)),
         a[..., D_FF:].reshape(a.shape[:-1] + (nfc, FFN_FC))], axis=-1)
    wu = jnp.transpose(glu(w_up[0]), (1, 0, 2)).astype(BF16)
    cw = jnp.transpose(glu(conv_w[0]), (1, 0, 2))
    cb = glu(conv_b[0]).reshape(nfc, 1, 2 * FFN_FC)
    wd = w_down[0].reshape(nfc, FFN_FC, D_MODEL).astype(BF16)

    x2 = x.reshape(N, D)
    pos2 = positions.reshape(N, 1)
    q, ff, fb, gate, v, qa, ka, va = _proj(
        x2, pos2, r1(ln_in_g), r1(ln_in_b), w_hg, w_lat, lb_fwd, lb_bwd,
        r1(q_a_norm_g[0]), wqb, r1(kv_a_norm_g[0]), wkvb, invf, sgn, B, S)
    hg = _hgrn(q, ff, fb, v, gate, r1(hg_norm_g[0]), B, S)
    at = _attn(qa, ka, va, r1(attn_norm_g[0]), B, S)
    h1 = _mix(x2, hg, at, r1(ln_in_g), r1(ln_in_b), w_out[0].astype(BF16), r1(ln1_g[0]), r1(ln1_b[0]))
    out = _ffn(h1, wu, cw, cb, wd, r1(ln2_g[0]), r1(ln2_b[0]), S)
    return out.reshape(B, S, D)
```

```python
import functools

import jax
import jax.numpy as jnp
from jax import lax
from jax.experimental import pallas as pl
from jax.experimental.pallas import tpu as pltpu

D_MODEL = 1024
HG_HEADS = 4
HG_DIM = 128
HG_W = HG_HEADS * HG_DIM
MLA_HEADS = 4
MLA_Q_RANK = 384
MLA_KV_RANK = 128
MLA_NOPE = 128
MLA_ROPE = 64
MLA_VHEAD = 128
MLA_QK_PAD = 256
ROPE_THETA = 10000.0
D_FF = 2816
DN_ALPHA = 2.0 ** 0.25
NORM_EPS = 1e-5
LOG2E = 1.4426950408889634

LANES = 128
SUBLANES = 8
VMEM_LIMIT = 56 * 1024 * 1024

PROJ_TM = 512
HG_CHUNK = 128
HG_BLOCKS = HG_CHUNK // SUBLANES
HG_UNROLL = 8
ATT_TQ = 512
ATT_SUB = 32
FFN_TM = 512
FFN_FC = 256

F32 = jnp.float32
BF16 = jnp.bfloat16


def _ln(x, g, b):
    mu = jnp.mean(x, axis=-1, keepdims=True)
    xc = x - mu
    var = jnp.mean(xc * xc, axis=-1, keepdims=True)
    return xc * lax.rsqrt(var + NORM_EPS) * g + b


def _rms(x, g):
    return x * lax.rsqrt(jnp.mean(x * x, axis=-1, keepdims=True) + NORM_EPS) * g


def _dot(a, b):
    return jnp.dot(a, b, preferred_element_type=F32)


def _dot_nt(a, b):
    return lax.dot_general(a, b, (((1,), (1,)), ((), ())), preferred_element_type=F32)


def _proj_kernel(x_ref, pos_ref, lng_ref, lnb_ref, w_hg_ref, w_lat_ref, lbf_ref, lbb_ref,
                 qan_ref, wqb_ref, kvn_ref, wkvb_ref, invf_ref, sgn_ref,
                 q_ref, ff_ref, fb_ref, gate_ref, v_ref, qa_ref, ka_ref, va_ref):
    h = _ln(x_ref[...], lng_ref[...], lnb_ref[...]).astype(BF16)

    def lower_bound(ref):
        r0, r1 = ref[0:1, :], ref[1:2, :]
        m = jnp.maximum(r0, r1)
        e0, e1 = jnp.exp(r0 - m), jnp.exp(r1 - m)
        return e0 / (e0 + e1)

    def hg_cols(j):
        return _dot(h, w_hg_ref[:, j * HG_W:(j + 1) * HG_W])

    q_ref[...] = jax.nn.silu(hg_cols(0))
    v_ref[...] = hg_cols(1).astype(BF16)
    lbf = lower_bound(lbf_ref)
    ff_ref[...] = lbf + (1.0 - lbf) * jax.nn.sigmoid(hg_cols(2))
    lbb = lower_bound(lbb_ref)
    fb_ref[...] = lbb + (1.0 - lbb) * jax.nn.sigmoid(hg_cols(3))
    gate_ref[...] = hg_cols(4)

    ang = pos_ref[...].astype(F32) * invf_ref[...]
    sgn = sgn_ref[...]
    cos_t = jnp.cos(ang) * jnp.abs(sgn)
    sin_t = jnp.sin(ang) * sgn

    def rope(grp):
        return grp * cos_t + pltpu.roll(grp, MLA_ROPE, 1) * sin_t

    lat = _dot(h, w_lat_ref[...])
    qa = _rms(lat[:, :MLA_Q_RANK], qan_ref[...]).astype(BF16)
    kva = _rms(lat[:, MLA_Q_RANK:MLA_Q_RANK + MLA_KV_RANK], kvn_ref[...]).astype(BF16)
    k_pe = rope(lat[:, MLA_Q_RANK + MLA_KV_RANK:]).astype(BF16)

    scale = (MLA_NOPE + MLA_ROPE) ** -0.5 * LOG2E
    qh = _dot(qa, wqb_ref[...])
    kvh = _dot(kva, wkvb_ref[...])
    for hd in range(MLA_HEADS):
        qa_ref[0, hd, :, 0:LANES] = (qh[:, hd * LANES:(hd + 1) * LANES] * scale).astype(BF16)
        pe = rope(qh[:, (MLA_HEADS + hd) * LANES:(MLA_HEADS + hd + 1) * LANES])
        qa_ref[0, hd, :, LANES:2 * LANES] = (pe * scale).astype(BF16)
        ka_ref[0, hd, :, 0:LANES] = kvh[:, 2 * hd * LANES:(2 * hd + 1) * LANES].astype(BF16)
        ka_ref[0, hd, :, LANES:2 * LANES] = k_pe
        va_ref[0, hd, :, :] = kvh[:, (2 * hd + 1) * LANES:(2 * hd + 2) * LANES].astype(BF16)


def _proj(x2, pos2, ln_g, ln_b, w_hg, w_lat, lb_f, lb_b, qan, wqb, kvn, wkvb, invf, sgn, B, S):
    N = B * S
    tm = PROJ_TM
    spb = S // tm
    row = lambda i: (i, 0)
    const = lambda i: (0, 0)
    head = lambda i: (i // spb, 0, i % spb, 0)
    full = lambda a: pl.BlockSpec(a.shape, const)
    out_shape = (
        jax.ShapeDtypeStruct((N, HG_W), F32),
        jax.ShapeDtypeStruct((N, HG_W), F32),
        jax.ShapeDtypeStruct((N, HG_W), F32),
        jax.ShapeDtypeStruct((N, HG_W), F32),
        jax.ShapeDtypeStruct((N, HG_W), BF16),
        jax.ShapeDtypeStruct((B, MLA_HEADS, S, MLA_QK_PAD), BF16),
        jax.ShapeDtypeStruct((B, MLA_HEADS, S, MLA_QK_PAD), BF16),
        jax.ShapeDtypeStruct((B, MLA_HEADS, S, MLA_VHEAD), BF16),
    )
    tok = pl.BlockSpec((tm, HG_W), row)
    return pl.pallas_call(
        _proj_kernel,
        grid=(N // tm,),
        in_specs=[pl.BlockSpec((tm, D_MODEL), row), pl.BlockSpec((tm, 1), row),
                  full(ln_g), full(ln_b), full(w_hg), full(w_lat), full(lb_f), full(lb_b),
                  full(qan), full(wqb), full(kvn), full(wkvb), full(invf), full(sgn)],
        out_specs=(tok, tok, tok, tok, tok,
                   pl.BlockSpec((1, MLA_HEADS, tm, MLA_QK_PAD), head),
                   pl.BlockSpec((1, MLA_HEADS, tm, MLA_QK_PAD), head),
                   pl.BlockSpec((1, MLA_HEADS, tm, MLA_VHEAD), head)),
        out_shape=out_shape,
        compiler_params=pltpu.CompilerParams(dimension_semantics=("arbitrary",),
                                             vmem_limit_bytes=VMEM_LIMIT),
        name="proj",
    )(x2, pos2, ln_g, ln_b, w_hg, w_lat, lb_f, lb_b, qan, wqb, kvn, wkvb, invf, sgn)


def _blocks(x):
    return [x[i * SUBLANES:(i + 1) * SUBLANES] for i in range(x.shape[0] // SUBLANES)]


def _rows(blocks):
    return jnp.concatenate(blocks, axis=0)


def _hgrn_chunk(q, f, v, lmap_ref, rev):
    C = HG_CHUNK
    g = jnp.log(f) * LOG2E
    k = 1.0 - f
    row = lax.broadcasted_iota(jnp.int32, (C, HG_DIM), 0)

    def roll8(x, shift):
        return _rows([pltpu.roll(blk, shift, 0) for blk in _blocks(x)])

    P, T = g, g
    A = None
    for lvl in range(3):
        b = 1 << lvl
        bit = ((row >> lvl) & 1) == 1
        later = jnp.logical_not(bit) if rev else bit
        w = jnp.exp2(jnp.where(later, P, T - P))
        a_l = _dot_nt((q * w).astype(BF16), (k * w).astype(BF16))
        hit = lmap_ref[...] == lvl
        A = jnp.where(hit, a_l, 0.0) if A is None else jnp.where(hit, a_l, A)
        sib = roll8(T, b) if 2 * b == SUBLANES else jnp.where(bit, roll8(T, b), roll8(T, SUBLANES - b))
        P = P + jnp.where(later, sib, 0.0)
        T = T + sib

    Pb, Tb, qb, kb, Ab = _blocks(P), _blocks(T), _blocks(q), _blocks(k), _blocks(A)
    zero = jnp.zeros((SUBLANES, HG_DIM), F32)
    lvl, m = 3, 1
    while m < HG_BLOCKS:
        later = [(((i // m) & 1) == 1) != rev for i in range(HG_BLOCKS)]
        wb = [jnp.exp2(Pb[i] if later[i] else Tb[i] - Pb[i]) for i in range(HG_BLOCKS)]
        qt = _rows([qb[i] * wb[i] for i in range(HG_BLOCKS) if later[i]]).astype(BF16)
        kt = _rows([zero if later[i] else kb[i] * wb[i] for i in range(HG_BLOCKS)]).astype(BF16)
        a_l = _blocks(_dot_nt(qt, kt))
        n_later = 0
        for i in range(HG_BLOCKS):
            if later[i]:
                hit = lmap_ref[i * SUBLANES:(i + 1) * SUBLANES, :] == lvl
                Ab[i] = jnp.where(hit, a_l[n_later], Ab[i])
                n_later += 1
        sib = [Tb[i ^ m] for i in range(HG_BLOCKS)]
        Pb = [Pb[i] + sib[i] if later[i] else Pb[i] for i in range(HG_BLOCKS)]
        tot = {}
        for i in range(HG_BLOCKS):
            key = i // (2 * m)
            if key not in tot:
                tot[key] = Tb[i] + sib[i]
        Tb = [tot[i // (2 * m)] for i in range(HG_BLOCKS)]
        lvl, m = lvl + 1, 2 * m

    vf = v.astype(F32)
    o = _dot(_rows(Ab).astype(BF16), v) + jnp.sum(q * k, axis=1, keepdims=True) * vf
    qt = _rows([qb[i] * jnp.exp2(Pb[i]) for i in range(HG_BLOCKS)]).astype(BF16)
    kt = _rows([kb[i] * jnp.exp2(Tb[i] - Pb[i]) for i in range(HG_BLOCKS)]).astype(BF16)
    return o, qt, kt, jnp.exp2(Tb[0])


def _hgrn_kernel(q_ref, ff_ref, fb_ref, v_ref, gate_ref, ng_ref, out_ref, o_acc, lmap_s):
    S = q_ref.shape[0]
    C = HG_CHUNK
    n = S // C

    ri = lax.broadcasted_iota(jnp.int32, (C, C), 0)
    ci = lax.broadcasted_iota(jnp.int32, (C, C), 1)
    x = ri ^ ci
    level = jnp.full((C, C), -1, jnp.int32)
    b = 1
    while b < C:
        level = level + jnp.where(x >= b, 1, 0)
        b *= 2
    lmap_s[0] = jnp.where(ci < ri, level, -1)
    lmap_s[1] = jnp.where(ci > ri, level, -1)

    def visit(c, f_ref, st, rev, first):
        r = pl.multiple_of(c * C, C)
        q = q_ref[pl.ds(r, C), :]
        v = v_ref[pl.ds(r, C), :]
        o, qt, kt, d = _hgrn_chunk(q, f_ref[pl.ds(r, C), :], v, lmap_s.at[1 if rev else 0], rev)
        o = o + _dot_nt(qt, st.astype(BF16))
        st = st * d[0:1, :] + _dot(v.astype(F32).T.astype(BF16), kt)
        if first:
            o_acc[pl.ds(r, C), :] = o
        else:
            o = _rms(o + o_acc[pl.ds(r, C), :], ng_ref[...])
            out_ref[pl.ds(r, C), :] = (o * jax.nn.silu(gate_ref[pl.ds(r, C), :])).astype(BF16)
        return st

    def step(first):
        def body(i, carry):
            st_f, st_b = carry
            return (visit(i, ff_ref, st_f, False, first),
                    visit(n - 1 - i, fb_ref, st_b, True, first))
        return body

    z = jnp.zeros((HG_DIM, HG_DIM), F32)
    carry = lax.fori_loop(0, n // 2, step(True), (z, z), unroll=HG_UNROLL)
    lax.fori_loop(n // 2, n, step(False), carry, unroll=HG_UNROLL)


def _hgrn(q, ff, fb, v, gate, ng, B, S):
    N = B * S
    C = HG_CHUNK
    assert (S // C) % 2 == 0
    blk = pl.BlockSpec((S, HG_DIM), lambda b, h: (b, h))
    return pl.pallas_call(
        _hgrn_kernel,
        grid=(B, HG_HEADS),
        in_specs=[blk, blk, blk, blk, blk, pl.BlockSpec((1, HG_DIM), lambda b, h: (0, 0))],
        out_specs=blk,
        out_shape=jax.ShapeDtypeStruct((N, HG_W), BF16),
        scratch_shapes=[pltpu.VMEM((S, HG_DIM), F32),
                        pltpu.VMEM((2, C, C), jnp.int32)],
        compiler_params=pltpu.CompilerParams(dimension_semantics=("arbitrary", "arbitrary"),
                                             vmem_limit_bytes=VMEM_LIMIT),
        name="hgrn",
    )(q, ff, fb, v, gate, ng)


def _attn_kernel(q_ref, k_ref, v_ref, ng_ref, out_ref, s_s, p_s, l_s):
    S = k_ref.shape[2]
    tq = ATT_TQ
    nq = S // tq

    def scores(i, slot):
        s_s[slot] = _dot_nt(q_ref[0, 0, i * tq:(i + 1) * tq, :], k_ref[0, 0])

    def softmax(slot):
        for r in range(0, tq, ATT_SUB):
            s = s_s[slot, r:r + ATT_SUB, :]
            p = jnp.exp2(s - jnp.max(s, axis=-1, keepdims=True))
            l_s[slot, r:r + ATT_SUB, :] = jnp.sum(p, axis=-1, keepdims=True)
            p_s[slot, r:r + ATT_SUB, :] = p.astype(BF16)

    def values(i, slot):
        o = _dot(p_s[slot], v_ref[0, 0]) / l_s[slot]
        out_ref[i * tq:(i + 1) * tq, :] = _rms(o, ng_ref[...]).astype(BF16)

    scores(0, 0)
    for i in range(nq):
        if i + 1 < nq:
            scores(i + 1, (i + 1) % 2)
        softmax(i % 2)
        values(i, i % 2)


def _attn(qa, ka, va, ng, B, S):
    tq = ATT_TQ
    return pl.pallas_call(
        _attn_kernel,
        grid=(B, MLA_HEADS),
        in_specs=[pl.BlockSpec((1, 1, S, MLA_QK_PAD), lambda b, h: (b, h, 0, 0)),
                  pl.BlockSpec((1, 1, S, MLA_QK_PAD), lambda b, h: (b, h, 0, 0)),
                  pl.BlockSpec((1, 1, S, MLA_VHEAD), lambda b, h: (b, h, 0, 0)),
                  pl.BlockSpec((1, MLA_VHEAD), lambda b, h: (0, 0))],
        out_specs=pl.BlockSpec((S, MLA_VHEAD), lambda b, h: (b, h)),
        out_shape=jax.ShapeDtypeStruct((B * S, MLA_HEADS * MLA_VHEAD), BF16),
        scratch_shapes=[pltpu.VMEM((2, tq, S), F32),
                        pltpu.VMEM((2, tq, S), BF16),
                        pltpu.VMEM((2, tq, 1), F32)],
        compiler_params=pltpu.CompilerParams(
            dimension_semantics=("arbitrary", "arbitrary"),
            vmem_limit_bytes=VMEM_LIMIT),
        name="attn",
    )(qa, ka, va, ng)


def _mix_kernel(x_ref, hg_ref, at_ref, lng_ref, lnb_ref, wo_ref, g1_ref, b1_ref, out_ref):
    h = _ln(x_ref[...], lng_ref[...], lnb_ref[...])
    mix = _dot(hg_ref[...], wo_ref[0:HG_W, :]) + _dot(at_ref[...], wo_ref[HG_W:, :])
    out_ref[...] = _ln(DN_ALPHA * h + mix, g1_ref[...], b1_ref[...])


def _mix(x2, hg, at, ln_g, ln_b, wo, g1, b1):
    N = x2.shape[0]
    tm = PROJ_TM
    row = lambda i: (i, 0)
    const = lambda i: (0, 0)
    full = lambda a: pl.BlockSpec(a.shape, const)
    return pl.pallas_call(
        _mix_kernel,
        grid=(N // tm,),
        in_specs=[pl.BlockSpec((tm, D_MODEL), row), pl.BlockSpec((tm, HG_W), row),
                  pl.BlockSpec((tm, HG_W), row), full(ln_g), full(ln_b), full(wo),
                  full(g1), full(b1)],
        out_specs=pl.BlockSpec((tm, D_MODEL), row),
        out_shape=jax.ShapeDtypeStruct((N, D_MODEL), F32),
        compiler_params=pltpu.CompilerParams(dimension_semantics=("arbitrary",),
                                             vmem_limit_bytes=VMEM_LIMIT),
        name="mix",
    )(x2, hg, at, ln_g, ln_b, wo, g1, b1)


def _ffn_kernel(h_ref, hp_ref, hn_ref, wu_ref, cw_ref, cb_ref, wd_ref, g2_ref, b2_ref, out_ref,
                ext_s, u_s, act_s, *, tiles_per_seq):
    tm = FFN_TM
    n = wu_ref.shape[0]
    i = pl.program_id(0)
    has_prev = jnp.where(i % tiles_per_seq != 0, 1.0, 0.0)
    has_next = jnp.where(i % tiles_per_seq != tiles_per_seq - 1, 1.0, 0.0)
    ext_s[0:SUBLANES, :] = (hp_ref[...] * has_prev).astype(BF16)
    ext_s[SUBLANES:SUBLANES + tm, :] = h_ref[...].astype(BF16)
    ext_s[SUBLANES + tm:, :] = (hn_ref[...] * has_next).astype(BF16)

    def up(j, slot):
        u_s[slot] = _dot(ext_s[...], wu_ref[j])

    def glu(j, slot):
        cw = cw_ref[j]
        c = (u_s[slot, SUBLANES - 1:SUBLANES - 1 + tm, :] * cw[0:1, :]
             + u_s[slot, SUBLANES:SUBLANES + tm, :] * cw[1:2, :]
             + u_s[slot, SUBLANES + 1:SUBLANES + 1 + tm, :] * cw[2:3, :] + cb_ref[j])
        act = (jax.nn.gelu(c[:, :FFN_FC]) * c[:, FFN_FC:]).astype(BF16)
        act_s[:, pl.ds(pl.multiple_of(j * FFN_FC, FFN_FC), FFN_FC)] = act

    up(0, 0)

    def pair(jj, carry):
        j = 2 * jj
        up(j + 1, 1)
        glu(j, 0)
        up(j + 2, 0)
        glu(j + 1, 1)
        return carry

    npairs = (n - 1) // 2
    lax.fori_loop(0, npairs, pair, 0)
    if n % 2 == 0:
        up(n - 1, 1)
        glu(n - 2, 0)
        glu(n - 1, 1)
    else:
        glu(n - 1, 0)
    ffn = _dot(act_s[...], wd_ref[...])
    out_ref[...] = _ln(DN_ALPHA * h_ref[...] + ffn, g2_ref[...], b2_ref[...])


def _ffn(h1, wu, cw, cb, wd, g2, b2, S):
    N = h1.shape[0]
    tm = FFN_TM
    nb = tm // SUBLANES
    last_blk = N // SUBLANES - 1
    row = lambda i: (i, 0)
    full = lambda a: pl.BlockSpec(a.shape, lambda i: (0,) * a.ndim)
    return pl.pallas_call(
        functools.partial(_ffn_kernel, tiles_per_seq=S // tm),
        grid=(N // tm,),
        in_specs=[pl.BlockSpec((tm, D_MODEL), row),
                  pl.BlockSpec((SUBLANES, D_MODEL), lambda i: (jnp.maximum(i * nb - 1, 0), 0)),
                  pl.BlockSpec((SUBLANES, D_MODEL), lambda i: (jnp.minimum((i + 1) * nb, last_blk), 0)),
                  full(wu), full(cw), full(cb), full(wd), full(g2), full(b2)],
        out_specs=pl.BlockSpec((tm, D_MODEL), row),
        out_shape=jax.ShapeDtypeStruct((N, D_MODEL), F32),
        scratch_shapes=[pltpu.VMEM((tm + 2 * SUBLANES, D_MODEL), BF16),
                        pltpu.VMEM((2, tm + 2 * SUBLANES, 2 * FFN_FC), F32),
                        pltpu.VMEM((tm, D_FF), BF16)],
        compiler_params=pltpu.CompilerParams(dimension_semantics=("arbitrary",),
                                             vmem_limit_bytes=VMEM_LIMIT),
        name="ffn",
    )(h1, h1, h1, wu, cw, cb, wd, g2, b2)


def _swap_halves(w):
    half = MLA_ROPE // 2
    return jnp.concatenate([w[..., half:], w[..., :half]], axis=-1)


def kernel(x, positions, ln_in_g, ln_in_b, w_in, lb_fwd, lb_bwd, hg_norm_g, q_a_norm_g, w_q_b,
           kv_a_norm_g, w_kv_b, attn_norm_g, w_out, ln1_g, ln1_b, w_up, conv_w, conv_b, w_down,
           ln2_g, ln2_b):
    B, S, D = x.shape
    assert D == D_MODEL and w_in.shape[0] == 1 and lb_fwd.shape == (2, HG_W)
    assert S % HG_CHUNK == 0 and S % ATT_TQ == 0 and S % FFN_TM == 0 and D_FF % FFN_FC == 0
    N = B * S
    r1 = lambda a: a.reshape(1, -1)

    w_in0 = w_in[0]
    c_lat = 5 * HG_W
    c_kr = c_lat + MLA_Q_RANK + MLA_KV_RANK
    w_hg = w_in0[:, :c_lat].astype(BF16)
    w_kr = w_in0[:, c_kr:]
    w_lat = jnp.concatenate([w_in0[:, c_lat:c_kr], w_kr, _swap_halves(w_kr)], axis=1).astype(BF16)
    wq = w_q_b[0].reshape(MLA_Q_RANK, MLA_HEADS, MLA_NOPE + MLA_ROPE)
    wq_pe = wq[:, :, MLA_NOPE:]
    wqb = jnp.concatenate(
        [wq[:, :, :MLA_NOPE].reshape(MLA_Q_RANK, -1),
         jnp.concatenate([wq_pe, _swap_halves(wq_pe)], axis=-1).reshape(MLA_Q_RANK, -1)],
        axis=1).astype(BF16)
    wkvb = w_kv_b[0].astype(BF16)
    half = MLA_ROPE // 2
    inv_freq = 1.0 / (ROPE_THETA ** (jnp.arange(half, dtype=F32) / half))
    zeros = jnp.zeros((LANES - MLA_ROPE,), F32)
    invf = r1(jnp.concatenate([inv_freq, inv_freq, zeros]))
    sgn = r1(jnp.concatenate([-jnp.ones((half,), F32), jnp.ones((half,), F32), zeros]))
    nfc = D_FF // FFN_FC
    glu = lambda a: jnp.concatenate(
        [a[..., :D_FF].reshape(a.shape[:-1] + (nfc, FFN_FC)),
         a[..., D_FF:].reshape(a.shape[:-1] + (nfc, FFN_FC))], axis=-1)
    wu = jnp.transpose(glu(w_up[0]), (1, 0, 2)).astype(BF16)
    cw = jnp.transpose(glu(conv_w[0]), (1, 0, 2))
    cb = glu(conv_b[0]).reshape(nfc, 1, 2 * FFN_FC)
    wd = w_down[0].astype(BF16)

    x2 = x.reshape(N, D)
    pos2 = positions.reshape(N, 1)
    q, ff, fb, gate, v, qa, ka, va = _proj(
        x2, pos2, r1(ln_in_g), r1(ln_in_b), w_hg, w_lat, lb_fwd, lb_bwd,
        r1(q_a_norm_g[0]), wqb, r1(kv_a_norm_g[0]), wkvb, invf, sgn, B, S)
    hg = _hgrn(q, ff, fb, v, gate, r1(hg_norm_g[0]), B, S)
    at = _attn(qa, ka, va, r1(attn_norm_g[0]), B, S)
    h1 = _mix(x2, hg, at, r1(ln_in_g), r1(ln_in_b), w_out[0].astype(BF16), r1(ln1_g[0]), r1(ln1_b[0]))
    out = _ffn(h1, wu, cw, cb, wd, r1(ln2_g[0]), r1(ln2_b[0]), S)
    return out.reshape(B, S, D)
```

```python
import functools

import jax
import jax.numpy as jnp
from jax import lax
from jax.experimental import pallas as pl
from jax.experimental.pallas import tpu as pltpu

D_MODEL = 1024
HG_HEADS = 4
HG_DIM = 128
HG_W = HG_HEADS * HG_DIM
MLA_HEADS = 4
MLA_Q_RANK = 384
MLA_KV_RANK = 128
MLA_NOPE = 128
MLA_ROPE = 64
MLA_VHEAD = 128
MLA_QK_PAD = 256
ROPE_THETA = 10000.0
D_FF = 2816
DN_ALPHA = 2.0 ** 0.25
NORM_EPS = 1e-5
LOG2E = 1.4426950408889634

LANES = 128
SUBLANES = 8
VMEM_LIMIT = 56 * 1024 * 1024

PROJ_TM = 512
HG_CHUNK = 128
HG_BLOCKS = HG_CHUNK // SUBLANES
HG_UNROLL = 8
ATT_TQ = 512
ATT_SUB = 32
ATT_ROW_BLOCKS = 1
FFN_TM = 512
FFN_FC = 256
FFN_ROW_BLOCKS = 4

F32 = jnp.float32
BF16 = jnp.bfloat16


def _ln(x, g, b):
    mu = jnp.mean(x, axis=-1, keepdims=True)
    xc = x - mu
    var = jnp.mean(xc * xc, axis=-1, keepdims=True)
    return xc * lax.rsqrt(var + NORM_EPS) * g + b


def _rms(x, g):
    return x * lax.rsqrt(jnp.mean(x * x, axis=-1, keepdims=True) + NORM_EPS) * g


def _dot(a, b):
    return jnp.dot(a, b, preferred_element_type=F32)


def _dot_nt(a, b):
    return lax.dot_general(a, b, (((1,), (1,)), ((), ())), preferred_element_type=F32)


def _proj_kernel(x_ref, pos_ref, lng_ref, lnb_ref, w_hg_ref, w_lat_ref, lbf_ref, lbb_ref,
                 qan_ref, wqb_ref, kvn_ref, wkvb_ref, invf_ref, sgn_ref,
                 q_ref, ff_ref, fb_ref, gate_ref, v_ref, qa_ref, ka_ref, va_ref):
    h = _ln(x_ref[...], lng_ref[...], lnb_ref[...]).astype(BF16)

    def lower_bound(ref):
        r0, r1 = ref[0:1, :], ref[1:2, :]
        m = jnp.maximum(r0, r1)
        e0, e1 = jnp.exp(r0 - m), jnp.exp(r1 - m)
        return e0 / (e0 + e1)

    def hg_cols(j):
        return _dot(h, w_hg_ref[:, j * HG_W:(j + 1) * HG_W])

    q_ref[...] = jax.nn.silu(hg_cols(0))
    v_ref[...] = hg_cols(1).astype(BF16)
    lbf = lower_bound(lbf_ref)
    ff_ref[...] = lbf + (1.0 - lbf) * jax.nn.sigmoid(hg_cols(2))
    lbb = lower_bound(lbb_ref)
    fb_ref[...] = lbb + (1.0 - lbb) * jax.nn.sigmoid(hg_cols(3))
    gate_ref[...] = hg_cols(4)

    ang = pos_ref[...].astype(F32) * invf_ref[...]
    sgn = sgn_ref[...]
    cos_t = jnp.cos(ang) * jnp.abs(sgn)
    sin_t = jnp.sin(ang) * sgn

    def rope(grp):
        return grp * cos_t + pltpu.roll(grp, MLA_ROPE, 1) * sin_t

    lat = _dot(h, w_lat_ref[...])
    qa = _rms(lat[:, :MLA_Q_RANK], qan_ref[...]).astype(BF16)
    kva = _rms(lat[:, MLA_Q_RANK:MLA_Q_RANK + MLA_KV_RANK], kvn_ref[...]).astype(BF16)
    k_pe = rope(lat[:, MLA_Q_RANK + MLA_KV_RANK:]).astype(BF16)

    scale = (MLA_NOPE + MLA_ROPE) ** -0.5 * LOG2E
    qh = _dot(qa, wqb_ref[...])
    kvh = _dot(kva, wkvb_ref[...])
    for hd in range(MLA_HEADS):
        qa_ref[0, hd, :, 0:LANES] = (qh[:, hd * LANES:(hd + 1) * LANES] * scale).astype(BF16)
        pe = rope(qh[:, (MLA_HEADS + hd) * LANES:(MLA_HEADS + hd + 1) * LANES])
        qa_ref[0, hd, :, LANES:2 * LANES] = (pe * scale).astype(BF16)
        ka_ref[0, hd, :, 0:LANES] = kvh[:, 2 * hd * LANES:(2 * hd + 1) * LANES].astype(BF16)
        ka_ref[0, hd, :, LANES:2 * LANES] = k_pe
        va_ref[0, hd, :, :] = kvh[:, (2 * hd + 1) * LANES:(2 * hd + 2) * LANES].astype(BF16)


def _proj(x2, pos2, ln_g, ln_b, w_hg, w_lat, lb_f, lb_b, qan, wqb, kvn, wkvb, invf, sgn, B, S):
    N = B * S
    tm = PROJ_TM
    spb = S // tm
    row = lambda i: (i, 0)
    const = lambda i: (0, 0)
    head = lambda i: (i // spb, 0, i % spb, 0)
    full = lambda a: pl.BlockSpec(a.shape, const)
    out_shape = (
        jax.ShapeDtypeStruct((N, HG_W), F32),
        jax.ShapeDtypeStruct((N, HG_W), F32),
        jax.ShapeDtypeStruct((N, HG_W), F32),
        jax.ShapeDtypeStruct((N, HG_W), F32),
        jax.ShapeDtypeStruct((N, HG_W), BF16),
        jax.ShapeDtypeStruct((B, MLA_HEADS, S, MLA_QK_PAD), BF16),
        jax.ShapeDtypeStruct((B, MLA_HEADS, S, MLA_QK_PAD), BF16),
        jax.ShapeDtypeStruct((B, MLA_HEADS, S, MLA_VHEAD), BF16),
    )
    tok = pl.BlockSpec((tm, HG_W), row)
    return pl.pallas_call(
        _proj_kernel,
        grid=(N // tm,),
        in_specs=[pl.BlockSpec((tm, D_MODEL), row), pl.BlockSpec((tm, 1), row),
                  full(ln_g), full(ln_b), full(w_hg), full(w_lat), full(lb_f), full(lb_b),
                  full(qan), full(wqb), full(kvn), full(wkvb), full(invf), full(sgn)],
        out_specs=(tok, tok, tok, tok, tok,
                   pl.BlockSpec((1, MLA_HEADS, tm, MLA_QK_PAD), head),
                   pl.BlockSpec((1, MLA_HEADS, tm, MLA_QK_PAD), head),
                   pl.BlockSpec((1, MLA_HEADS, tm, MLA_VHEAD), head)),
        out_shape=out_shape,
        compiler_params=pltpu.CompilerParams(dimension_semantics=("arbitrary",),
                                             vmem_limit_bytes=VMEM_LIMIT),
        name="proj",
    )(x2, pos2, ln_g, ln_b, w_hg, w_lat, lb_f, lb_b, qan, wqb, kvn, wkvb, invf, sgn)


def _blocks(x):
    return [x[i * SUBLANES:(i + 1) * SUBLANES] for i in range(x.shape[0] // SUBLANES)]


def _rows(blocks):
    return jnp.concatenate(blocks, axis=0)


def _hgrn_chunk(q, f, v, lmap_ref, rev):
    C = HG_CHUNK
    g = jnp.log(f) * LOG2E
    k = 1.0 - f
    row = lax.broadcasted_iota(jnp.int32, (C, HG_DIM), 0)

    def roll8(x, shift):
        return _rows([pltpu.roll(blk, shift, 0) for blk in _blocks(x)])

    P, T = g, g
    A = None
    for lvl in range(3):
        b = 1 << lvl
        bit = ((row >> lvl) & 1) == 1
        later = jnp.logical_not(bit) if rev else bit
        w = jnp.exp2(jnp.where(later, P, T - P))
        a_l = _dot_nt((q * w).astype(BF16), (k * w).astype(BF16))
        hit = lmap_ref[...] == lvl
        A = jnp.where(hit, a_l, 0.0) if A is None else jnp.where(hit, a_l, A)
        sib = roll8(T, b) if 2 * b == SUBLANES else jnp.where(bit, roll8(T, b), roll8(T, SUBLANES - b))
        P = P + jnp.where(later, sib, 0.0)
        T = T + sib

    Pb, Tb, qb, kb, Ab = _blocks(P), _blocks(T), _blocks(q), _blocks(k), _blocks(A)
    zero = jnp.zeros((SUBLANES, HG_DIM), F32)
    lvl, m = 3, 1
    while m < HG_BLOCKS:
        later = [(((i // m) & 1) == 1) != rev for i in range(HG_BLOCKS)]
        wb = [jnp.exp2(Pb[i] if later[i] else Tb[i] - Pb[i]) for i in range(HG_BLOCKS)]
        qt = _rows([qb[i] * wb[i] for i in range(HG_BLOCKS) if later[i]]).astype(BF16)
        kt = _rows([zero if later[i] else kb[i] * wb[i] for i in range(HG_BLOCKS)]).astype(BF16)
        a_l = _blocks(_dot_nt(qt, kt))
        n_later = 0
        for i in range(HG_BLOCKS):
            if later[i]:
                hit = lmap_ref[i * SUBLANES:(i + 1) * SUBLANES, :] == lvl
                Ab[i] = jnp.where(hit, a_l[n_later], Ab[i])
                n_later += 1
        sib = [Tb[i ^ m] for i in range(HG_BLOCKS)]
        Pb = [Pb[i] + sib[i] if later[i] else Pb[i] for i in range(HG_BLOCKS)]
        tot = {}
        for i in range(HG_BLOCKS):
            key = i // (2 * m)
            if key not in tot:
                tot[key] = Tb[i] + sib[i]
        Tb = [tot[i // (2 * m)] for i in range(HG_BLOCKS)]
        lvl, m = lvl + 1, 2 * m

    vf = v.astype(F32)
    o = _dot(_rows(Ab).astype(BF16), v) + jnp.sum(q * k, axis=1, keepdims=True) * vf
    qt = _rows([qb[i] * jnp.exp2(Pb[i]) for i in range(HG_BLOCKS)]).astype(BF16)
    kt = _rows([kb[i] * jnp.exp2(Tb[i] - Pb[i]) for i in range(HG_BLOCKS)]).astype(BF16)
    return o, qt, kt, jnp.exp2(Tb[0])


def _hgrn_kernel(q_ref, ff_ref, fb_ref, v_ref, gate_ref, ng_ref, out_ref, o_acc, lmap_s):
    S = q_ref.shape[0]
    C = HG_CHUNK
    n = S // C

    ri = lax.broadcasted_iota(jnp.int32, (C, C), 0)
    ci = lax.broadcasted_iota(jnp.int32, (C, C), 1)
    x = ri ^ ci
    level = jnp.full((C, C), -1, jnp.int32)
    b = 1
    while b < C:
        level = level + jnp.where(x >= b, 1, 0)
        b *= 2
    lmap_s[0] = jnp.where(ci < ri, level, -1)
    lmap_s[1] = jnp.where(ci > ri, level, -1)

    def visit(c, f_ref, st, rev, first):
        r = pl.multiple_of(c * C, C)
        q = q_ref[pl.ds(r, C), :]
        v = v_ref[pl.ds(r, C), :]
        o, qt, kt, d = _hgrn_chunk(q, f_ref[pl.ds(r, C), :], v, lmap_s.at[1 if rev else 0], rev)
        o = o + _dot_nt(qt, st.astype(BF16))
        st = st * d[0:1, :] + _dot(v.astype(F32).T.astype(BF16), kt)
        if first:
            o_acc[pl.ds(r, C), :] = o
        else:
            o = _rms(o + o_acc[pl.ds(r, C), :], ng_ref[...])
            out_ref[pl.ds(r, C), :] = (o * jax.nn.silu(gate_ref[pl.ds(r, C), :])).astype(BF16)
        return st

    def step(first):
        def body(i, carry):
            st_f, st_b = carry
            return (visit(i, ff_ref, st_f, False, first),
                    visit(n - 1 - i, fb_ref, st_b, True, first))
        return body

    z = jnp.zeros((HG_DIM, HG_DIM), F32)
    carry = lax.fori_loop(0, n // 2, step(True), (z, z), unroll=HG_UNROLL)
    lax.fori_loop(n // 2, n, step(False), carry, unroll=HG_UNROLL)


def _hgrn(q, ff, fb, v, gate, ng, B, S):
    N = B * S
    C = HG_CHUNK
    assert (S // C) % 2 == 0
    blk = pl.BlockSpec((S, HG_DIM), lambda b, h: (b, h))
    return pl.pallas_call(
        _hgrn_kernel,
        grid=(B, HG_HEADS),
        in_specs=[blk, blk, blk, blk, blk, pl.BlockSpec((1, HG_DIM), lambda b, h: (0, 0))],
        out_specs=blk,
        out_shape=jax.ShapeDtypeStruct((N, HG_W), BF16),
        scratch_shapes=[pltpu.VMEM((S, HG_DIM), F32),
                        pltpu.VMEM((2, C, C), jnp.int32)],
        compiler_params=pltpu.CompilerParams(dimension_semantics=("arbitrary", "arbitrary"),
                                             vmem_limit_bytes=VMEM_LIMIT),
        name="hgrn",
    )(q, ff, fb, v, gate, ng)


def _attn_kernel(q_ref, k_ref, v_ref, ng_ref, out_ref, s_s, p_s, l_s):
    S = k_ref.shape[2]
    tq = ATT_TQ
    nq = S // tq

    rb_rows = tq // ATT_ROW_BLOCKS

    def scores(i, slot, rb):
        r0 = rb * rb_rows
        s_s[slot, r0:r0 + rb_rows, :] = _dot_nt(
            q_ref[0, 0, i * tq + r0:i * tq + r0 + rb_rows, :], k_ref[0, 0])

    def softmax(slot, rb):
        for r in range(rb * rb_rows, (rb + 1) * rb_rows, ATT_SUB):
            s = s_s[slot, r:r + ATT_SUB, :]
            p = jnp.exp2(s - jnp.max(s, axis=-1, keepdims=True))
            l_s[slot, r:r + ATT_SUB, :] = jnp.sum(p, axis=-1, keepdims=True)
            p_s[slot, r:r + ATT_SUB, :] = p.astype(BF16)

    def values(i, slot, rb):
        r0 = rb * rb_rows
        o = _dot(p_s[slot, r0:r0 + rb_rows, :], v_ref[0, 0]) / l_s[slot, r0:r0 + rb_rows, :]
        out_ref[i * tq + r0:i * tq + r0 + rb_rows, :] = _rms(o, ng_ref[...]).astype(BF16)

    for rb in range(ATT_ROW_BLOCKS):
        scores(0, 0, rb)
    for i in range(nq):
        for rb in range(ATT_ROW_BLOCKS):
            if i + 1 < nq:
                scores(i + 1, (i + 1) % 2, rb)
            softmax(i % 2, rb)
            values(i, i % 2, rb)


def _attn(qa, ka, va, ng, B, S):
    tq = ATT_TQ
    return pl.pallas_call(
        _attn_kernel,
        grid=(B, MLA_HEADS),
        in_specs=[pl.BlockSpec((1, 1, S, MLA_QK_PAD), lambda b, h: (b, h, 0, 0)),
                  pl.BlockSpec((1, 1, S, MLA_QK_PAD), lambda b, h: (b, h, 0, 0)),
                  pl.BlockSpec((1, 1, S, MLA_VHEAD), lambda b, h: (b, h, 0, 0)),
                  pl.BlockSpec((1, MLA_VHEAD), lambda b, h: (0, 0))],
        out_specs=pl.BlockSpec((S, MLA_VHEAD), lambda b, h: (b, h)),
        out_shape=jax.ShapeDtypeStruct((B * S, MLA_HEADS * MLA_VHEAD), BF16),
        scratch_shapes=[pltpu.VMEM((2, tq, S), F32),
                        pltpu.VMEM((2, tq, S), BF16),
                        pltpu.VMEM((2, tq, 1), F32)],
        compiler_params=pltpu.CompilerParams(
            dimension_semantics=("arbitrary", "arbitrary"),
            vmem_limit_bytes=VMEM_LIMIT),
        name="attn",
    )(qa, ka, va, ng)


def _mix_kernel(x_ref, hg_ref, at_ref, lng_ref, lnb_ref, wo_ref, g1_ref, b1_ref, out_ref):
    h = _ln(x_ref[...], lng_ref[...], lnb_ref[...])
    mix = _dot(hg_ref[...], wo_ref[0:HG_W, :]) + _dot(at_ref[...], wo_ref[HG_W:, :])
    out_ref[...] = _ln(DN_ALPHA * h + mix, g1_ref[...], b1_ref[...])


def _mix(x2, hg, at, ln_g, ln_b, wo, g1, b1):
    N = x2.shape[0]
    tm = PROJ_TM
    row = lambda i: (i, 0)
    const = lambda i: (0, 0)
    full = lambda a: pl.BlockSpec(a.shape, const)
    return pl.pallas_call(
        _mix_kernel,
        grid=(N // tm,),
        in_specs=[pl.BlockSpec((tm, D_MODEL), row), pl.BlockSpec((tm, HG_W), row),
                  pl.BlockSpec((tm, HG_W), row), full(ln_g), full(ln_b), full(wo),
                  full(g1), full(b1)],
        out_specs=pl.BlockSpec((tm, D_MODEL), row),
        out_shape=jax.ShapeDtypeStruct((N, D_MODEL), F32),
        compiler_params=pltpu.CompilerParams(dimension_semantics=("arbitrary",),
                                             vmem_limit_bytes=VMEM_LIMIT),
        name="mix",
    )(x2, hg, at, ln_g, ln_b, wo, g1, b1)


def _ffn_kernel(h_ref, hp_ref, hn_ref, wu_ref, cw_ref, cb_ref, wd_ref, g2_ref, b2_ref, out_ref,
                il_s, ext_s, u_s, act_s, acc_s, *, tiles_per_seq):
    tm = FFN_TM
    grp = tm // SUBLANES
    n = wu_ref.shape[0]
    i = pl.program_id(0)
    has_prev = jnp.where(i % tiles_per_seq != 0, 1.0, 0.0)
    has_next = jnp.where(i % tiles_per_seq != tiles_per_seq - 1, 1.0, 0.0)

    ngrp = D_MODEL // LANES
    for c in range(ngrp):
        for a in range(SUBLANES):
            il_s[c, pl.ds(a, grp, stride=SUBLANES), :] = h_ref[a * grp:(a + 1) * grp,
                                                               c * LANES:(c + 1) * LANES]
        ext_s[0:tm, c * LANES:(c + 1) * LANES] = il_s[c].astype(BF16)
    sub = lax.broadcasted_iota(jnp.int32, (2 * SUBLANES, D_MODEL), 0)
    halo = jnp.where(sub == 0, hp_ref[SUBLANES - 1:SUBLANES, :] * has_prev,
                     jnp.where(sub == 1, hn_ref[0:1, :] * has_next, 0.0))
    ext_s[tm:, :] = halo.astype(BF16)
    acc_s[...] = jnp.zeros_like(acc_s)

    rb_rows = tm // FFN_ROW_BLOCKS

    def up(j, slot, rb):
        r0 = rb * rb_rows
        r1 = tm + 2 * SUBLANES if rb == FFN_ROW_BLOCKS - 1 else r0 + rb_rows
        u_s[slot, r0:r1, :] = _dot(ext_s[r0:r1, :], wu_ref[j])

    def glu(j, slot, rb):
        r0 = rb * rb_rows
        r1 = r0 + rb_rows
        cw = cw_ref[j]
        sub8 = lax.broadcasted_iota(jnp.int32, (SUBLANES, 2 * FFN_FC), 0)
        if rb == 0:
            first = jnp.where(sub8 == 0, u_s[slot, tm:tm + 1, :],
                              pltpu.roll(u_s[slot, tm - SUBLANES:tm, :], 1, 0))
            prev = jnp.concatenate([first, u_s[slot, 0:r1 - SUBLANES, :]], axis=0)
        else:
            prev = u_s[slot, r0 - SUBLANES:r1 - SUBLANES, :]
        if rb == FFN_ROW_BLOCKS - 1:
            last = jnp.where(sub8 == SUBLANES - 1, u_s[slot, tm + 1:tm + 2, :],
                             pltpu.roll(u_s[slot, 0:SUBLANES, :], SUBLANES - 1, 0))
            nxt = jnp.concatenate([u_s[slot, r0 + SUBLANES:tm, :], last], axis=0)
        else:
            nxt = u_s[slot, r0 + SUBLANES:r1 + SUBLANES, :]
        c = prev * cw[0:1, :] + u_s[slot, r0:r1, :] * cw[1:2, :] + nxt * cw[2:3, :] + cb_ref[j]
        act_s[slot, r0:r1, :] = (jax.nn.gelu(c[:, :FFN_FC]) * c[:, FFN_FC:]).astype(BF16)

    def down(j, slot, rb):
        r0 = rb * rb_rows
        r1 = r0 + rb_rows
        part = _dot(act_s[slot, r0:r1, :], wd_ref[j])
        for c in range(ngrp):
            acc_s[c, r0:r1, :] += part[:, c * LANES:(c + 1) * LANES]

    def stage(t, par):
        static = isinstance(t, int)
        for rb in range(FFN_ROW_BLOCKS):
            if not static or t + 1 < n:
                up(t + 1, 1 - par, rb)
            if not static or 0 <= t < n:
                glu(t, par, rb)
            if not static or 1 <= t <= n:
                down(t - 1, 1 - par, rb)

    stage(-1, 1)
    stage(0, 0)
    npairs = (n - 2) // 2

    def pair(jj, carry):
        t = 2 * jj + 1
        stage(t, 1)
        stage(t + 1, 0)
        return carry

    lax.fori_loop(0, npairs, pair, 0)
    for t in range(2 * npairs + 1, n + 1):
        stage(t, t % 2)

    for a in range(SUBLANES):
        ffn = jnp.concatenate([acc_s[c, pl.ds(a, grp, stride=SUBLANES), :] for c in range(ngrp)],
                              axis=1)
        rows = slice(a * grp, (a + 1) * grp)
        out_ref[rows, :] = _ln(DN_ALPHA * h_ref[rows, :] + ffn, g2_ref[...], b2_ref[...])


def _ffn(h1, wu, cw, cb, wd, g2, b2, S):
    N = h1.shape[0]
    tm = FFN_TM
    nb = tm // SUBLANES
    last_blk = N // SUBLANES - 1
    row = lambda i: (i, 0)
    full = lambda a: pl.BlockSpec(a.shape, lambda i: (0,) * a.ndim)
    return pl.pallas_call(
        functools.partial(_ffn_kernel, tiles_per_seq=S // tm),
        grid=(N // tm,),
        in_specs=[pl.BlockSpec((tm, D_MODEL), row),
                  pl.BlockSpec((SUBLANES, D_MODEL), lambda i: (jnp.maximum(i * nb - 1, 0), 0)),
                  pl.BlockSpec((SUBLANES, D_MODEL), lambda i: (jnp.minimum((i + 1) * nb, last_blk), 0)),
                  full(wu), full(cw), full(cb), full(wd), full(g2), full(b2)],
        out_specs=pl.BlockSpec((tm, D_MODEL), row),
        out_shape=jax.ShapeDtypeStruct((N, D_MODEL), F32),
        scratch_shapes=[pltpu.VMEM((D_MODEL // LANES, tm, LANES), F32),
                        pltpu.VMEM((tm + 2 * SUBLANES, D_MODEL), BF16),
                        pltpu.VMEM((2, tm + 2 * SUBLANES, 2 * FFN_FC), F32),
                        pltpu.VMEM((2, tm, FFN_FC), BF16),
                        pltpu.VMEM((D_MODEL // LANES, tm, LANES), F32)],
        compiler_params=pltpu.CompilerParams(dimension_semantics=("arbitrary",),
                                             vmem_limit_bytes=VMEM_LIMIT),
        name="ffn",
    )(h1, h1, h1, wu, cw, cb, wd, g2, b2)


def _swap_halves(w):
    half = MLA_ROPE // 2
    return jnp.concatenate([w[..., half:], w[..., :half]], axis=-1)


def kernel(x, positions, ln_in_g, ln_in_b, w_in, lb_fwd, lb_bwd, hg_norm_g, q_a_norm_g, w_q_b,
           kv_a_norm_g, w_kv_b, attn_norm_g, w_out, ln1_g, ln1_b, w_up, conv_w, conv_b, w_down,
           ln2_g, ln2_b):
    B, S, D = x.shape
    assert D == D_MODEL and w_in.shape[0] == 1 and lb_fwd.shape == (2, HG_W)
    assert S % HG_CHUNK == 0 and S % ATT_TQ == 0 and S % FFN_TM == 0 and D_FF % FFN_FC == 0
    N = B * S
    r1 = lambda a: a.reshape(1, -1)

    w_in0 = w_in[0]
    c_lat = 5 * HG_W
    c_kr = c_lat + MLA_Q_RANK + MLA_KV_RANK
    w_hg = w_in0[:, :c_lat].astype(BF16)
    w_kr = w_in0[:, c_kr:]
    w_lat = jnp.concatenate([w_in0[:, c_lat:c_kr], w_kr, _swap_halves(w_kr)], axis=1).astype(BF16)
    wq = w_q_b[0].reshape(MLA_Q_RANK, MLA_HEADS, MLA_NOPE + MLA_ROPE)
    wq_pe = wq[:, :, MLA_NOPE:]
    wqb = jnp.concatenate(
        [wq[:, :, :MLA_NOPE].reshape(MLA_Q_RANK, -1),
         jnp.concatenate([wq_pe, _swap_halves(wq_pe)], axis=-1).reshape(MLA_Q_RANK, -1)],
        axis=1).astype(BF16)
    wkvb = w_kv_b[0].astype(BF16)
    half = MLA_ROPE // 2
    inv_freq = 1.0 / (ROPE_THETA ** (jnp.arange(half, dtype=F32) / half))
    zeros = jnp.zeros((LANES - MLA_ROPE,), F32)
    invf = r1(jnp.concatenate([inv_freq, inv_freq, zeros]))
    sgn = r1(jnp.concatenate([-jnp.ones((half,), F32), jnp.ones((half,), F32), zeros]))
    nfc = D_FF // FFN_FC
    glu = lambda a: jnp.concatenate(
        [a[..., :D_FF].reshape(a.shape[:-1] + (nfc, FFN_FC)),
         a[..., D_FF:].reshape(a.shape[:-1] + (nfc, FFN_FC))], axis=-1)
    wu = jnp.transpose(glu(w_up[0]), (1, 0, 2)).astype(BF16)
    cw = jnp.transpose(glu(conv_w[0]), (1, 0, 2))
    cb = glu(conv_b[0]).reshape(nfc, 1, 2 * FFN_FC)
    wd = w_down[0].reshape(nfc, FFN_FC, D_MODEL).astype(BF16)

    x2 = x.reshape(N, D)
    pos2 = positions.reshape(N, 1)
    q, ff, fb, gate, v, qa, ka, va = _proj(
        x2, pos2, r1(ln_in_g), r1(ln_in_b), w_hg, w_lat, lb_fwd, lb_bwd,
        r1(q_a_norm_g[0]), wqb, r1(kv_a_norm_g[0]), wkvb, invf, sgn, B, S)
    hg = _hgrn(q, ff, fb, v, gate, r1(hg_norm_g[0]), B, S)
    at = _attn(qa, ka, va, r1(attn_norm_g[0]), B, S)
    h1 = _mix(x2, hg, at, r1(ln_in_g), r1(ln_in_b), w_out[0].astype(BF16), r1(ln1_g[0]), r1(ln1_b[0]))
    out = _ffn(h1, wu, cw, cb, wd, r1(ln2_g[0]), r1(ln2_b[0]), S)
    return out.reshape(B, S, D)
```

```python
import functools

import jax
import jax.numpy as jnp
from jax import lax
from jax.experimental import pallas as pl
from jax.experimental.pallas import tpu as pltpu

D_MODEL = 1024
HG_HEADS = 4
HG_DIM = 128
HG_W = HG_HEADS * HG_DIM
MLA_HEADS = 4
MLA_Q_RANK = 384
MLA_KV_RANK = 128
MLA_NOPE = 128
MLA_ROPE = 64
MLA_VHEAD = 128
MLA_QK_PAD = 256
ROPE_THETA = 10000.0
D_FF = 2816
DN_ALPHA = 2.0 ** 0.25
NORM_EPS = 1e-5
LOG2E = 1.4426950408889634

LANES = 128
SUBLANES = 8
VMEM_LIMIT = 56 * 1024 * 1024

PROJ_TM = 512
HG_CHUNK = 128
HG_BLOCKS = HG_CHUNK // SUBLANES
HG_UNROLL = 8
ATT_TQ = 512
ATT_SUB = 32
ATT_ROW_BLOCKS = 1
FFN_TM = 512
FFN_FC = 256

F32 = jnp.float32
BF16 = jnp.bfloat16


def _ln(x, g, b):
    mu = jnp.mean(x, axis=-1, keepdims=True)
    xc = x - mu
    var = jnp.mean(xc * xc, axis=-1, keepdims=True)
    return xc * lax.rsqrt(var + NORM_EPS) * g + b


def _rms(x, g):
    return x * lax.rsqrt(jnp.mean(x * x, axis=-1, keepdims=True) + NORM_EPS) * g


def _dot(a, b):
    return jnp.dot(a, b, preferred_element_type=F32)


def _dot_nt(a, b):
    return lax.dot_general(a, b, (((1,), (1,)), ((), ())), preferred_element_type=F32)


def _proj_kernel(x_ref, pos_ref, lng_ref, lnb_ref, w_hg_ref, w_lat_ref, lbf_ref, lbb_ref,
                 qan_ref, wqb_ref, kvn_ref, wkvb_ref, invf_ref, sgn_ref,
                 q_ref, ff_ref, fb_ref, gate_ref, v_ref, qa_ref, ka_ref, va_ref):
    h = _ln(x_ref[...], lng_ref[...], lnb_ref[...]).astype(BF16)

    def lower_bound(ref):
        r0, r1 = ref[0:1, :], ref[1:2, :]
        m = jnp.maximum(r0, r1)
        e0, e1 = jnp.exp(r0 - m), jnp.exp(r1 - m)
        return e0 / (e0 + e1)

    def hg_cols(j):
        return _dot(h, w_hg_ref[:, j * HG_W:(j + 1) * HG_W])

    q_ref[...] = jax.nn.silu(hg_cols(0))
    v_ref[...] = hg_cols(1).astype(BF16)
    lbf = lower_bound(lbf_ref)
    ff_ref[...] = lbf + (1.0 - lbf) * jax.nn.sigmoid(hg_cols(2))
    lbb = lower_bound(lbb_ref)
    fb_ref[...] = lbb + (1.0 - lbb) * jax.nn.sigmoid(hg_cols(3))
    gate_ref[...] = hg_cols(4)

    ang = pos_ref[...].astype(F32) * invf_ref[...]
    sgn = sgn_ref[...]
    cos_t = jnp.cos(ang) * jnp.abs(sgn)
    sin_t = jnp.sin(ang) * sgn

    def rope(grp):
        return grp * cos_t + pltpu.roll(grp, MLA_ROPE, 1) * sin_t

    lat = _dot(h, w_lat_ref[...])
    qa = _rms(lat[:, :MLA_Q_RANK], qan_ref[...]).astype(BF16)
    kva = _rms(lat[:, MLA_Q_RANK:MLA_Q_RANK + MLA_KV_RANK], kvn_ref[...]).astype(BF16)
    k_pe = rope(lat[:, MLA_Q_RANK + MLA_KV_RANK:]).astype(BF16)

    scale = (MLA_NOPE + MLA_ROPE) ** -0.5 * LOG2E
    qh = _dot(qa, wqb_ref[...])
    kvh = _dot(kva, wkvb_ref[...])
    for hd in range(MLA_HEADS):
        qa_ref[0, hd, :, 0:LANES] = (qh[:, hd * LANES:(hd + 1) * LANES] * scale).astype(BF16)
        pe = rope(qh[:, (MLA_HEADS + hd) * LANES:(MLA_HEADS + hd + 1) * LANES])
        qa_ref[0, hd, :, LANES:2 * LANES] = (pe * scale).astype(BF16)
        ka_ref[0, hd, :, 0:LANES] = kvh[:, 2 * hd * LANES:(2 * hd + 1) * LANES].astype(BF16)
        ka_ref[0, hd, :, LANES:2 * LANES] = k_pe
        va_ref[0, hd, :, :] = kvh[:, (2 * hd + 1) * LANES:(2 * hd + 2) * LANES].astype(BF16)


def _proj(x2, pos2, ln_g, ln_b, w_hg, w_lat, lb_f, lb_b, qan, wqb, kvn, wkvb, invf, sgn, B, S):
    N = B * S
    tm = PROJ_TM
    spb = S // tm
    row = lambda i: (i, 0)
    const = lambda i: (0, 0)
    head = lambda i: (i // spb, 0, i % spb, 0)
    full = lambda a: pl.BlockSpec(a.shape, const)
    out_shape = (
        jax.ShapeDtypeStruct((N, HG_W), F32),
        jax.ShapeDtypeStruct((N, HG_W), F32),
        jax.ShapeDtypeStruct((N, HG_W), F32),
        jax.ShapeDtypeStruct((N, HG_W), F32),
        jax.ShapeDtypeStruct((N, HG_W), BF16),
        jax.ShapeDtypeStruct((B, MLA_HEADS, S, MLA_QK_PAD), BF16),
        jax.ShapeDtypeStruct((B, MLA_HEADS, S, MLA_QK_PAD), BF16),
        jax.ShapeDtypeStruct((B, MLA_HEADS, S, MLA_VHEAD), BF16),
    )
    tok = pl.BlockSpec((tm, HG_W), row)
    return pl.pallas_call(
        _proj_kernel,
        grid=(N // tm,),
        in_specs=[pl.BlockSpec((tm, D_MODEL), row), pl.BlockSpec((tm, 1), row),
                  full(ln_g), full(ln_b), full(w_hg), full(w_lat), full(lb_f), full(lb_b),
                  full(qan), full(wqb), full(kvn), full(wkvb), full(invf), full(sgn)],
        out_specs=(tok, tok, tok, tok, tok,
                   pl.BlockSpec((1, MLA_HEADS, tm, MLA_QK_PAD), head),
                   pl.BlockSpec((1, MLA_HEADS, tm, MLA_QK_PAD), head),
                   pl.BlockSpec((1, MLA_HEADS, tm, MLA_VHEAD), head)),
        out_shape=out_shape,
        compiler_params=pltpu.CompilerParams(dimension_semantics=("arbitrary",),
                                             vmem_limit_bytes=VMEM_LIMIT),
        name="proj",
    )(x2, pos2, ln_g, ln_b, w_hg, w_lat, lb_f, lb_b, qan, wqb, kvn, wkvb, invf, sgn)


def _blocks(x):
    return [x[i * SUBLANES:(i + 1) * SUBLANES] for i in range(x.shape[0] // SUBLANES)]


def _rows(blocks):
    return jnp.concatenate(blocks, axis=0)


def _hgrn_chunk(q, f, v, lmap_ref, rev):
    C = HG_CHUNK
    g = jnp.log(f) * LOG2E
    k = 1.0 - f
    row = lax.broadcasted_iota(jnp.int32, (C, HG_DIM), 0)

    def roll8(x, shift):
        return _rows([pltpu.roll(blk, shift, 0) for blk in _blocks(x)])

    P, T = g, g
    A = None
    for lvl in range(3):
        b = 1 << lvl
        bit = ((row >> lvl) & 1) == 1
        later = jnp.logical_not(bit) if rev else bit
        w = jnp.exp2(jnp.where(later, P, T - P))
        a_l = _dot_nt((q * w).astype(BF16), (k * w).astype(BF16))
        hit = lmap_ref[...] == lvl
        A = jnp.where(hit, a_l, 0.0) if A is None else jnp.where(hit, a_l, A)
        sib = roll8(T, b) if 2 * b == SUBLANES else jnp.where(bit, roll8(T, b), roll8(T, SUBLANES - b))
        P = P + jnp.where(later, sib, 0.0)
        T = T + sib

    Pb, Tb, qb, kb, Ab = _blocks(P), _blocks(T), _blocks(q), _blocks(k), _blocks(A)
    zero = jnp.zeros((SUBLANES, HG_DIM), F32)
    lvl, m = 3, 1
    while m < HG_BLOCKS:
        later = [(((i // m) & 1) == 1) != rev for i in range(HG_BLOCKS)]
        wb = [jnp.exp2(Pb[i] if later[i] else Tb[i] - Pb[i]) for i in range(HG_BLOCKS)]
        qt = _rows([qb[i] * wb[i] for i in range(HG_BLOCKS) if later[i]]).astype(BF16)
        kt = _rows([zero if later[i] else kb[i] * wb[i] for i in range(HG_BLOCKS)]).astype(BF16)
        a_l = _blocks(_dot_nt(qt, kt))
        n_later = 0
        for i in range(HG_BLOCKS):
            if later[i]:
                hit = lmap_ref[i * SUBLANES:(i + 1) * SUBLANES, :] == lvl
                Ab[i] = jnp.where(hit, a_l[n_later], Ab[i])
                n_later += 1
        sib = [Tb[i ^ m] for i in range(HG_BLOCKS)]
        Pb = [Pb[i] + sib[i] if later[i] else Pb[i] for i in range(HG_BLOCKS)]
        tot = {}
        for i in range(HG_BLOCKS):
            key = i // (2 * m)
            if key not in tot:
                tot[key] = Tb[i] + sib[i]
        Tb = [tot[i // (2 * m)] for i in range(HG_BLOCKS)]
        lvl, m = lvl + 1, 2 * m

    vf = v.astype(F32)
    o = _dot(_rows(Ab).astype(BF16), v) + jnp.sum(q * k, axis=1, keepdims=True) * vf
    qt = _rows([qb[i] * jnp.exp2(Pb[i]) for i in range(HG_BLOCKS)]).astype(BF16)
    kt = _rows([kb[i] * jnp.exp2(Tb[i] - Pb[i]) for i in range(HG_BLOCKS)]).astype(BF16)
    return o, qt, kt, jnp.exp2(Tb[0])


def _hgrn_kernel(q_ref, ff_ref, fb_ref, v_ref, gate_ref, ng_ref, out_ref, o_acc, lmap_s):
    S = q_ref.shape[0]
    C = HG_CHUNK
    n = S // C

    ri = lax.broadcasted_iota(jnp.int32, (C, C), 0)
    ci = lax.broadcasted_iota(jnp.int32, (C, C), 1)
    x = ri ^ ci
    level = jnp.full((C, C), -1, jnp.int32)
    b = 1
    while b < C:
        level = level + jnp.where(x >= b, 1, 0)
        b *= 2
    lmap_s[0] = jnp.where(ci < ri, level, -1)
    lmap_s[1] = jnp.where(ci > ri, level, -1)

    def visit(c, f_ref, st, rev, first):
        r = pl.multiple_of(c * C, C)
        q = q_ref[pl.ds(r, C), :]
        v = v_ref[pl.ds(r, C), :]
        o, qt, kt, d = _hgrn_chunk(q, f_ref[pl.ds(r, C), :], v, lmap_s.at[1 if rev else 0], rev)
        o = o + _dot_nt(qt, st.astype(BF16))
        st = st * d[0:1, :] + _dot(v.astype(F32).T.astype(BF16), kt)
        if first:
            o_acc[pl.ds(r, C), :] = o
        else:
            o = _rms(o + o_acc[pl.ds(r, C), :], ng_ref[...])
            out_ref[pl.ds(r, C), :] = (o * jax.nn.silu(gate_ref[pl.ds(r, C), :])).astype(BF16)
        return st

    def step(first):
        def body(i, carry):
            st_f, st_b = carry
            return (visit(i, ff_ref, st_f, False, first),
                    visit(n - 1 - i, fb_ref, st_b, True, first))
        return body

    z = jnp.zeros((HG_DIM, HG_DIM), F32)
    carry = lax.fori_loop(0, n // 2, step(True), (z, z), unroll=HG_UNROLL)
    lax.fori_loop(n // 2, n, step(False), carry, unroll=HG_UNROLL)


def _hgrn(q, ff, fb, v, gate, ng, B, S):
    N = B * S
    C = HG_CHUNK
    assert (S // C) % 2 == 0
    blk = pl.BlockSpec((S, HG_DIM), lambda b, h: (b, h))
    return pl.pallas_call(
        _hgrn_kernel,
        grid=(B, HG_HEADS),
        in_specs=[blk, blk, blk, blk, blk, pl.BlockSpec((1, HG_DIM), lambda b, h: (0, 0))],
        out_specs=blk,
        out_shape=jax.ShapeDtypeStruct((N, HG_W), BF16),
        scratch_shapes=[pltpu.VMEM((S, HG_DIM), F32),
                        pltpu.VMEM((2, C, C), jnp.int32)],
        compiler_params=pltpu.CompilerParams(dimension_semantics=("arbitrary", "arbitrary"),
                                             vmem_limit_bytes=VMEM_LIMIT),
        name="hgrn",
    )(q, ff, fb, v, gate, ng)


def _attn_kernel(q_ref, k_ref, v_ref, ng_ref, out_ref, s_s, p_s, l_s):
    S = k_ref.shape[2]
    tq = ATT_TQ
    nq = S // tq

    rb_rows = tq // ATT_ROW_BLOCKS

    def scores(i, slot, rb):
        r0 = rb * rb_rows
        s_s[slot, r0:r0 + rb_rows, :] = _dot_nt(
            q_ref[0, 0, i * tq + r0:i * tq + r0 + rb_rows, :], k_ref[0, 0])

    def softmax(slot, rb):
        for r in range(rb * rb_rows, (rb + 1) * rb_rows, ATT_SUB):
            s = s_s[slot, r:r + ATT_SUB, :]
            p = jnp.exp2(s - jnp.max(s, axis=-1, keepdims=True))
            l_s[slot, r:r + ATT_SUB, :] = jnp.sum(p, axis=-1, keepdims=True)
            p_s[slot, r:r + ATT_SUB, :] = p.astype(BF16)

    def values(i, slot, rb):
        r0 = rb * rb_rows
        o = _dot(p_s[slot, r0:r0 + rb_rows, :], v_ref[0, 0]) / l_s[slot, r0:r0 + rb_rows, :]
        out_ref[i * tq + r0:i * tq + r0 + rb_rows, :] = _rms(o, ng_ref[...]).astype(BF16)

    for rb in range(ATT_ROW_BLOCKS):
        scores(0, 0, rb)
    for i in range(nq):
        for rb in range(ATT_ROW_BLOCKS):
            if i + 1 < nq:
                scores(i + 1, (i + 1) % 2, rb)
            softmax(i % 2, rb)
            values(i, i % 2, rb)


def _attn(qa, ka, va, ng, B, S):
    tq = ATT_TQ
    return pl.pallas_call(
        _attn_kernel,
        grid=(B, MLA_HEADS),
        in_specs=[pl.BlockSpec((1, 1, S, MLA_QK_PAD), lambda b, h: (b, h, 0, 0)),
                  pl.BlockSpec((1, 1, S, MLA_QK_PAD), lambda b, h: (b, h, 0, 0)),
                  pl.BlockSpec((1, 1, S, MLA_VHEAD), lambda b, h: (b, h, 0, 0)),
                  pl.BlockSpec((1, MLA_VHEAD), lambda b, h: (0, 0))],
        out_specs=pl.BlockSpec((S, MLA_VHEAD), lambda b, h: (b, h)),
        out_shape=jax.ShapeDtypeStruct((B * S, MLA_HEADS * MLA_VHEAD), BF16),
        scratch_shapes=[pltpu.VMEM((2, tq, S), F32),
                        pltpu.VMEM((2, tq, S), BF16),
                        pltpu.VMEM((2, tq, 1), F32)],
        compiler_params=pltpu.CompilerParams(
            dimension_semantics=("arbitrary", "arbitrary"),
            vmem_limit_bytes=VMEM_LIMIT),
        name="attn",
    )(qa, ka, va, ng)


def _mix_kernel(x_ref, hg_ref, at_ref, lng_ref, lnb_ref, wo_ref, g1_ref, b1_ref, out_ref):
    h = _ln(x_ref[...], lng_ref[...], lnb_ref[...])
    mix = _dot(hg_ref[...], wo_ref[0:HG_W, :]) + _dot(at_ref[...], wo_ref[HG_W:, :])
    out_ref[...] = _ln(DN_ALPHA * h + mix, g1_ref[...], b1_ref[...])


def _mix(x2, hg, at, ln_g, ln_b, wo, g1, b1):
    N = x2.shape[0]
    tm = PROJ_TM
    row = lambda i: (i, 0)
    const = lambda i: (0, 0)
    full = lambda a: pl.BlockSpec(a.shape, const)
    return pl.pallas_call(
        _mix_kernel,
        grid=(N // tm,),
        in_specs=[pl.BlockSpec((tm, D_MODEL), row), pl.BlockSpec((tm, HG_W), row),
                  pl.BlockSpec((tm, HG_W), row), full(ln_g), full(ln_b), full(wo),
                  full(g1), full(b1)],
        out_specs=pl.BlockSpec((tm, D_MODEL), row),
        out_shape=jax.ShapeDtypeStruct((N, D_MODEL), F32),
        compiler_params=pltpu.CompilerParams(dimension_semantics=("arbitrary",),
                                             vmem_limit_bytes=VMEM_LIMIT),
        name="mix",
    )(x2, hg, at, ln_g, ln_b, wo, g1, b1)


def _ffn_kernel(h_ref, hp_ref, hn_ref, wu_ref, cw_ref, cb_ref, wd_ref, g2_ref, b2_ref, out_ref,
                il_s, ext_s, u_s, act_s, acc_s, *, tiles_per_seq):
    tm = FFN_TM
    grp = tm // SUBLANES
    n = D_FF // FFN_FC
    i = pl.program_id(0)
    has_prev = jnp.where(i % tiles_per_seq != 0, 1.0, 0.0)
    has_next = jnp.where(i % tiles_per_seq != tiles_per_seq - 1, 1.0, 0.0)

    ngrp = D_MODEL // LANES
    for c in range(ngrp):
        for a in range(SUBLANES):
            il_s[c, pl.ds(a, grp, stride=SUBLANES), :] = h_ref[a * grp:(a + 1) * grp,
                                                               c * LANES:(c + 1) * LANES]
        ext_s[0:tm, c * LANES:(c + 1) * LANES] = il_s[c].astype(BF16)
    sub = lax.broadcasted_iota(jnp.int32, (2 * SUBLANES, D_MODEL), 0)
    halo = jnp.where(sub == 0, hp_ref[SUBLANES - 1:SUBLANES, :] * has_prev,
                     jnp.where(sub == 1, hn_ref[0:1, :] * has_next, 0.0))
    ext_s[tm:, :] = halo.astype(BF16)
    acc_s[...] = jnp.zeros_like(acc_s)

    half = tm // 2

    def chunk_cols(j, part):
        start = part * D_FF + j * FFN_FC
        return pl.ds(start if isinstance(start, int) else pl.multiple_of(start, FFN_FC), FFN_FC)

    def up(j, slot, part):
        u_s[slot, :, part * FFN_FC:(part + 1) * FFN_FC] = _dot(ext_s[...],
                                                               wu_ref[:, chunk_cols(j, part)])

    def glu(j, slot, part):
        r0, r1 = part * half, (part + 1) * half
        cw = jnp.concatenate([cw_ref[:, chunk_cols(j, 0)], cw_ref[:, chunk_cols(j, 1)]], axis=1)
        cb = jnp.concatenate([cb_ref[:, chunk_cols(j, 0)], cb_ref[:, chunk_cols(j, 1)]], axis=1)
        sub8 = lax.broadcasted_iota(jnp.int32, (SUBLANES, 2 * FFN_FC), 0)
        if part == 0:
            first = jnp.where(sub8 == 0, u_s[slot, tm:tm + 1, :],
                              pltpu.roll(u_s[slot, tm - SUBLANES:tm, :], 1, 0))
            prev = jnp.concatenate([first, u_s[slot, 0:r1 - SUBLANES, :]], axis=0)
            nxt = u_s[slot, SUBLANES:r1 + SUBLANES, :]
        else:
            last = jnp.where(sub8 == SUBLANES - 1, u_s[slot, tm + 1:tm + 2, :],
                             pltpu.roll(u_s[slot, 0:SUBLANES, :], SUBLANES - 1, 0))
            prev = u_s[slot, r0 - SUBLANES:r1 - SUBLANES, :]
            nxt = jnp.concatenate([u_s[slot, r0 + SUBLANES:tm, :], last], axis=0)
        c = prev * cw[0:1, :] + u_s[slot, r0:r1, :] * cw[1:2, :] + nxt * cw[2:3, :] + cb
        act_s[slot, r0:r1, :] = (jax.nn.gelu(c[:, :FFN_FC]) * c[:, FFN_FC:]).astype(BF16)

    def down(j, slot, part):
        c0 = part * 2 * LANES
        k0 = j * FFN_FC
        rows = pl.ds(k0 if isinstance(k0, int) else pl.multiple_of(k0, FFN_FC), FFN_FC)
        res = _dot(act_s[slot], wd_ref[rows, c0:c0 + 2 * LANES])
        acc_s[2 * part] += res[:, :LANES]
        acc_s[2 * part + 1] += res[:, LANES:]

    def stage(t, par):
        static = isinstance(t, int)
        do_up = not static or t + 1 < n
        do_glu = not static or 0 <= t < n
        do_down = not static or 1 <= t <= n
        for part in range(2):
            if do_up:
                up(t + 1, 1 - par, part)
            if do_glu:
                glu(t, par, part)
            if do_down:
                down(t - 1, 1 - par, 2 * part)
                down(t - 1, 1 - par, 2 * part + 1)

    stage(-1, 1)
    stage(0, 0)
    npairs = (n - 2) // 2

    def pair(jj, carry):
        t = 2 * jj + 1
        stage(t, 1)
        stage(t + 1, 0)
        return carry

    lax.fori_loop(0, npairs, pair, 0)
    for t in range(2 * npairs + 1, n + 1):
        stage(t, t % 2)

    for a in range(SUBLANES):
        ffn = jnp.concatenate([acc_s[c, pl.ds(a, grp, stride=SUBLANES), :] for c in range(ngrp)],
                              axis=1)
        rows = slice(a * grp, (a + 1) * grp)
        out_ref[rows, :] = _ln(DN_ALPHA * h_ref[rows, :] + ffn, g2_ref[...], b2_ref[...])


def _ffn(h1, wu, cw, cb, wd, g2, b2, S):
    N = h1.shape[0]
    tm = FFN_TM
    nb = tm // SUBLANES
    last_blk = N // SUBLANES - 1
    row = lambda i: (i, 0)
    full = lambda a: pl.BlockSpec(a.shape, lambda i: (0,) * a.ndim)
    return pl.pallas_call(
        functools.partial(_ffn_kernel, tiles_per_seq=S // tm),
        grid=(N // tm,),
        in_specs=[pl.BlockSpec((tm, D_MODEL), row),
                  pl.BlockSpec((SUBLANES, D_MODEL), lambda i: (jnp.maximum(i * nb - 1, 0), 0)),
                  pl.BlockSpec((SUBLANES, D_MODEL), lambda i: (jnp.minimum((i + 1) * nb, last_blk), 0)),
                  full(wu), full(cw), full(cb), full(wd), full(g2), full(b2)],
        out_specs=pl.BlockSpec((tm, D_MODEL), row),
        out_shape=jax.ShapeDtypeStruct((N, D_MODEL), F32),
        scratch_shapes=[pltpu.VMEM((D_MODEL // LANES, tm, LANES), F32),
                        pltpu.VMEM((tm + 2 * SUBLANES, D_MODEL), BF16),
                        pltpu.VMEM((2, tm + 2 * SUBLANES, 2 * FFN_FC), F32),
                        pltpu.VMEM((2, tm, FFN_FC), BF16),
                        pltpu.VMEM((D_MODEL // LANES, tm, LANES), F32)],
        compiler_params=pltpu.CompilerParams(dimension_semantics=("arbitrary",),
                                             vmem_limit_bytes=VMEM_LIMIT),
        name="ffn",
    )(h1, h1, h1, wu, cw, cb, wd, g2, b2)


def _swap_halves(w):
    half = MLA_ROPE // 2
    return jnp.concatenate([w[..., half:], w[..., :half]], axis=-1)


def kernel(x, positions, ln_in_g, ln_in_b, w_in, lb_fwd, lb_bwd, hg_norm_g, q_a_norm_g, w_q_b,
           kv_a_norm_g, w_kv_b, attn_norm_g, w_out, ln1_g, ln1_b, w_up, conv_w, conv_b, w_down,
           ln2_g, ln2_b):
    B, S, D = x.shape
    assert D == D_MODEL and w_in.shape[0] == 1 and lb_fwd.shape == (2, HG_W)
    assert S % HG_CHUNK == 0 and S % ATT_TQ == 0 and S % FFN_TM == 0 and D_FF % FFN_FC == 0
    N = B * S
    r1 = lambda a: a.reshape(1, -1)

    w_in0 = w_in[0]
    c_lat = 5 * HG_W
    c_kr = c_lat + MLA_Q_RANK + MLA_KV_RANK
    w_hg = w_in0[:, :c_lat].astype(BF16)
    w_kr = w_in0[:, c_kr:]
    w_lat = jnp.concatenate([w_in0[:, c_lat:c_kr], w_kr, _swap_halves(w_kr)], axis=1).astype(BF16)
    wq = w_q_b[0].reshape(MLA_Q_RANK, MLA_HEADS, MLA_NOPE + MLA_ROPE)
    wq_pe = wq[:, :, MLA_NOPE:]
    wqb = jnp.concatenate(
        [wq[:, :, :MLA_NOPE].reshape(MLA_Q_RANK, -1),
         jnp.concatenate([wq_pe, _swap_halves(wq_pe)], axis=-1).reshape(MLA_Q_RANK, -1)],
        axis=1).astype(BF16)
    wkvb = w_kv_b[0].astype(BF16)
    half = MLA_ROPE // 2
    inv_freq = 1.0 / (ROPE_THETA ** (jnp.arange(half, dtype=F32) / half))
    zeros = jnp.zeros((LANES - MLA_ROPE,), F32)
    invf = r1(jnp.concatenate([inv_freq, inv_freq, zeros]))
    sgn = r1(jnp.concatenate([-jnp.ones((half,), F32), jnp.ones((half,), F32), zeros]))
    wu = w_up[0].astype(BF16)
    cw = conv_w[0]
    cb = r1(conv_b[0])
    wd = w_down[0].astype(BF16)

    x2 = x.reshape(N, D)
    pos2 = positions.reshape(N, 1)
    q, ff, fb, gate, v, qa, ka, va = _proj(
        x2, pos2, r1(ln_in_g), r1(ln_in_b), w_hg, w_lat, lb_fwd, lb_bwd,
        r1(q_a_norm_g[0]), wqb, r1(kv_a_norm_g[0]), wkvb, invf, sgn, B, S)
    hg = _hgrn(q, ff, fb, v, gate, r1(hg_norm_g[0]), B, S)
    at = _attn(qa, ka, va, r1(attn_norm_g[0]), B, S)
    h1 = _mix(x2, hg, at, r1(ln_in_g), r1(ln_in_b), w_out[0].astype(BF16), r1(ln1_g[0]), r1(ln1_b[0]))
    out = _ffn(h1, wu, cw, cb, wd, r1(ln2_g[0]), r1(ln2_b[0]), S)
    return out.reshape(B, S, D)
```

```python
import functools

import jax
import jax.numpy as jnp
from jax import lax
from jax.experimental import pallas as pl
from jax.experimental.pallas import tpu as pltpu

D_MODEL = 1024
HG_HEADS = 4
HG_DIM = 128
HG_W = HG_HEADS * HG_DIM
MLA_HEADS = 4
MLA_Q_RANK = 384
MLA_KV_RANK = 128
MLA_NOPE = 128
MLA_ROPE = 64
MLA_VHEAD = 128
MLA_QK_PAD = 256
ROPE_THETA = 10000.0
D_FF = 2816
DN_ALPHA = 2.0 ** 0.25
NORM_EPS = 1e-5
LOG2E = 1.4426950408889634

LANES = 128
SUBLANES = 8
VMEM_LIMIT = 56 * 1024 * 1024

PROJ_TM = 512
HG_CHUNK = 128
HG_BLOCKS = HG_CHUNK // SUBLANES
HG_UNROLL = 8
ATT_TQ = 512
ATT_SUB = 32
ATT_ROW_BLOCKS = 1
FFN_TM = 512
FFN_FC = 256

F32 = jnp.float32
BF16 = jnp.bfloat16


def _ln(x, g, b):
    mu = jnp.mean(x, axis=-1, keepdims=True)
    xc = x - mu
    var = jnp.mean(xc * xc, axis=-1, keepdims=True)
    return xc * lax.rsqrt(var + NORM_EPS) * g + b


def _rms(x, g):
    return x * lax.rsqrt(jnp.mean(x * x, axis=-1, keepdims=True) + NORM_EPS) * g


def _dot(a, b):
    return jnp.dot(a, b, preferred_element_type=F32)


def _dot_nt(a, b):
    return lax.dot_general(a, b, (((1,), (1,)), ((), ())), preferred_element_type=F32)


def _proj_kernel(x_ref, pos_ref, lng_ref, lnb_ref, w_hg_ref, w_lat_ref, lbf_ref, lbb_ref,
                 qan_ref, wqb_ref, kvn_ref, wkvb_ref, invf_ref, sgn_ref,
                 q_ref, ff_ref, fb_ref, gate_ref, v_ref, qa_ref, ka_ref, va_ref):
    h = _ln(x_ref[...], lng_ref[...], lnb_ref[...]).astype(BF16)

    def lower_bound(ref):
        r0, r1 = ref[0:1, :], ref[1:2, :]
        m = jnp.maximum(r0, r1)
        e0, e1 = jnp.exp(r0 - m), jnp.exp(r1 - m)
        return e0 / (e0 + e1)

    def hg_cols(j):
        return _dot(h, w_hg_ref[:, j * HG_W:(j + 1) * HG_W])

    q_ref[...] = jax.nn.silu(hg_cols(0))
    v_ref[...] = hg_cols(1).astype(BF16)
    lbf = lower_bound(lbf_ref)
    ff_ref[...] = lbf + (1.0 - lbf) * jax.nn.sigmoid(hg_cols(2))
    lbb = lower_bound(lbb_ref)
    fb_ref[...] = lbb + (1.0 - lbb) * jax.nn.sigmoid(hg_cols(3))
    gate_ref[...] = hg_cols(4)

    ang = pos_ref[...].astype(F32) * invf_ref[...]
    sgn = sgn_ref[...]
    cos_t = jnp.cos(ang) * jnp.abs(sgn)
    sin_t = jnp.sin(ang) * sgn

    def rope(grp):
        return grp * cos_t + pltpu.roll(grp, MLA_ROPE, 1) * sin_t

    lat = _dot(h, w_lat_ref[...])
    qa = _rms(lat[:, :MLA_Q_RANK], qan_ref[...]).astype(BF16)
    kva = _rms(lat[:, MLA_Q_RANK:MLA_Q_RANK + MLA_KV_RANK], kvn_ref[...]).astype(BF16)
    k_pe = rope(lat[:, MLA_Q_RANK + MLA_KV_RANK:]).astype(BF16)

    scale = (MLA_NOPE + MLA_ROPE) ** -0.5 * LOG2E
    qh = _dot(qa, wqb_ref[...])
    kvh = _dot(kva, wkvb_ref[...])
    for hd in range(MLA_HEADS):
        qa_ref[0, hd, :, 0:LANES] = (qh[:, hd * LANES:(hd + 1) * LANES] * scale).astype(BF16)
        pe = rope(qh[:, (MLA_HEADS + hd) * LANES:(MLA_HEADS + hd + 1) * LANES])
        qa_ref[0, hd, :, LANES:2 * LANES] = (pe * scale).astype(BF16)
        ka_ref[0, hd, :, 0:LANES] = kvh[:, 2 * hd * LANES:(2 * hd + 1) * LANES].astype(BF16)
        ka_ref[0, hd, :, LANES:2 * LANES] = k_pe
        va_ref[0, hd, :, :] = kvh[:, (2 * hd + 1) * LANES:(2 * hd + 2) * LANES].astype(BF16)


def _proj(x2, pos2, ln_g, ln_b, w_hg, w_lat, lb_f, lb_b, qan, wqb, kvn, wkvb, invf, sgn, B, S):
    N = B * S
    tm = PROJ_TM
    spb = S // tm
    row = lambda i: (i, 0)
    const = lambda i: (0, 0)
    head = lambda i: (i // spb, 0, i % spb, 0)
    full = lambda a: pl.BlockSpec(a.shape, const)
    out_shape = (
        jax.ShapeDtypeStruct((N, HG_W), F32),
        jax.ShapeDtypeStruct((N, HG_W), F32),
        jax.ShapeDtypeStruct((N, HG_W), F32),
        jax.ShapeDtypeStruct((N, HG_W), F32),
        jax.ShapeDtypeStruct((N, HG_W), BF16),
        jax.ShapeDtypeStruct((B, MLA_HEADS, S, MLA_QK_PAD), BF16),
        jax.ShapeDtypeStruct((B, MLA_HEADS, S, MLA_QK_PAD), BF16),
        jax.ShapeDtypeStruct((B, MLA_HEADS, S, MLA_VHEAD), BF16),
    )
    tok = pl.BlockSpec((tm, HG_W), row)
    return pl.pallas_call(
        _proj_kernel,
        grid=(N // tm,),
        in_specs=[pl.BlockSpec((tm, D_MODEL), row), pl.BlockSpec((tm, 1), row),
                  full(ln_g), full(ln_b), full(w_hg), full(w_lat), full(lb_f), full(lb_b),
                  full(qan), full(wqb), full(kvn), full(wkvb), full(invf), full(sgn)],
        out_specs=(tok, tok, tok, tok, tok,
                   pl.BlockSpec((1, MLA_HEADS, tm, MLA_QK_PAD), head),
                   pl.BlockSpec((1, MLA_HEADS, tm, MLA_QK_PAD), head),
                   pl.BlockSpec((1, MLA_HEADS, tm, MLA_VHEAD), head)),
        out_shape=out_shape,
        compiler_params=pltpu.CompilerParams(dimension_semantics=("arbitrary",),
                                             vmem_limit_bytes=VMEM_LIMIT),
        name="proj",
    )(x2, pos2, ln_g, ln_b, w_hg, w_lat, lb_f, lb_b, qan, wqb, kvn, wkvb, invf, sgn)


def _blocks(x):
    return [x[i * SUBLANES:(i + 1) * SUBLANES] for i in range(x.shape[0] // SUBLANES)]


def _rows(blocks):
    return jnp.concatenate(blocks, axis=0)


def _hgrn_chunk(q, f, v, lmap_ref, rev):
    C = HG_CHUNK
    g = jnp.log(f) * LOG2E
    k = 1.0 - f
    row = lax.broadcasted_iota(jnp.int32, (C, HG_DIM), 0)

    def roll8(x, shift):
        return _rows([pltpu.roll(blk, shift, 0) for blk in _blocks(x)])

    P, T = g, g
    A = None
    for lvl in range(3):
        b = 1 << lvl
        bit = ((row >> lvl) & 1) == 1
        later = jnp.logical_not(bit) if rev else bit
        w = jnp.exp2(jnp.where(later, P, T - P))
        a_l = _dot_nt((q * w).astype(BF16), (k * w).astype(BF16))
        hit = lmap_ref[...] == lvl
        A = jnp.where(hit, a_l, 0.0) if A is None else jnp.where(hit, a_l, A)
        sib = roll8(T, b) if 2 * b == SUBLANES else jnp.where(bit, roll8(T, b), roll8(T, SUBLANES - b))
        P = P + jnp.where(later, sib, 0.0)
        T = T + sib

    Pb, Tb, qb, kb, Ab = _blocks(P), _blocks(T), _blocks(q), _blocks(k), _blocks(A)
    zero = jnp.zeros((SUBLANES, HG_DIM), F32)
    lvl, m = 3, 1
    while m < HG_BLOCKS:
        later = [(((i // m) & 1) == 1) != rev for i in range(HG_BLOCKS)]
        wb = [jnp.exp2(Pb[i] if later[i] else Tb[i] - Pb[i]) for i in range(HG_BLOCKS)]
        qt = _rows([qb[i] * wb[i] for i in range(HG_BLOCKS) if later[i]]).astype(BF16)
        kt = _rows([zero if later[i] else kb[i] * wb[i] for i in range(HG_BLOCKS)]).astype(BF16)
        a_l = _blocks(_dot_nt(qt, kt))
        n_later = 0
        for i in range(HG_BLOCKS):
            if later[i]:
                hit = lmap_ref[i * SUBLANES:(i + 1) * SUBLANES, :] == lvl
                Ab[i] = jnp.where(hit, a_l[n_later], Ab[i])
                n_later += 1
        sib = [Tb[i ^ m] for i in range(HG_BLOCKS)]
        Pb = [Pb[i] + sib[i] if later[i] else Pb[i] for i in range(HG_BLOCKS)]
        tot = {}
        for i in range(HG_BLOCKS):
            key = i // (2 * m)
            if key not in tot:
                tot[key] = Tb[i] + sib[i]
        Tb = [tot[i // (2 * m)] for i in range(HG_BLOCKS)]
        lvl, m = lvl + 1, 2 * m

    vf = v.astype(F32)
    o = _dot(_rows(Ab).astype(BF16), v) + jnp.sum(q * k, axis=1, keepdims=True) * vf
    qt = _rows([qb[i] * jnp.exp2(Pb[i]) for i in range(HG_BLOCKS)]).astype(BF16)
    kt = _rows([kb[i] * jnp.exp2(Tb[i] - Pb[i]) for i in range(HG_BLOCKS)]).astype(BF16)
    return o, qt, kt, jnp.exp2(Tb[0])


def _hgrn_kernel(q_ref, ff_ref, fb_ref, v_ref, gate_ref, ng_ref, out_ref, o_acc, lmap_s):
    S = q_ref.shape[0]
    C = HG_CHUNK
    n = S // C

    ri = lax.broadcasted_iota(jnp.int32, (C, C), 0)
    ci = lax.broadcasted_iota(jnp.int32, (C, C), 1)
    x = ri ^ ci
    level = jnp.full((C, C), -1, jnp.int32)
    b = 1
    while b < C:
        level = level + jnp.where(x >= b, 1, 0)
        b *= 2
    lmap_s[0] = jnp.where(ci < ri, level, -1)
    lmap_s[1] = jnp.where(ci > ri, level, -1)

    def visit(c, f_ref, st, rev, first):
        r = pl.multiple_of(c * C, C)
        q = q_ref[pl.ds(r, C), :]
        v = v_ref[pl.ds(r, C), :]
        o, qt, kt, d = _hgrn_chunk(q, f_ref[pl.ds(r, C), :], v, lmap_s.at[1 if rev else 0], rev)
        o = o + _dot_nt(qt, st.astype(BF16))
        st = st * d[0:1, :] + _dot(v.astype(F32).T.astype(BF16), kt)
        if first:
            o_acc[pl.ds(r, C), :] = o
        else:
            o = _rms(o + o_acc[pl.ds(r, C), :], ng_ref[...])
            out_ref[pl.ds(r, C), :] = (o * jax.nn.silu(gate_ref[pl.ds(r, C), :])).astype(BF16)
        return st

    def step(first):
        def body(i, carry):
            st_f, st_b = carry
            return (visit(i, ff_ref, st_f, False, first),
                    visit(n - 1 - i, fb_ref, st_b, True, first))
        return body

    z = jnp.zeros((HG_DIM, HG_DIM), F32)
    carry = lax.fori_loop(0, n // 2, step(True), (z, z), unroll=HG_UNROLL)
    lax.fori_loop(n // 2, n, step(False), carry, unroll=HG_UNROLL)


def _hgrn(q, ff, fb, v, gate, ng, B, S):
    N = B * S
    C = HG_CHUNK
    assert (S // C) % 2 == 0
    blk = pl.BlockSpec((S, HG_DIM), lambda b, h: (b, h))
    return pl.pallas_call(
        _hgrn_kernel,
        grid=(B, HG_HEADS),
        in_specs=[blk, blk, blk, blk, blk, pl.BlockSpec((1, HG_DIM), lambda b, h: (0, 0))],
        out_specs=blk,
        out_shape=jax.ShapeDtypeStruct((N, HG_W), BF16),
        scratch_shapes=[pltpu.VMEM((S, HG_DIM), F32),
                        pltpu.VMEM((2, C, C), jnp.int32)],
        compiler_params=pltpu.CompilerParams(dimension_semantics=("arbitrary", "arbitrary"),
                                             vmem_limit_bytes=VMEM_LIMIT),
        name="hgrn",
    )(q, ff, fb, v, gate, ng)


def _attn_kernel(q_ref, k_ref, v_ref, ng_ref, out_ref, s_s, p_s, l_s):
    S = k_ref.shape[2]
    tq = ATT_TQ
    nq = S // tq

    rb_rows = tq // ATT_ROW_BLOCKS

    def scores(i, slot, rb):
        r0 = rb * rb_rows
        s_s[slot, r0:r0 + rb_rows, :] = _dot_nt(
            q_ref[0, 0, i * tq + r0:i * tq + r0 + rb_rows, :], k_ref[0, 0])

    def softmax(slot, rb):
        for r in range(rb * rb_rows, (rb + 1) * rb_rows, ATT_SUB):
            s = s_s[slot, r:r + ATT_SUB, :]
            p = jnp.exp2(s - jnp.max(s, axis=-1, keepdims=True))
            l_s[slot, r:r + ATT_SUB, :] = jnp.sum(p, axis=-1, keepdims=True)
            p_s[slot, r:r + ATT_SUB, :] = p.astype(BF16)

    def values(i, slot, rb):
        r0 = rb * rb_rows
        o = _dot(p_s[slot, r0:r0 + rb_rows, :], v_ref[0, 0]) / l_s[slot, r0:r0 + rb_rows, :]
        out_ref[i * tq + r0:i * tq + r0 + rb_rows, :] = _rms(o, ng_ref[...]).astype(BF16)

    for rb in range(ATT_ROW_BLOCKS):
        scores(0, 0, rb)
    for i in range(nq):
        for rb in range(ATT_ROW_BLOCKS):
            if i + 1 < nq:
                scores(i + 1, (i + 1) % 2, rb)
            softmax(i % 2, rb)
            values(i, i % 2, rb)


def _attn(qa, ka, va, ng, B, S):
    tq = ATT_TQ
    return pl.pallas_call(
        _attn_kernel,
        grid=(B, MLA_HEADS),
        in_specs=[pl.BlockSpec((1, 1, S, MLA_QK_PAD), lambda b, h: (b, h, 0, 0)),
                  pl.BlockSpec((1, 1, S, MLA_QK_PAD), lambda b, h: (b, h, 0, 0)),
                  pl.BlockSpec((1, 1, S, MLA_VHEAD), lambda b, h: (b, h, 0, 0)),
                  pl.BlockSpec((1, MLA_VHEAD), lambda b, h: (0, 0))],
        out_specs=pl.BlockSpec((S, MLA_VHEAD), lambda b, h: (b, h)),
        out_shape=jax.ShapeDtypeStruct((B * S, MLA_HEADS * MLA_VHEAD), BF16),
        scratch_shapes=[pltpu.VMEM((2, tq, S), F32),
                        pltpu.VMEM((2, tq, S), BF16),
                        pltpu.VMEM((2, tq, 1), F32)],
        compiler_params=pltpu.CompilerParams(
            dimension_semantics=("arbitrary", "arbitrary"),
            vmem_limit_bytes=VMEM_LIMIT),
        name="attn",
    )(qa, ka, va, ng)


def _ffn_kernel(x_ref, xp_ref, xn_ref, hg_ref, hgp_ref, hgn_ref, at_ref, atp_ref, atn_ref,
                lng_ref, lnb_ref, wo_ref, g1_ref, b1_ref,
                wu_ref, cw_ref, cb_ref, wd_ref, g2_ref, b2_ref, out_ref,
                h1_s, il_s, ext_s, u_s, act_s, acc_s, *, tiles_per_seq):
    tm = FFN_TM
    hrows = 2 * SUBLANES
    grp = tm // SUBLANES
    n = D_FF // FFN_FC
    i = pl.program_id(0)
    has_prev = jnp.where(i % tiles_per_seq != 0, 1.0, 0.0)
    has_next = jnp.where(i % tiles_per_seq != tiles_per_seq - 1, 1.0, 0.0)

    def with_halo(main, before, after):
        sub = lax.broadcasted_iota(jnp.int32, (hrows, main.shape[1]), 0)
        halo = jnp.where(sub == 0, before.astype(F32), jnp.where(sub == 1, after.astype(F32), 0.0))
        return jnp.concatenate([main, halo.astype(main.dtype)], axis=0)

    xx = with_halo(x_ref[...], xp_ref[SUBLANES - 1:SUBLANES, :], xn_ref[0:1, :])
    hgx = with_halo(hg_ref[...], hgp_ref[...].astype(F32)[hrows - 1:hrows, :], hgn_ref[...].astype(F32)[0:1, :])
    atx = with_halo(at_ref[...], atp_ref[...].astype(F32)[hrows - 1:hrows, :], atn_ref[...].astype(F32)[0:1, :])
    mix = _dot(hgx, wo_ref[0:HG_W, :]) + _dot(atx, wo_ref[HG_W:, :])
    h1x = _ln(DN_ALPHA * _ln(xx, lng_ref[...], lnb_ref[...]) + mix, g1_ref[...], b1_ref[...])
    h1_s[...] = h1x[0:tm, :]
    sub = lax.broadcasted_iota(jnp.int32, (hrows, D_MODEL), 0)
    valid = jnp.where(sub == 0, has_prev, jnp.where(sub == 1, has_next, 0.0))
    ext_s[tm:, :] = (h1x[tm:, :] * valid).astype(BF16)

    ngrp = D_MODEL // LANES
    for c in range(ngrp):
        for a in range(SUBLANES):
            il_s[c, pl.ds(a, grp, stride=SUBLANES), :] = h1_s[a * grp:(a + 1) * grp,
                                                              c * LANES:(c + 1) * LANES]
        ext_s[0:tm, c * LANES:(c + 1) * LANES] = il_s[c].astype(BF16)
    acc_s[...] = jnp.zeros_like(acc_s)

    half = tm // 2

    def chunk_cols(j, part):
        start = part * D_FF + j * FFN_FC
        return pl.ds(start if isinstance(start, int) else pl.multiple_of(start, FFN_FC), FFN_FC)

    def up(j, slot, part):
        u_s[slot, :, part * FFN_FC:(part + 1) * FFN_FC] = _dot(ext_s[...],
                                                               wu_ref[:, chunk_cols(j, part)])

    def glu(j, slot, part):
        r0, r1 = part * half, (part + 1) * half
        cw = jnp.concatenate([cw_ref[:, chunk_cols(j, 0)], cw_ref[:, chunk_cols(j, 1)]], axis=1)
        cb = jnp.concatenate([cb_ref[:, chunk_cols(j, 0)], cb_ref[:, chunk_cols(j, 1)]], axis=1)
        sub8 = lax.broadcasted_iota(jnp.int32, (SUBLANES, 2 * FFN_FC), 0)
        if part == 0:
            first = jnp.where(sub8 == 0, u_s[slot, tm:tm + 1, :],
                              pltpu.roll(u_s[slot, tm - SUBLANES:tm, :], 1, 0))
            prev = jnp.concatenate([first, u_s[slot, 0:r1 - SUBLANES, :]], axis=0)
            nxt = u_s[slot, SUBLANES:r1 + SUBLANES, :]
        else:
            last = jnp.where(sub8 == SUBLANES - 1, u_s[slot, tm + 1:tm + 2, :],
                             pltpu.roll(u_s[slot, 0:SUBLANES, :], SUBLANES - 1, 0))
            prev = u_s[slot, r0 - SUBLANES:r1 - SUBLANES, :]
            nxt = jnp.concatenate([u_s[slot, r0 + SUBLANES:tm, :], last], axis=0)
        c = prev * cw[0:1, :] + u_s[slot, r0:r1, :] * cw[1:2, :] + nxt * cw[2:3, :] + cb
        act_s[slot, r0:r1, :] = (jax.nn.gelu(c[:, :FFN_FC]) * c[:, FFN_FC:]).astype(BF16)

    def down(j, slot, part):
        c0 = part * 2 * LANES
        k0 = j * FFN_FC
        rows = pl.ds(k0 if isinstance(k0, int) else pl.multiple_of(k0, FFN_FC), FFN_FC)
        res = _dot(act_s[slot], wd_ref[rows, c0:c0 + 2 * LANES])
        acc_s[2 * part] += res[:, :LANES]
        acc_s[2 * part + 1] += res[:, LANES:]

    def stage(t, par):
        static = isinstance(t, int)
        do_up = not static or t + 1 < n
        do_glu = not static or 0 <= t < n
        do_down = not static or 1 <= t <= n
        for part in range(2):
            if do_up:
                up(t + 1, 1 - par, part)
            if do_glu:
                glu(t, par, part)
            if do_down:
                down(t - 1, 1 - par, 2 * part)
                down(t - 1, 1 - par, 2 * part + 1)

    stage(-1, 1)
    stage(0, 0)
    npairs = (n - 2) // 2

    def pair(jj, carry):
        t = 2 * jj + 1
        stage(t, 1)
        stage(t + 1, 0)
        return carry

    lax.fori_loop(0, npairs, pair, 0)
    for t in range(2 * npairs + 1, n + 1):
        stage(t, t % 2)

    for a in range(SUBLANES):
        ffn = jnp.concatenate([acc_s[c, pl.ds(a, grp, stride=SUBLANES), :] for c in range(ngrp)],
                              axis=1)
        rows = slice(a * grp, (a + 1) * grp)
        out_ref[rows, :] = _ln(DN_ALPHA * h1_s[rows, :] + ffn, g2_ref[...], b2_ref[...])


def _ffn(x2, hg, at, ln_g, ln_b, wo, g1, b1, wu, cw, cb, wd, g2, b2, S):
    N = x2.shape[0]
    tm = FFN_TM
    hrows = 2 * SUBLANES
    row = lambda i: (i, 0)
    full = lambda a: pl.BlockSpec(a.shape, lambda i: (0,) * a.ndim)

    def halo_specs(rows, width):
        per_tile = tm // rows
        last = N // rows - 1
        return [pl.BlockSpec((rows, width), lambda i: (jnp.maximum(i * per_tile - 1, 0), 0)),
                pl.BlockSpec((rows, width), lambda i: (jnp.minimum((i + 1) * per_tile, last), 0))]

    return pl.pallas_call(
        functools.partial(_ffn_kernel, tiles_per_seq=S // tm),
        grid=(N // tm,),
        in_specs=([pl.BlockSpec((tm, D_MODEL), row)] + halo_specs(SUBLANES, D_MODEL)
                  + [pl.BlockSpec((tm, HG_W), row)] + halo_specs(hrows, HG_W)
                  + [pl.BlockSpec((tm, HG_W), row)] + halo_specs(hrows, HG_W)
                  + [full(a) for a in (ln_g, ln_b, wo, g1, b1, wu, cw, cb, wd, g2, b2)]),
        out_specs=pl.BlockSpec((tm, D_MODEL), row),
        out_shape=jax.ShapeDtypeStruct((N, D_MODEL), F32),
        scratch_shapes=[pltpu.VMEM((tm, D_MODEL), F32),
                        pltpu.VMEM((D_MODEL // LANES, tm, LANES), F32),
                        pltpu.VMEM((tm + hrows, D_MODEL), BF16),
                        pltpu.VMEM((2, tm + hrows, 2 * FFN_FC), F32),
                        pltpu.VMEM((2, tm, FFN_FC), BF16),
                        pltpu.VMEM((D_MODEL // LANES, tm, LANES), F32)],
        compiler_params=pltpu.CompilerParams(dimension_semantics=("arbitrary",),
                                             vmem_limit_bytes=VMEM_LIMIT),
        name="ffn",
    )(x2, x2, x2, hg, hg, hg, at, at, at, ln_g, ln_b, wo, g1, b1, wu, cw, cb, wd, g2, b2)


def _swap_halves(w):
    half = MLA_ROPE // 2
    return jnp.concatenate([w[..., half:], w[..., :half]], axis=-1)


def kernel(x, positions, ln_in_g, ln_in_b, w_in, lb_fwd, lb_bwd, hg_norm_g, q_a_norm_g, w_q_b,
           kv_a_norm_g, w_kv_b, attn_norm_g, w_out, ln1_g, ln1_b, w_up, conv_w, conv_b, w_down,
           ln2_g, ln2_b):
    B, S, D = x.shape
    assert D == D_MODEL and w_in.shape[0] == 1 and lb_fwd.shape == (2, HG_W)
    assert S % HG_CHUNK == 0 and S % ATT_TQ == 0 and S % FFN_TM == 0 and D_FF % FFN_FC == 0
    N = B * S
    r1 = lambda a: a.reshape(1, -1)

    w_in0 = w_in[0]
    c_lat = 5 * HG_W
    c_kr = c_lat + MLA_Q_RANK + MLA_KV_RANK
    w_hg = w_in0[:, :c_lat].astype(BF16)
    w_kr = w_in0[:, c_kr:]
    w_lat = jnp.concatenate([w_in0[:, c_lat:c_kr], w_kr, _swap_halves(w_kr)], axis=1).astype(BF16)
    wq = w_q_b[0].reshape(MLA_Q_RANK, MLA_HEADS, MLA_NOPE + MLA_ROPE)
    wq_pe = wq[:, :, MLA_NOPE:]
    wqb = jnp.concatenate(
        [wq[:, :, :MLA_NOPE].reshape(MLA_Q_RANK, -1),
         jnp.concatenate([wq_pe, _swap_halves(wq_pe)], axis=-1).reshape(MLA_Q_RANK, -1)],
        axis=1).astype(BF16)
    wkvb = w_kv_b[0].astype(BF16)
    half = MLA_ROPE // 2
    inv_freq = 1.0 / (ROPE_THETA ** (jnp.arange(half, dtype=F32) / half))
    zeros = jnp.zeros((LANES - MLA_ROPE,), F32)
    invf = r1(jnp.concatenate([inv_freq, inv_freq, zeros]))
    sgn = r1(jnp.concatenate([-jnp.ones((half,), F32), jnp.ones((half,), F32), zeros]))
    wu = w_up[0].astype(BF16)
    cw = conv_w[0]
    cb = r1(conv_b[0])
    wd = w_down[0].astype(BF16)

    x2 = x.reshape(N, D)
    pos2 = positions.reshape(N, 1)
    q, ff, fb, gate, v, qa, ka, va = _proj(
        x2, pos2, r1(ln_in_g), r1(ln_in_b), w_hg, w_lat, lb_fwd, lb_bwd,
        r1(q_a_norm_g[0]), wqb, r1(kv_a_norm_g[0]), wkvb, invf, sgn, B, S)
    hg = _hgrn(q, ff, fb, v, gate, r1(hg_norm_g[0]), B, S)
    at = _attn(qa, ka, va, r1(attn_norm_g[0]), B, S)
    out = _ffn(x2, hg, at, r1(ln_in_g), r1(ln_in_b), w_out[0].astype(BF16), r1(ln1_g[0]), r1(ln1_b[0]),
               wu, cw, cb, wd, r1(ln2_g[0]), r1(ln2_b[0]), S)
    return out.reshape(B, S, D)
```

```python
import functools

import jax
import jax.numpy as jnp
from jax import lax
from jax.experimental import pallas as pl
from jax.experimental.pallas import tpu as pltpu

D_MODEL = 1024
HG_HEADS = 4
HG_DIM = 128
HG_W = HG_HEADS * HG_DIM
MLA_HEADS = 4
MLA_Q_RANK = 384
MLA_KV_RANK = 128
MLA_NOPE = 128
MLA_ROPE = 64
MLA_VHEAD = 128
MLA_QK_PAD = 256
ROPE_THETA = 10000.0
D_FF = 2816
DN_ALPHA = 2.0 ** 0.25
NORM_EPS = 1e-5
LOG2E = 1.4426950408889634

LANES = 128
SUBLANES = 8
VMEM_LIMIT = 56 * 1024 * 1024

ROPE_TR = 512
PROJ_TM = 512
HG_CHUNK = 128
HG_BLOCKS = HG_CHUNK // SUBLANES
HG_UNROLL = 8
ATT_TQ = 512
ATT_SUB = 32
ATT_ROW_BLOCKS = 1
FFN_TM = 512
FFN_FC = 256

F32 = jnp.float32
BF16 = jnp.bfloat16


def _ln(x, g, b):
    mu = jnp.mean(x, axis=-1, keepdims=True)
    xc = x - mu
    var = jnp.mean(xc * xc, axis=-1, keepdims=True)
    return xc * lax.rsqrt(var + NORM_EPS) * g + b


def _rms(x, g):
    return x * lax.rsqrt(jnp.mean(x * x, axis=-1, keepdims=True) + NORM_EPS) * g


def _dot(a, b):
    return jnp.dot(a, b, preferred_element_type=F32)


def _dot_nt(a, b):
    return lax.dot_general(a, b, (((1,), (1,)), ((), ())), preferred_element_type=F32)


def _rope_kernel(pos_ref, invf_ref, cos_ref, sin_ref):
    ang = pos_ref[...].astype(F32) * invf_ref[...]
    cos_ref[...] = jnp.cos(ang)
    sin_ref[...] = jnp.sin(ang)


def _rope_tables(pos_l, invf_l):
    rows = pos_l.shape[0]
    tr = min(ROPE_TR, rows)
    blk = pl.BlockSpec((tr, LANES), lambda i: (i, 0))
    sds = jax.ShapeDtypeStruct((rows, LANES), F32)
    return pl.pallas_call(
        _rope_kernel,
        grid=(rows // tr,),
        in_specs=[blk, pl.BlockSpec((1, LANES), lambda i: (0, 0))],
        out_specs=(blk, blk),
        out_shape=(sds, sds),
        compiler_params=pltpu.CompilerParams(dimension_semantics=("arbitrary",)),
        name="rope",
    )(pos_l, invf_l)


def _proj_kernel(x_ref, tab_ref, lng_ref, lnb_ref, w_hg_ref, w_lat_ref, lbf_ref, lbb_ref,
                 qan_ref, wqb_ref, kvn_ref, wkvb_ref, sgn_ref,
                 q_ref, ff_ref, fb_ref, gate_ref, v_ref, qa_ref, ka_ref, va_ref):
    h = _ln(x_ref[...], lng_ref[...], lnb_ref[...]).astype(BF16)

    def lower_bound(ref):
        r0, r1 = ref[0:1, :], ref[1:2, :]
        m = jnp.maximum(r0, r1)
        e0, e1 = jnp.exp(r0 - m), jnp.exp(r1 - m)
        return e0 / (e0 + e1)

    def hg_cols(j):
        return _dot(h, w_hg_ref[:, j * HG_W:(j + 1) * HG_W])

    q_ref[...] = jax.nn.silu(hg_cols(0))
    v_ref[...] = hg_cols(1).astype(BF16)
    lbf = lower_bound(lbf_ref)
    ff_ref[...] = lbf + (1.0 - lbf) * jax.nn.sigmoid(hg_cols(2))
    lbb = lower_bound(lbb_ref)
    fb_ref[...] = lbb + (1.0 - lbb) * jax.nn.sigmoid(hg_cols(3))
    gate_ref[...] = hg_cols(4)

    tab = tab_ref[...]
    sgn = sgn_ref[...]
    cos_t = tab * jnp.abs(sgn)
    sin_t = pltpu.roll(tab, MLA_ROPE, 1) * sgn

    def rope(grp):
        return grp * cos_t + pltpu.roll(grp, MLA_ROPE, 1) * sin_t

    lat = _dot(h, w_lat_ref[...])
    qa = _rms(lat[:, :MLA_Q_RANK], qan_ref[...]).astype(BF16)
    kva = _rms(lat[:, MLA_Q_RANK:MLA_Q_RANK + MLA_KV_RANK], kvn_ref[...]).astype(BF16)
    k_pe = rope(lat[:, MLA_Q_RANK + MLA_KV_RANK:]).astype(BF16)

    scale = (MLA_NOPE + MLA_ROPE) ** -0.5 * LOG2E
    qh = _dot(qa, wqb_ref[...])
    kvh = _dot(kva, wkvb_ref[...])
    for hd in range(MLA_HEADS):
        qa_ref[0, hd, :, 0:LANES] = (qh[:, hd * LANES:(hd + 1) * LANES] * scale).astype(BF16)
        pe = rope(qh[:, (MLA_HEADS + hd) * LANES:(MLA_HEADS + hd + 1) * LANES])
        qa_ref[0, hd, :, LANES:2 * LANES] = (pe * scale).astype(BF16)
        ka_ref[0, hd, :, 0:LANES] = kvh[:, 2 * hd * LANES:(2 * hd + 1) * LANES].astype(BF16)
        ka_ref[0, hd, :, LANES:2 * LANES] = k_pe
        va_ref[0, hd, :, :] = kvh[:, (2 * hd + 1) * LANES:(2 * hd + 2) * LANES].astype(BF16)


def _proj(x2, tab, ln_g, ln_b, w_hg, w_lat, lb_f, lb_b, qan, wqb, kvn, wkvb, sgn, B, S):
    N = B * S
    tm = PROJ_TM
    spb = S // tm
    row = lambda i: (i, 0)
    const = lambda i: (0, 0)
    head = lambda i: (i // spb, 0, i % spb, 0)
    full = lambda a: pl.BlockSpec(a.shape, const)
    out_shape = (
        jax.ShapeDtypeStruct((N, HG_W), F32),
        jax.ShapeDtypeStruct((N, HG_W), F32),
        jax.ShapeDtypeStruct((N, HG_W), F32),
        jax.ShapeDtypeStruct((N, HG_W), F32),
        jax.ShapeDtypeStruct((N, HG_W), BF16),
        jax.ShapeDtypeStruct((B, MLA_HEADS, S, MLA_QK_PAD), BF16),
        jax.ShapeDtypeStruct((B, MLA_HEADS, S, MLA_QK_PAD), BF16),
        jax.ShapeDtypeStruct((B, MLA_HEADS, S, MLA_VHEAD), BF16),
    )
    tok = pl.BlockSpec((tm, HG_W), row)
    return pl.pallas_call(
        _proj_kernel,
        grid=(N // tm,),
        in_specs=[pl.BlockSpec((tm, D_MODEL), row), pl.BlockSpec((tm, LANES), row),
                  full(ln_g), full(ln_b), full(w_hg), full(w_lat), full(lb_f), full(lb_b),
                  full(qan), full(wqb), full(kvn), full(wkvb), full(sgn)],
        out_specs=(tok, tok, tok, tok, tok,
                   pl.BlockSpec((1, MLA_HEADS, tm, MLA_QK_PAD), head),
                   pl.BlockSpec((1, MLA_HEADS, tm, MLA_QK_PAD), head),
                   pl.BlockSpec((1, MLA_HEADS, tm, MLA_VHEAD), head)),
        out_shape=out_shape,
        compiler_params=pltpu.CompilerParams(dimension_semantics=("arbitrary",),
                                             vmem_limit_bytes=VMEM_LIMIT),
        name="proj",
    )(x2, tab, ln_g, ln_b, w_hg, w_lat, lb_f, lb_b, qan, wqb, kvn, wkvb, sgn)


def _blocks(x):
    return [x[i * SUBLANES:(i + 1) * SUBLANES] for i in range(x.shape[0] // SUBLANES)]


def _rows(blocks):
    return jnp.concatenate(blocks, axis=0)


def _hgrn_chunk(q, f, v, lmap_ref, rev):
    C = HG_CHUNK
    g = jnp.log(f) * LOG2E
    k = 1.0 - f
    row = lax.broadcasted_iota(jnp.int32, (C, HG_DIM), 0)

    def roll8(x, shift):
        return _rows([pltpu.roll(blk, shift, 0) for blk in _blocks(x)])

    P, T = g, g
    A = None
    for lvl in range(3):
        b = 1 << lvl
        bit = ((row >> lvl) & 1) == 1
        later = jnp.logical_not(bit) if rev else bit
        w = jnp.exp2(jnp.where(later, P, T - P))
        a_l = _dot_nt((q * w).astype(BF16), (k * w).astype(BF16))
        hit = lmap_ref[...] == lvl
        A = jnp.where(hit, a_l, 0.0) if A is None else jnp.where(hit, a_l, A)
        sib = roll8(T, b) if 2 * b == SUBLANES else jnp.where(bit, roll8(T, b), roll8(T, SUBLANES - b))
        P = P + jnp.where(later, sib, 0.0)
        T = T + sib

    Pb, Tb, qb, kb, Ab = _blocks(P), _blocks(T), _blocks(q), _blocks(k), _blocks(A)
    zero = jnp.zeros((SUBLANES, HG_DIM), F32)
    lvl, m = 3, 1
    while m < HG_BLOCKS:
        later = [(((i // m) & 1) == 1) != rev for i in range(HG_BLOCKS)]
        wb = [jnp.exp2(Pb[i] if later[i] else Tb[i] - Pb[i]) for i in range(HG_BLOCKS)]
        qt = _rows([qb[i] * wb[i] for i in range(HG_BLOCKS) if later[i]]).astype(BF16)
        kt = _rows([zero if later[i] else kb[i] * wb[i] for i in range(HG_BLOCKS)]).astype(BF16)
        a_l = _blocks(_dot_nt(qt, kt))
        n_later = 0
        for i in range(HG_BLOCKS):
            if later[i]:
                hit = lmap_ref[i * SUBLANES:(i + 1) * SUBLANES, :] == lvl
                Ab[i] = jnp.where(hit, a_l[n_later], Ab[i])
                n_later += 1
        sib = [Tb[i ^ m] for i in range(HG_BLOCKS)]
        Pb = [Pb[i] + sib[i] if later[i] else Pb[i] for i in range(HG_BLOCKS)]
        tot = {}
        for i in range(HG_BLOCKS):
            key = i // (2 * m)
            if key not in tot:
                tot[key] = Tb[i] + sib[i]
        Tb = [tot[i // (2 * m)] for i in range(HG_BLOCKS)]
        lvl, m = lvl + 1, 2 * m

    vf = v.astype(F32)
    o = _dot(_rows(Ab).astype(BF16), v) + jnp.sum(q * k, axis=1, keepdims=True) * vf
    qt = _rows([qb[i] * jnp.exp2(Pb[i]) for i in range(HG_BLOCKS)]).astype(BF16)
    kt = _rows([kb[i] * jnp.exp2(Tb[i] - Pb[i]) for i in range(HG_BLOCKS)]).astype(BF16)
    return o, qt, kt, jnp.exp2(Tb[0])


def _hgrn_kernel(q_ref, ff_ref, fb_ref, v_ref, gate_ref, ng_ref, out_ref, o_acc, lmap_s):
    S = q_ref.shape[0]
    C = HG_CHUNK
    n = S // C

    ri = lax.broadcasted_iota(jnp.int32, (C, C), 0)
    ci = lax.broadcasted_iota(jnp.int32, (C, C), 1)
    x = ri ^ ci
    level = jnp.full((C, C), -1, jnp.int32)
    b = 1
    while b < C:
        level = level + jnp.where(x >= b, 1, 0)
        b *= 2
    lmap_s[0] = jnp.where(ci < ri, level, -1)
    lmap_s[1] = jnp.where(ci > ri, level, -1)

    def visit(c, f_ref, st, rev, first):
        r = pl.multiple_of(c * C, C)
        q = q_ref[pl.ds(r, C), :]
        v = v_ref[pl.ds(r, C), :]
        o, qt, kt, d = _hgrn_chunk(q, f_ref[pl.ds(r, C), :], v, lmap_s.at[1 if rev else 0], rev)
        o = o + _dot_nt(qt, st.astype(BF16))
        st = st * d[0:1, :] + _dot(v.astype(F32).T.astype(BF16), kt)
        if first:
            o_acc[pl.ds(r, C), :] = o
        else:
            o = _rms(o + o_acc[pl.ds(r, C), :], ng_ref[...])
            out_ref[pl.ds(r, C), :] = (o * jax.nn.silu(gate_ref[pl.ds(r, C), :])).astype(BF16)
        return st

    def step(first):
        def body(i, carry):
            st_f, st_b = carry
            return (visit(i, ff_ref, st_f, False, first),
                    visit(n - 1 - i, fb_ref, st_b, True, first))
        return body

    z = jnp.zeros((HG_DIM, HG_DIM), F32)
    carry = lax.fori_loop(0, n // 2, step(True), (z, z), unroll=HG_UNROLL)
    lax.fori_loop(n // 2, n, step(False), carry, unroll=HG_UNROLL)


def _hgrn(q, ff, fb, v, gate, ng, B, S):
    N = B * S
    C = HG_CHUNK
    assert (S // C) % 2 == 0
    blk = pl.BlockSpec((S, HG_DIM), lambda b, h: (b, h))
    return pl.pallas_call(
        _hgrn_kernel,
        grid=(B, HG_HEADS),
        in_specs=[blk, blk, blk, blk, blk, pl.BlockSpec((1, HG_DIM), lambda b, h: (0, 0))],
        out_specs=blk,
        out_shape=jax.ShapeDtypeStruct((N, HG_W), BF16),
        scratch_shapes=[pltpu.VMEM((S, HG_DIM), F32),
                        pltpu.VMEM((2, C, C), jnp.int32)],
        compiler_params=pltpu.CompilerParams(dimension_semantics=("arbitrary", "arbitrary"),
                                             vmem_limit_bytes=VMEM_LIMIT),
        name="hgrn",
    )(q, ff, fb, v, gate, ng)


def _attn_kernel(q_ref, k_ref, v_ref, ng_ref, out_ref, s_s, p_s, l_s):
    S = k_ref.shape[2]
    tq = ATT_TQ
    nq = S // tq

    rb_rows = tq // ATT_ROW_BLOCKS

    def scores(i, slot, rb):
        r0 = rb * rb_rows
        s_s[slot, r0:r0 + rb_rows, :] = _dot_nt(
            q_ref[0, 0, i * tq + r0:i * tq + r0 + rb_rows, :], k_ref[0, 0])

    def softmax(slot, rb):
        for r in range(rb * rb_rows, (rb + 1) * rb_rows, ATT_SUB):
            s = s_s[slot, r:r + ATT_SUB, :]
            p = jnp.exp2(s - jnp.max(s, axis=-1, keepdims=True))
            l_s[slot, r:r + ATT_SUB, :] = jnp.sum(p, axis=-1, keepdims=True)
            p_s[slot, r:r + ATT_SUB, :] = p.astype(BF16)

    def values(i, slot, rb):
        r0 = rb * rb_rows
        o = _dot(p_s[slot, r0:r0 + rb_rows, :], v_ref[0, 0]) / l_s[slot, r0:r0 + rb_rows, :]
        out_ref[i * tq + r0:i * tq + r0 + rb_rows, :] = _rms(o, ng_ref[...]).astype(BF16)

    for rb in range(ATT_ROW_BLOCKS):
        scores(0, 0, rb)
    for i in range(nq):
        for rb in range(ATT_ROW_BLOCKS):
            if i + 1 < nq:
                scores(i + 1, (i + 1) % 2, rb)
            softmax(i % 2, rb)
            values(i, i % 2, rb)


def _attn(qa, ka, va, ng, B, S):
    tq = ATT_TQ
    return pl.pallas_call(
        _attn_kernel,
        grid=(B, MLA_HEADS),
        in_specs=[pl.BlockSpec((1, 1, S, MLA_QK_PAD), lambda b, h: (b, h, 0, 0)),
                  pl.BlockSpec((1, 1, S, MLA_QK_PAD), lambda b, h: (b, h, 0, 0)),
                  pl.BlockSpec((1, 1, S, MLA_VHEAD), lambda b, h: (b, h, 0, 0)),
                  pl.BlockSpec((1, MLA_VHEAD), lambda b, h: (0, 0))],
        out_specs=pl.BlockSpec((S, MLA_VHEAD), lambda b, h: (b, h)),
        out_shape=jax.ShapeDtypeStruct((B * S, MLA_HEADS * MLA_VHEAD), BF16),
        scratch_shapes=[pltpu.VMEM((2, tq, S), F32),
                        pltpu.VMEM((2, tq, S), BF16),
                        pltpu.VMEM((2, tq, 1), F32)],
        compiler_params=pltpu.CompilerParams(
            dimension_semantics=("arbitrary", "arbitrary"),
            vmem_limit_bytes=VMEM_LIMIT),
        name="attn",
    )(qa, ka, va, ng)


def _ffn_kernel(x_ref, xp_ref, xn_ref, hg_ref, hgp_ref, hgn_ref, at_ref, atp_ref, atn_ref,
                lng_ref, lnb_ref, wo_ref, g1_ref, b1_ref,
                wu_ref, cw_ref, cb_ref, wd_ref, g2_ref, b2_ref, out_ref,
                h1_s, il_s, ext_s, u_s, act_s, acc_s, *, tiles_per_seq):
    tm = FFN_TM
    hrows = 2 * SUBLANES
    grp = tm // SUBLANES
    n = D_FF // FFN_FC
    i = pl.program_id(0)
    has_prev = jnp.where(i % tiles_per_seq != 0, 1.0, 0.0)
    has_next = jnp.where(i % tiles_per_seq != tiles_per_seq - 1, 1.0, 0.0)

    def with_halo(main, before, after):
        sub = lax.broadcasted_iota(jnp.int32, (hrows, main.shape[1]), 0)
        halo = jnp.where(sub == 0, before.astype(F32), jnp.where(sub == 1, after.astype(F32), 0.0))
        return jnp.concatenate([main, halo.astype(main.dtype)], axis=0)

    xx = with_halo(x_ref[...], xp_ref[SUBLANES - 1:SUBLANES, :], xn_ref[0:1, :])
    hgx = with_halo(hg_ref[...], hgp_ref[...].astype(F32)[hrows - 1:hrows, :], hgn_ref[...].astype(F32)[0:1, :])
    atx = with_halo(at_ref[...], atp_ref[...].astype(F32)[hrows - 1:hrows, :], atn_ref[...].astype(F32)[0:1, :])
    mix = _dot(hgx, wo_ref[0:HG_W, :]) + _dot(atx, wo_ref[HG_W:, :])
    h1x = _ln(DN_ALPHA * _ln(xx, lng_ref[...], lnb_ref[...]) + mix, g1_ref[...], b1_ref[...])
    h1_s[...] = h1x[0:tm, :]
    sub = lax.broadcasted_iota(jnp.int32, (hrows, D_MODEL), 0)
    valid = jnp.where(sub == 0, has_prev, jnp.where(sub == 1, has_next, 0.0))
    ext_s[tm:, :] = (h1x[tm:, :] * valid).astype(BF16)

    ngrp = D_MODEL // LANES
    for c in range(ngrp):
        for a in range(SUBLANES):
            il_s[c, pl.ds(a, grp, stride=SUBLANES), :] = h1_s[a * grp:(a + 1) * grp,
                                                              c * LANES:(c + 1) * LANES]
        ext_s[0:tm, c * LANES:(c + 1) * LANES] = il_s[c].astype(BF16)
    acc_s[...] = jnp.zeros_like(acc_s)

    half = tm // 2

    def chunk_cols(j, part):
        start = part * D_FF + j * FFN_FC
        return pl.ds(start if isinstance(start, int) else pl.multiple_of(start, FFN_FC), FFN_FC)

    def up(j, slot, part):
        u_s[slot, :, part * FFN_FC:(part + 1) * FFN_FC] = _dot(ext_s[...],
                                                               wu_ref[:, chunk_cols(j, part)])

    def glu(j, slot, part):
        r0, r1 = part * half, (part + 1) * half
        cw = jnp.concatenate([cw_ref[:, chunk_cols(j, 0)], cw_ref[:, chunk_cols(j, 1)]], axis=1)
        cb = jnp.concatenate([cb_ref[:, chunk_cols(j, 0)], cb_ref[:, chunk_cols(j, 1)]], axis=1)
        sub8 = lax.broadcasted_iota(jnp.int32, (SUBLANES, 2 * FFN_FC), 0)
        if part == 0:
            first = jnp.where(sub8 == 0, u_s[slot, tm:tm + 1, :],
                              pltpu.roll(u_s[slot, tm - SUBLANES:tm, :], 1, 0))
            prev = jnp.concatenate([first, u_s[slot, 0:r1 - SUBLANES, :]], axis=0)
            nxt = u_s[slot, SUBLANES:r1 + SUBLANES, :]
        else:
            last = jnp.where(sub8 == SUBLANES - 1, u_s[slot, tm + 1:tm + 2, :],
                             pltpu.roll(u_s[slot, 0:SUBLANES, :], SUBLANES - 1, 0))
            prev = u_s[slot, r0 - SUBLANES:r1 - SUBLANES, :]
            nxt = jnp.concatenate([u_s[slot, r0 + SUBLANES:tm, :], last], axis=0)
        c = prev * cw[0:1, :] + u_s[slot, r0:r1, :] * cw[1:2, :] + nxt * cw[2:3, :] + cb
        act_s[slot, r0:r1, :] = (jax.nn.gelu(c[:, :FFN_FC]) * c[:, FFN_FC:]).astype(BF16)

    def down(j, slot, part):
        c0 = part * 2 * LANES
        k0 = j * FFN_FC
        rows = pl.ds(k0 if isinstance(k0, int) else pl.multiple_of(k0, FFN_FC), FFN_FC)
        res = _dot(act_s[slot], wd_ref[rows, c0:c0 + 2 * LANES])
        acc_s[2 * part] += res[:, :LANES]
        acc_s[2 * part + 1] += res[:, LANES:]

    def stage(t, par):
        static = isinstance(t, int)
        do_up = not static or t + 1 < n
        do_glu = not static or 0 <= t < n
        do_down = not static or 1 <= t <= n
        for part in range(2):
            if do_up:
                up(t + 1, 1 - par, part)
            if do_glu:
                glu(t, par, part)
            if do_down:
                down(t - 1, 1 - par, 2 * part)
                down(t - 1, 1 - par, 2 * part + 1)

    stage(-1, 1)
    stage(0, 0)
    npairs = (n - 2) // 2

    def pair(jj, carry):
        t = 2 * jj + 1
        stage(t, 1)
        stage(t + 1, 0)
        return carry

    lax.fori_loop(0, npairs, pair, 0)
    for t in range(2 * npairs + 1, n + 1):
        stage(t, t % 2)

    for a in range(SUBLANES):
        ffn = jnp.concatenate([acc_s[c, pl.ds(a, grp, stride=SUBLANES), :] for c in range(ngrp)],
                              axis=1)
        rows = slice(a * grp, (a + 1) * grp)
        out_ref[rows, :] = _ln(DN_ALPHA * h1_s[rows, :] + ffn, g2_ref[...], b2_ref[...])


def _ffn(x2, hg, at, ln_g, ln_b, wo, g1, b1, wu, cw, cb, wd, g2, b2, S):
    N = x2.shape[0]
    tm = FFN_TM
    hrows = 2 * SUBLANES
    row = lambda i: (i, 0)
    full = lambda a: pl.BlockSpec(a.shape, lambda i: (0,) * a.ndim)

    def halo_specs(rows, width):
        per_tile = tm // rows
        last = N // rows - 1
        return [pl.BlockSpec((rows, width), lambda i: (jnp.maximum(i * per_tile - 1, 0), 0)),
                pl.BlockSpec((rows, width), lambda i: (jnp.minimum((i + 1) * per_tile, last), 0))]

    return pl.pallas_call(
        functools.partial(_ffn_kernel, tiles_per_seq=S // tm),
        grid=(N // tm,),
        in_specs=([pl.BlockSpec((tm, D_MODEL), row)] + halo_specs(SUBLANES, D_MODEL)
                  + [pl.BlockSpec((tm, HG_W), row)] + halo_specs(hrows, HG_W)
                  + [pl.BlockSpec((tm, HG_W), row)] + halo_specs(hrows, HG_W)
                  + [full(a) for a in (ln_g, ln_b, wo, g1, b1, wu, cw, cb, wd, g2, b2)]),
        out_specs=pl.BlockSpec((tm, D_MODEL), row),
        out_shape=jax.ShapeDtypeStruct((N, D_MODEL), F32),
        scratch_shapes=[pltpu.VMEM((tm, D_MODEL), F32),
                        pltpu.VMEM((D_MODEL // LANES, tm, LANES), F32),
                        pltpu.VMEM((tm + hrows, D_MODEL), BF16),
                        pltpu.VMEM((2, tm + hrows, 2 * FFN_FC), F32),
                        pltpu.VMEM((2, tm, FFN_FC), BF16),
                        pltpu.VMEM((D_MODEL // LANES, tm, LANES), F32)],
        compiler_params=pltpu.CompilerParams(dimension_semantics=("arbitrary",),
                                             vmem_limit_bytes=VMEM_LIMIT),
        name="ffn",
    )(x2, x2, x2, hg, hg, hg, at, at, at, ln_g, ln_b, wo, g1, b1, wu, cw, cb, wd, g2, b2)


def _swap_halves(w):
    half = MLA_ROPE // 2
    return jnp.concatenate([w[..., half:], w[..., :half]], axis=-1)


def kernel(x, positions, ln_in_g, ln_in_b, w_in, lb_fwd, lb_bwd, hg_norm_g, q_a_norm_g, w_q_b,
           kv_a_norm_g, w_kv_b, attn_norm_g, w_out, ln1_g, ln1_b, w_up, conv_w, conv_b, w_down,
           ln2_g, ln2_b):
    B, S, D = x.shape
    assert D == D_MODEL and w_in.shape[0] == 1 and lb_fwd.shape == (2, HG_W)
    assert S % HG_CHUNK == 0 and S % ATT_TQ == 0 and S % FFN_TM == 0 and D_FF % FFN_FC == 0
    N = B * S
    r1 = lambda a: a.reshape(1, -1)

    w_in0 = w_in[0]
    c_lat = 5 * HG_W
    c_kr = c_lat + MLA_Q_RANK + MLA_KV_RANK
    w_hg = w_in0[:, :c_lat].astype(BF16)
    w_kr = w_in0[:, c_kr:]
    w_lat = jnp.concatenate([w_in0[:, c_lat:c_kr], w_kr, _swap_halves(w_kr)], axis=1).astype(BF16)
    wq = w_q_b[0].reshape(MLA_Q_RANK, MLA_HEADS, MLA_NOPE + MLA_ROPE)
    wq_pe = wq[:, :, MLA_NOPE:]
    wqb = jnp.concatenate(
        [wq[:, :, :MLA_NOPE].reshape(MLA_Q_RANK, -1),
         jnp.concatenate([wq_pe, _swap_halves(wq_pe)], axis=-1).reshape(MLA_Q_RANK, -1)],
        axis=1).astype(BF16)
    wkvb = w_kv_b[0].astype(BF16)
    half = MLA_ROPE // 2
    inv_freq = 1.0 / (ROPE_THETA ** (jnp.arange(half, dtype=F32) / half))
    zeros = jnp.zeros((LANES - MLA_ROPE,), F32)
    sgn = r1(jnp.concatenate([-jnp.ones((half,), F32), jnp.ones((half,), F32), zeros]))
    tpr = LANES // half
    pos_l = jnp.repeat(positions.reshape(N // tpr, tpr), half, axis=1)
    cos_l, sin_l = _rope_tables(pos_l, r1(jnp.tile(inv_freq, tpr)))
    cos_n, sin_n = cos_l.reshape(N, half), sin_l.reshape(N, half)
    tab = jnp.concatenate([cos_n, cos_n, sin_n, sin_n], axis=1)
    wu = w_up[0].astype(BF16)
    cw = conv_w[0]
    cb = r1(conv_b[0])
    wd = w_down[0].astype(BF16)

    x2 = x.reshape(N, D)
    q, ff, fb, gate, v, qa, ka, va = _proj(
        x2, tab, r1(ln_in_g), r1(ln_in_b), w_hg, w_lat, lb_fwd, lb_bwd,
        r1(q_a_norm_g[0]), wqb, r1(kv_a_norm_g[0]), wkvb, sgn, B, S)
    hg = _hgrn(q, ff, fb, v, gate, r1(hg_norm_g[0]), B, S)
    at = _attn(qa, ka, va, r1(attn_norm_g[0]), B, S)
    out = _ffn(x2, hg, at, r1(ln_in_g), r1(ln_in_b), w_out[0].astype(BF16), r1(ln1_g[0]), r1(ln1_b[0]),
               wu, cw, cb, wd, r1(ln2_g[0]), r1(ln2_b[0]), S)
    return out.reshape(B, S, D)
```

```python
import functools

import jax
import jax.numpy as jnp
from jax import lax
from jax.experimental import pallas as pl
from jax.experimental.pallas import tpu as pltpu

D_MODEL = 1024
HG_HEADS = 4
HG_DIM = 128
HG_W = HG_HEADS * HG_DIM
MLA_HEADS = 4
MLA_Q_RANK = 384
MLA_KV_RANK = 128
MLA_NOPE = 128
MLA_ROPE = 64
MLA_VHEAD = 128
MLA_QK_PAD = 256
ROPE_THETA = 10000.0
D_FF = 2816
DN_ALPHA = 2.0 ** 0.25
NORM_EPS = 1e-5
LOG2E = 1.4426950408889634

LANES = 128
SUBLANES = 8
VMEM_LIMIT = 56 * 1024 * 1024

PROJ_TM = 512
HG_CHUNK = 256
HG_BLOCKS = HG_CHUNK // SUBLANES
HG_UNROLL = 8
ATT_TQ = 256
ATT_SUB = 32
ATT_ROW_BLOCKS = 1
FFN_TM = 512
FFN_FC = 256

F32 = jnp.float32
BF16 = jnp.bfloat16


def _ln(x, g, b):
    mu = jnp.mean(x, axis=-1, keepdims=True)
    xc = x - mu
    var = jnp.mean(xc * xc, axis=-1, keepdims=True)
    return xc * lax.rsqrt(var + NORM_EPS) * g + b


def _rms(x, g):
    return x * lax.rsqrt(jnp.mean(x * x, axis=-1, keepdims=True) + NORM_EPS) * g


def _dot(a, b):
    return jnp.dot(a, b, preferred_element_type=F32)


def _dot_nt(a, b):
    return lax.dot_general(a, b, (((1,), (1,)), ((), ())), preferred_element_type=F32)


def _proj_kernel(x_ref, pos_ref, lng_ref, lnb_ref, w_hg_ref, w_lat_ref, lbf_ref, lbb_ref,
                 qan_ref, wqb_ref, kvn_ref, wkvb_ref, invf_ref, sgn_ref,
                 q_ref, ff_ref, fb_ref, gate_ref, v_ref, qa_ref, ka_ref, va_ref):
    h = _ln(x_ref[...], lng_ref[...], lnb_ref[...]).astype(BF16)

    def lower_bound(ref):
        r0, r1 = ref[0:1, :], ref[1:2, :]
        m = jnp.maximum(r0, r1)
        e0, e1 = jnp.exp(r0 - m), jnp.exp(r1 - m)
        return e0 / (e0 + e1)

    def hg_cols(j):
        return _dot(h, w_hg_ref[:, j * HG_W:(j + 1) * HG_W])

    q_ref[...] = jax.nn.silu(hg_cols(0))
    v_ref[...] = hg_cols(1).astype(BF16)
    lbf = lower_bound(lbf_ref)
    ff_ref[...] = lbf + (1.0 - lbf) * jax.nn.sigmoid(hg_cols(2))
    lbb = lower_bound(lbb_ref)
    fb_ref[...] = lbb + (1.0 - lbb) * jax.nn.sigmoid(hg_cols(3))
    gate_ref[...] = hg_cols(4)

    hr = x_ref.shape[0] // 2
    pos = pos_ref[...].astype(F32)
    lane = lax.broadcasted_iota(jnp.int32, (hr, LANES), 1)
    ang = jnp.where(lane < MLA_ROPE, pos[0:hr, :], pos[hr:, :]) * invf_ref[...]
    cs, sn = jnp.cos(ang), jnp.sin(ang)
    sgn = sgn_ref[...]
    keep = jnp.abs(sgn)
    cos_t = jnp.concatenate([cs * keep, pltpu.roll(cs, MLA_ROPE, 1) * keep], axis=0)
    sin_t = jnp.concatenate([sn * sgn, pltpu.roll(sn, MLA_ROPE, 1) * sgn], axis=0)

    def rope(grp):
        return grp * cos_t + pltpu.roll(grp, MLA_ROPE, 1) * sin_t

    lat = _dot(h, w_lat_ref[...])
    qa = _rms(lat[:, :MLA_Q_RANK], qan_ref[...]).astype(BF16)
    kva = _rms(lat[:, MLA_Q_RANK:MLA_Q_RANK + MLA_KV_RANK], kvn_ref[...]).astype(BF16)
    k_pe = rope(lat[:, MLA_Q_RANK + MLA_KV_RANK:]).astype(BF16)

    scale = (MLA_NOPE + MLA_ROPE) ** -0.5 * LOG2E
    qh = _dot(qa, wqb_ref[...])
    kvh = _dot(kva, wkvb_ref[...])
    for hd in range(MLA_HEADS):
        qa_ref[0, hd, :, 0:LANES] = (qh[:, hd * LANES:(hd + 1) * LANES] * scale).astype(BF16)
        pe = rope(qh[:, (MLA_HEADS + hd) * LANES:(MLA_HEADS + hd + 1) * LANES])
        qa_ref[0, hd, :, LANES:2 * LANES] = (pe * scale).astype(BF16)
        ka_ref[0, hd, :, 0:LANES] = kvh[:, 2 * hd * LANES:(2 * hd + 1) * LANES].astype(BF16)
        ka_ref[0, hd, :, LANES:2 * LANES] = k_pe
        va_ref[0, hd, :, :] = kvh[:, (2 * hd + 1) * LANES:(2 * hd + 2) * LANES].astype(BF16)


def _proj(x2, pos2, ln_g, ln_b, w_hg, w_lat, lb_f, lb_b, qan, wqb, kvn, wkvb, invf, sgn, B, S):
    N = B * S
    tm = PROJ_TM
    spb = S // tm
    row = lambda i: (i, 0)
    const = lambda i: (0, 0)
    head = lambda i: (i // spb, 0, i % spb, 0)
    full = lambda a: pl.BlockSpec(a.shape, const)
    out_shape = (
        jax.ShapeDtypeStruct((N, HG_W), F32),
        jax.ShapeDtypeStruct((N, HG_W), F32),
        jax.ShapeDtypeStruct((N, HG_W), F32),
        jax.ShapeDtypeStruct((N, HG_W), F32),
        jax.ShapeDtypeStruct((N, HG_W), BF16),
        jax.ShapeDtypeStruct((B, MLA_HEADS, S, MLA_QK_PAD), BF16),
        jax.ShapeDtypeStruct((B, MLA_HEADS, S, MLA_QK_PAD), BF16),
        jax.ShapeDtypeStruct((B, MLA_HEADS, S, MLA_VHEAD), BF16),
    )
    tok = pl.BlockSpec((tm, HG_W), row)
    return pl.pallas_call(
        _proj_kernel,
        grid=(N // tm,),
        in_specs=[pl.BlockSpec((tm, D_MODEL), row), pl.BlockSpec((tm, 1), row),
                  full(ln_g), full(ln_b), full(w_hg), full(w_lat), full(lb_f), full(lb_b),
                  full(qan), full(wqb), full(kvn), full(wkvb), full(invf), full(sgn)],
        out_specs=(tok, tok, tok, tok, tok,
                   pl.BlockSpec((1, MLA_HEADS, tm, MLA_QK_PAD), head),
                   pl.BlockSpec((1, MLA_HEADS, tm, MLA_QK_PAD), head),
                   pl.BlockSpec((1, MLA_HEADS, tm, MLA_VHEAD), head)),
        out_shape=out_shape,
        compiler_params=pltpu.CompilerParams(dimension_semantics=("arbitrary",),
                                             vmem_limit_bytes=VMEM_LIMIT),
        name="proj",
    )(x2, pos2, ln_g, ln_b, w_hg, w_lat, lb_f, lb_b, qan, wqb, kvn, wkvb, invf, sgn)


def _blocks(x):
    return [x[i * SUBLANES:(i + 1) * SUBLANES] for i in range(x.shape[0] // SUBLANES)]


def _rows(blocks):
    return jnp.concatenate(blocks, axis=0)


def _hgrn_chunk(q, f, v, lmap_ref, rev):
    C = HG_CHUNK
    g = jnp.log(f) * LOG2E
    k = 1.0 - f
    row = lax.broadcasted_iota(jnp.int32, (C, HG_DIM), 0)

    def roll8(x, shift):
        return _rows([pltpu.roll(blk, shift, 0) for blk in _blocks(x)])

    P, T = g, g
    A = None
    for lvl in range(3):
        b = 1 << lvl
        bit = ((row >> lvl) & 1) == 1
        later = jnp.logical_not(bit) if rev else bit
        w = jnp.exp2(jnp.where(later, P, T - P))
        a_l = _dot_nt((q * w).astype(BF16), (k * w).astype(BF16))
        hit = lmap_ref[...] == lvl
        A = jnp.where(hit, a_l, 0.0) if A is None else jnp.where(hit, a_l, A)
        sib = roll8(T, b) if 2 * b == SUBLANES else jnp.where(bit, roll8(T, b), roll8(T, SUBLANES - b))
        P = P + jnp.where(later, sib, 0.0)
        T = T + sib

    Pb, Tb, qb, kb, Ab = _blocks(P), _blocks(T), _blocks(q), _blocks(k), _blocks(A)
    zero = jnp.zeros((SUBLANES, HG_DIM), F32)
    lvl, m = 3, 1
    while m < HG_BLOCKS:
        later = [(((i // m) & 1) == 1) != rev for i in range(HG_BLOCKS)]
        wb = [jnp.exp2(Pb[i] if later[i] else Tb[i] - Pb[i]) for i in range(HG_BLOCKS)]
        qt = _rows([qb[i] * wb[i] for i in range(HG_BLOCKS) if later[i]]).astype(BF16)
        kt = _rows([zero if later[i] else kb[i] * wb[i] for i in range(HG_BLOCKS)]).astype(BF16)
        a_l = _blocks(_dot_nt(qt, kt))
        n_later = 0
        for i in range(HG_BLOCKS):
            if later[i]:
                hit = lmap_ref[i * SUBLANES:(i + 1) * SUBLANES, :] == lvl
                Ab[i] = jnp.where(hit, a_l[n_later], Ab[i])
                n_later += 1
        sib = [Tb[i ^ m] for i in range(HG_BLOCKS)]
        Pb = [Pb[i] + sib[i] if later[i] else Pb[i] for i in range(HG_BLOCKS)]
        tot = {}
        for i in range(HG_BLOCKS):
            key = i // (2 * m)
            if key not in tot:
                tot[key] = Tb[i] + sib[i]
        Tb = [tot[i // (2 * m)] for i in range(HG_BLOCKS)]
        lvl, m = lvl + 1, 2 * m

    vf = v.astype(F32)
    o = _dot(_rows(Ab).astype(BF16), v) + jnp.sum(q * k, axis=1, keepdims=True) * vf
    qt = _rows([qb[i] * jnp.exp2(Pb[i]) for i in range(HG_BLOCKS)]).astype(BF16)
    kt = _rows([kb[i] * jnp.exp2(Tb[i] - Pb[i]) for i in range(HG_BLOCKS)]).astype(BF16)
    return o, qt, kt, jnp.exp2(Tb[0])


def _hgrn_kernel(q_ref, ff_ref, fb_ref, v_ref, gate_ref, ng_ref, out_ref, o_acc, lmap_s):
    S = q_ref.shape[0]
    C = HG_CHUNK
    n = S // C

    ri = lax.broadcasted_iota(jnp.int32, (C, C), 0)
    ci = lax.broadcasted_iota(jnp.int32, (C, C), 1)
    x = ri ^ ci
    level = jnp.full((C, C), -1, jnp.int32)
    b = 1
    while b < C:
        level = level + jnp.where(x >= b, 1, 0)
        b *= 2
    lmap_s[0] = jnp.where(ci < ri, level, -1)
    lmap_s[1] = jnp.where(ci > ri, level, -1)

    def visit(c, f_ref, st, rev, first):
        r = pl.multiple_of(c * C, C)
        q = q_ref[pl.ds(r, C), :]
        v = v_ref[pl.ds(r, C), :]
        o, qt, kt, d = _hgrn_chunk(q, f_ref[pl.ds(r, C), :], v, lmap_s.at[1 if rev else 0], rev)
        o = o + _dot_nt(qt, st.astype(BF16))
        st = st * d[0:1, :] + _dot(v.astype(F32).T.astype(BF16), kt)
        if first:
            o_acc[pl.ds(r, C), :] = o
        else:
            o = _rms(o + o_acc[pl.ds(r, C), :], ng_ref[...])
            out_ref[pl.ds(r, C), :] = (o * jax.nn.silu(gate_ref[pl.ds(r, C), :])).astype(BF16)
        return st

    def step(first):
        def body(i, carry):
            st_f, st_b = carry
            return (visit(i, ff_ref, st_f, False, first),
                    visit(n - 1 - i, fb_ref, st_b, True, first))
        return body

    z = jnp.zeros((HG_DIM, HG_DIM), F32)
    carry = lax.fori_loop(0, n // 2, step(True), (z, z), unroll=HG_UNROLL)
    lax.fori_loop(n // 2, n, step(False), carry, unroll=HG_UNROLL)


def _hgrn(q, ff, fb, v, gate, ng, B, S):
    N = B * S
    C = HG_CHUNK
    assert (S // C) % 2 == 0
    blk = pl.BlockSpec((S, HG_DIM), lambda b, h: (b, h))
    return pl.pallas_call(
        _hgrn_kernel,
        grid=(B, HG_HEADS),
        in_specs=[blk, blk, blk, blk, blk, pl.BlockSpec((1, HG_DIM), lambda b, h: (0, 0))],
        out_specs=blk,
        out_shape=jax.ShapeDtypeStruct((N, HG_W), BF16),
        scratch_shapes=[pltpu.VMEM((S, HG_DIM), F32),
                        pltpu.VMEM((2, C, C), jnp.int32)],
        compiler_params=pltpu.CompilerParams(dimension_semantics=("arbitrary", "arbitrary"),
                                             vmem_limit_bytes=VMEM_LIMIT),
        name="hgrn",
    )(q, ff, fb, v, gate, ng)


def _attn_kernel(q_ref, k_ref, v_ref, ng_ref, out_ref, s_s, p_s, l_s):
    S = k_ref.shape[2]
    tq = ATT_TQ
    nq = S // tq

    rb_rows = tq // ATT_ROW_BLOCKS

    def scores(i, slot, rb):
        r0 = rb * rb_rows
        s_s[slot, r0:r0 + rb_rows, :] = _dot_nt(
            q_ref[0, 0, i * tq + r0:i * tq + r0 + rb_rows, :], k_ref[0, 0])

    def softmax(slot, rb):
        for r in range(rb * rb_rows, (rb + 1) * rb_rows, ATT_SUB):
            s = s_s[slot, r:r + ATT_SUB, :]
            p = jnp.exp2(s - jnp.max(s, axis=-1, keepdims=True))
            l_s[slot, r:r + ATT_SUB, :] = jnp.sum(p, axis=-1, keepdims=True)
            p_s[slot, r:r + ATT_SUB, :] = p.astype(BF16)

    def values(i, slot, rb):
        r0 = rb * rb_rows
        o = _dot(p_s[slot, r0:r0 + rb_rows, :], v_ref[0, 0]) / l_s[slot, r0:r0 + rb_rows, :]
        out_ref[i * tq + r0:i * tq + r0 + rb_rows, :] = _rms(o, ng_ref[...]).astype(BF16)

    for rb in range(ATT_ROW_BLOCKS):
        scores(0, 0, rb)
    for i in range(nq):
        for rb in range(ATT_ROW_BLOCKS):
            if i + 1 < nq:
                scores(i + 1, (i + 1) % 2, rb)
            softmax(i % 2, rb)
            values(i, i % 2, rb)


def _attn(qa, ka, va, ng, B, S):
    tq = ATT_TQ
    return pl.pallas_call(
        _attn_kernel,
        grid=(B, MLA_HEADS),
        in_specs=[pl.BlockSpec((1, 1, S, MLA_QK_PAD), lambda b, h: (b, h, 0, 0)),
                  pl.BlockSpec((1, 1, S, MLA_QK_PAD), lambda b, h: (b, h, 0, 0)),
                  pl.BlockSpec((1, 1, S, MLA_VHEAD), lambda b, h: (b, h, 0, 0)),
                  pl.BlockSpec((1, MLA_VHEAD), lambda b, h: (0, 0))],
        out_specs=pl.BlockSpec((S, MLA_VHEAD), lambda b, h: (b, h)),
        out_shape=jax.ShapeDtypeStruct((B * S, MLA_HEADS * MLA_VHEAD), BF16),
        scratch_shapes=[pltpu.VMEM((2, tq, S), F32),
                        pltpu.VMEM((2, tq, S), BF16),
                        pltpu.VMEM((2, tq, 1), F32)],
        compiler_params=pltpu.CompilerParams(
            dimension_semantics=("arbitrary", "arbitrary"),
            vmem_limit_bytes=VMEM_LIMIT),
        name="attn",
    )(qa, ka, va, ng)


def _ffn_kernel(x_ref, xp_ref, xn_ref, hg_ref, hgp_ref, hgn_ref, at_ref, atp_ref, atn_ref,
                lng_ref, lnb_ref, wo_ref, g1_ref, b1_ref,
                wu_ref, cw_ref, cb_ref, wd_ref, g2_ref, b2_ref, out_ref,
                h1_s, il_s, ext_s, u_s, act_s, acc_s, *, tiles_per_seq):
    tm = FFN_TM
    hrows = 2 * SUBLANES
    grp = tm // SUBLANES
    n = D_FF // FFN_FC
    i = pl.program_id(0)
    has_prev = jnp.where(i % tiles_per_seq != 0, 1.0, 0.0)
    has_next = jnp.where(i % tiles_per_seq != tiles_per_seq - 1, 1.0, 0.0)

    def with_halo(main, before, after):
        sub = lax.broadcasted_iota(jnp.int32, (hrows, main.shape[1]), 0)
        halo = jnp.where(sub == 0, before.astype(F32), jnp.where(sub == 1, after.astype(F32), 0.0))
        return jnp.concatenate([main, halo.astype(main.dtype)], axis=0)

    xx = with_halo(x_ref[...], xp_ref[SUBLANES - 1:SUBLANES, :], xn_ref[0:1, :])
    hgx = with_halo(hg_ref[...], hgp_ref[...].astype(F32)[hrows - 1:hrows, :], hgn_ref[...].astype(F32)[0:1, :])
    atx = with_halo(at_ref[...], atp_ref[...].astype(F32)[hrows - 1:hrows, :], atn_ref[...].astype(F32)[0:1, :])
    mix = _dot(hgx, wo_ref[0:HG_W, :]) + _dot(atx, wo_ref[HG_W:, :])
    h1x = _ln(DN_ALPHA * _ln(xx, lng_ref[...], lnb_ref[...]) + mix, g1_ref[...], b1_ref[...])
    h1_s[...] = h1x[0:tm, :]
    sub = lax.broadcasted_iota(jnp.int32, (hrows, D_MODEL), 0)
    valid = jnp.where(sub == 0, has_prev, jnp.where(sub == 1, has_next, 0.0))
    ext_s[tm:, :] = (h1x[tm:, :] * valid).astype(BF16)

    ngrp = D_MODEL // LANES
    for c in range(ngrp):
        for a in range(SUBLANES):
            il_s[c, pl.ds(a, grp, stride=SUBLANES), :] = h1_s[a * grp:(a + 1) * grp,
                                                              c * LANES:(c + 1) * LANES]
        ext_s[0:tm, c * LANES:(c + 1) * LANES] = il_s[c].astype(BF16)
    acc_s[...] = jnp.zeros_like(acc_s)

    half = tm // 2

    def chunk_cols(j, part):
        start = part * D_FF + j * FFN_FC
        return pl.ds(start if isinstance(start, int) else pl.multiple_of(start, FFN_FC), FFN_FC)

    def up(j, slot, part):
        u_s[slot, :, part * FFN_FC:(part + 1) * FFN_FC] = _dot(ext_s[...],
                                                               wu_ref[:, chunk_cols(j, part)])

    def glu(j, slot, part):
        r0, r1 = part * half, (part + 1) * half
        cw = jnp.concatenate([cw_ref[:, chunk_cols(j, 0)], cw_ref[:, chunk_cols(j, 1)]], axis=1)
        cb = jnp.concatenate([cb_ref[:, chunk_cols(j, 0)], cb_ref[:, chunk_cols(j, 1)]], axis=1)
        sub8 = lax.broadcasted_iota(jnp.int32, (SUBLANES, 2 * FFN_FC), 0)
        if part == 0:
            first = jnp.where(sub8 == 0, u_s[slot, tm:tm + 1, :],
                              pltpu.roll(u_s[slot, tm - SUBLANES:tm, :], 1, 0))
            prev = jnp.concatenate([first, u_s[slot, 0:r1 - SUBLANES, :]], axis=0)
            nxt = u_s[slot, SUBLANES:r1 + SUBLANES, :]
        else:
            last = jnp.where(sub8 == SUBLANES - 1, u_s[slot, tm + 1:tm + 2, :],
                             pltpu.roll(u_s[slot, 0:SUBLANES, :], SUBLANES - 1, 0))
            prev = u_s[slot, r0 - SUBLANES:r1 - SUBLANES, :]
            nxt = jnp.concatenate([u_s[slot, r0 + SUBLANES:tm, :], last], axis=0)
        c = prev * cw[0:1, :] + u_s[slot, r0:r1, :] * cw[1:2, :] + nxt * cw[2:3, :] + cb
        act_s[slot, r0:r1, :] = (jax.nn.gelu(c[:, :FFN_FC]) * c[:, FFN_FC:]).astype(BF16)

    def down(j, slot, part):
        c0 = part * 2 * LANES
        k0 = j * FFN_FC
        rows = pl.ds(k0 if isinstance(k0, int) else pl.multiple_of(k0, FFN_FC), FFN_FC)
        res = _dot(act_s[slot], wd_ref[rows, c0:c0 + 2 * LANES])
        acc_s[2 * part] += res[:, :LANES]
        acc_s[2 * part + 1] += res[:, LANES:]

    def stage(t, par):
        static = isinstance(t, int)
        do_up = not static or t + 1 < n
        do_glu = not static or 0 <= t < n
        do_down = not static or 1 <= t <= n
        for part in range(2):
            if do_up:
                up(t + 1, 1 - par, part)
            if do_glu:
                glu(t, par, part)
            if do_down:
                down(t - 1, 1 - par, 2 * part)
                down(t - 1, 1 - par, 2 * part + 1)

    stage(-1, 1)
    stage(0, 0)
    npairs = (n - 2) // 2

    def pair(jj, carry):
        t = 2 * jj + 1
        stage(t, 1)
        stage(t + 1, 0)
        return carry

    lax.fori_loop(0, npairs, pair, 0)
    for t in range(2 * npairs + 1, n + 1):
        stage(t, t % 2)

    for a in range(SUBLANES):
        ffn = jnp.concatenate([acc_s[c, pl.ds(a, grp, stride=SUBLANES), :] for c in range(ngrp)],
                              axis=1)
        rows = slice(a * grp, (a + 1) * grp)
        out_ref[rows, :] = _ln(DN_ALPHA * h1_s[rows, :] + ffn, g2_ref[...], b2_ref[...])


def _ffn(x2, hg, at, ln_g, ln_b, wo, g1, b1, wu, cw, cb, wd, g2, b2, S):
    N = x2.shape[0]
    tm = FFN_TM
    hrows = 2 * SUBLANES
    row = lambda i: (i, 0)
    full = lambda a: pl.BlockSpec(a.shape, lambda i: (0,) * a.ndim)

    def halo_specs(rows, width):
        per_tile = tm // rows
        last = N // rows - 1
        return [pl.BlockSpec((rows, width), lambda i: (jnp.maximum(i * per_tile - 1, 0), 0)),
                pl.BlockSpec((rows, width), lambda i: (jnp.minimum((i + 1) * per_tile, last), 0))]

    return pl.pallas_call(
        functools.partial(_ffn_kernel, tiles_per_seq=S // tm),
        grid=(N // tm,),
        in_specs=([pl.BlockSpec((tm, D_MODEL), row)] + halo_specs(SUBLANES, D_MODEL)
                  + [pl.BlockSpec((tm, HG_W), row)] + halo_specs(hrows, HG_W)
                  + [pl.BlockSpec((tm, HG_W), row)] + halo_specs(hrows, HG_W)
                  + [full(a) for a in (ln_g, ln_b, wo, g1, b1, wu, cw, cb, wd, g2, b2)]),
        out_specs=pl.BlockSpec((tm, D_MODEL), row),
        out_shape=jax.ShapeDtypeStruct((N, D_MODEL), F32),
        scratch_shapes=[pltpu.VMEM((tm, D_MODEL), F32),
                        pltpu.VMEM((D_MODEL // LANES, tm, LANES), F32),
                        pltpu.VMEM((tm + hrows, D_MODEL), BF16),
                        pltpu.VMEM((2, tm + hrows, 2 * FFN_FC), F32),
                        pltpu.VMEM((2, tm, FFN_FC), BF16),
                        pltpu.VMEM((D_MODEL // LANES, tm, LANES), F32)],
        compiler_params=pltpu.CompilerParams(dimension_semantics=("arbitrary",),
                                             vmem_limit_bytes=VMEM_LIMIT),
        name="ffn",
    )(x2, x2, x2, hg, hg, hg, at, at, at, ln_g, ln_b, wo, g1, b1, wu, cw, cb, wd, g2, b2)


def _swap_halves(w):
    half = MLA_ROPE // 2
    return jnp.concatenate([w[..., half:], w[..., :half]], axis=-1)


def kernel(x, positions, ln_in_g, ln_in_b, w_in, lb_fwd, lb_bwd, hg_norm_g, q_a_norm_g, w_q_b,
           kv_a_norm_g, w_kv_b, attn_norm_g, w_out, ln1_g, ln1_b, w_up, conv_w, conv_b, w_down,
           ln2_g, ln2_b):
    B, S, D = x.shape
    assert D == D_MODEL and w_in.shape[0] == 1 and lb_fwd.shape == (2, HG_W)
    assert S % HG_CHUNK == 0 and S % ATT_TQ == 0 and S % FFN_TM == 0 and D_FF % FFN_FC == 0
    N = B * S
    r1 = lambda a: a.reshape(1, -1)

    w_in0 = w_in[0]
    c_lat = 5 * HG_W
    c_kr = c_lat + MLA_Q_RANK + MLA_KV_RANK
    w_hg = w_in0[:, :c_lat].astype(BF16)
    w_kr = w_in0[:, c_kr:]
    w_lat = jnp.concatenate([w_in0[:, c_lat:c_kr], w_kr, _swap_halves(w_kr)], axis=1).astype(BF16)
    wq = w_q_b[0].reshape(MLA_Q_RANK, MLA_HEADS, MLA_NOPE + MLA_ROPE)
    wq_pe = wq[:, :, MLA_NOPE:]
    wqb = jnp.concatenate(
        [wq[:, :, :MLA_NOPE].reshape(MLA_Q_RANK, -1),
         jnp.concatenate([wq_pe, _swap_halves(wq_pe)], axis=-1).reshape(MLA_Q_RANK, -1)],
        axis=1).astype(BF16)
    wkvb = w_kv_b[0].astype(BF16)
    half = MLA_ROPE // 2
    inv_freq = 1.0 / (ROPE_THETA ** (jnp.arange(half, dtype=F32) / half))
    zeros = jnp.zeros((LANES - MLA_ROPE,), F32)
    invf = r1(jnp.tile(inv_freq, LANES // half))
    sgn = r1(jnp.concatenate([-jnp.ones((half,), F32), jnp.ones((half,), F32), zeros]))
    wu = w_up[0].astype(BF16)
    cw = conv_w[0]
    cb = r1(conv_b[0])
    wd = w_down[0].astype(BF16)

    x2 = x.reshape(N, D)
    pos2 = positions.reshape(N, 1)
    q, ff, fb, gate, v, qa, ka, va = _proj(
        x2, pos2, r1(ln_in_g), r1(ln_in_b), w_hg, w_lat, lb_fwd, lb_bwd,
        r1(q_a_norm_g[0]), wqb, r1(kv_a_norm_g[0]), wkvb, invf, sgn, B, S)
    hg = _hgrn(q, ff, fb, v, gate, r1(hg_norm_g[0]), B, S)
    at = _attn(qa, ka, va, r1(attn_norm_g[0]), B, S)
    out = _ffn(x2, hg, at, r1(ln_in_g), r1(ln_in_b), w_out[0].astype(BF16), r1(ln1_g[0]), r1(ln1_b[0]),
               wu, cw, cb, wd, r1(ln2_g[0]), r1(ln2_b[0]), S)
    return out.reshape(B, S, D)
```

```python
import functools

import jax
import jax.numpy as jnp
from jax import lax
from jax.experimental import pallas as pl
from jax.experimental.pallas import tpu as pltpu

D_MODEL = 1024
HG_HEADS = 4
HG_DIM = 128
HG_W = HG_HEADS * HG_DIM
MLA_HEADS = 4
MLA_Q_RANK = 384
MLA_KV_RANK = 128
MLA_NOPE = 128
MLA_ROPE = 64
MLA_VHEAD = 128
MLA_QK_PAD = 256
ROPE_THETA = 10000.0
D_FF = 2816
DN_ALPHA = 2.0 ** 0.25
NORM_EPS = 1e-5
LOG2E = 1.4426950408889634

LANES = 128
SUBLANES = 8
VMEM_LIMIT = 56 * 1024 * 1024

PROJ_TM = 1024
HG_CHUNK = 256
HG_BLOCKS = HG_CHUNK // SUBLANES
HG_UNROLL = 8
ATT_TQ = 256
ATT_SUB = 32
FFN_TM = 512
FFN_HEAD_ROWS = 512
FFN_FC = 256

F32 = jnp.float32
BF16 = jnp.bfloat16


def _ln(x, g, b):
    mu = jnp.mean(x, axis=-1, keepdims=True)
    xc = x - mu
    var = jnp.mean(xc * xc, axis=-1, keepdims=True)
    return xc * lax.rsqrt(var + NORM_EPS) * g + b


def _rms(x, g):
    return x * lax.rsqrt(jnp.mean(x * x, axis=-1, keepdims=True) + NORM_EPS) * g


def _dot(a, b):
    return jnp.dot(a, b, preferred_element_type=F32)


def _dot_nt(a, b):
    return lax.dot_general(a, b, (((1,), (1,)), ((), ())), preferred_element_type=F32)


def _proj_kernel(x_ref, pos_ref, lng_ref, lnb_ref, w_hg_ref, w_lat_ref, lbf_ref, lbb_ref,
                 qan_ref, wqb_ref, kvn_ref, wkvb_ref, invf_ref, sgn_ref,
                 q_ref, ff_ref, fb_ref, gate_ref, v_ref, qa_ref, ka_ref, va_ref):
    h = _ln(x_ref[...], lng_ref[...], lnb_ref[...]).astype(BF16)

    def lower_bound(ref):
        r0, r1 = ref[0:1, :], ref[1:2, :]
        m = jnp.maximum(r0, r1)
        e0, e1 = jnp.exp(r0 - m), jnp.exp(r1 - m)
        return e0 / (e0 + e1)

    def hg_cols(j):
        return _dot(h, w_hg_ref[:, j * HG_W:(j + 1) * HG_W])

    q_ref[...] = jax.nn.silu(hg_cols(0))
    v_ref[...] = hg_cols(1).astype(BF16)
    lbf = lower_bound(lbf_ref)
    ff_ref[...] = lbf + (1.0 - lbf) * jax.nn.sigmoid(hg_cols(2))
    lbb = lower_bound(lbb_ref)
    fb_ref[...] = lbb + (1.0 - lbb) * jax.nn.sigmoid(hg_cols(3))
    gate_ref[...] = hg_cols(4)

    hr = x_ref.shape[0] // 2
    pos = pos_ref[...].astype(F32)
    lane = lax.broadcasted_iota(jnp.int32, (hr, LANES), 1)
    ang = jnp.where(lane < MLA_ROPE, pos[0:hr, :], pos[hr:, :]) * invf_ref[...]
    cs, sn = jnp.cos(ang), jnp.sin(ang)
    sgn = sgn_ref[...]
    keep = jnp.abs(sgn)
    cos_t = jnp.concatenate([cs * keep, pltpu.roll(cs, MLA_ROPE, 1) * keep], axis=0)
    sin_t = jnp.concatenate([sn * sgn, pltpu.roll(sn, MLA_ROPE, 1) * sgn], axis=0)

    def rope(grp):
        return grp * cos_t + pltpu.roll(grp, MLA_ROPE, 1) * sin_t

    lat = _dot(h, w_lat_ref[...])
    qa = _rms(lat[:, :MLA_Q_RANK], qan_ref[...]).astype(BF16)
    kva = _rms(lat[:, MLA_Q_RANK:MLA_Q_RANK + MLA_KV_RANK], kvn_ref[...]).astype(BF16)
    k_pe = rope(lat[:, MLA_Q_RANK + MLA_KV_RANK:]).astype(BF16)

    scale = (MLA_NOPE + MLA_ROPE) ** -0.5 * LOG2E
    qh = _dot(qa, wqb_ref[...])
    kvh = _dot(kva, wkvb_ref[...])
    for hd in range(MLA_HEADS):
        qa_ref[0, hd, :, 0:LANES] = (qh[:, hd * LANES:(hd + 1) * LANES] * scale).astype(BF16)
        pe = rope(qh[:, (MLA_HEADS + hd) * LANES:(MLA_HEADS + hd + 1) * LANES])
        qa_ref[0, hd, :, LANES:2 * LANES] = (pe * scale).astype(BF16)
        ka_ref[0, hd, :, 0:LANES] = kvh[:, 2 * hd * LANES:(2 * hd + 1) * LANES].astype(BF16)
        ka_ref[0, hd, :, LANES:2 * LANES] = k_pe
        va_ref[0, hd, :, :] = kvh[:, (2 * hd + 1) * LANES:(2 * hd + 2) * LANES].astype(BF16)


def _proj(x2, pos2, ln_g, ln_b, w_hg, w_lat, lb_f, lb_b, qan, wqb, kvn, wkvb, invf, sgn, B, S):
    N = B * S
    tm = PROJ_TM
    spb = S // tm
    row = lambda i: (i, 0)
    const = lambda i: (0, 0)
    head = lambda i: (i // spb, 0, i % spb, 0)
    full = lambda a: pl.BlockSpec(a.shape, const)
    out_shape = (
        jax.ShapeDtypeStruct((N, HG_W), F32),
        jax.ShapeDtypeStruct((N, HG_W), F32),
        jax.ShapeDtypeStruct((N, HG_W), F32),
        jax.ShapeDtypeStruct((N, HG_W), F32),
        jax.ShapeDtypeStruct((N, HG_W), BF16),
        jax.ShapeDtypeStruct((B, MLA_HEADS, S, MLA_QK_PAD), BF16),
        jax.ShapeDtypeStruct((B, MLA_HEADS, S, MLA_QK_PAD), BF16),
        jax.ShapeDtypeStruct((B, MLA_HEADS, S, MLA_VHEAD), BF16),
    )
    tok = pl.BlockSpec((tm, HG_W), row)
    return pl.pallas_call(
        _proj_kernel,
        grid=(N // tm,),
        in_specs=[pl.BlockSpec((tm, D_MODEL), row), pl.BlockSpec((tm, 1), row),
                  full(ln_g), full(ln_b), full(w_hg), full(w_lat), full(lb_f), full(lb_b),
                  full(qan), full(wqb), full(kvn), full(wkvb), full(invf), full(sgn)],
        out_specs=(tok, tok, tok, tok, tok,
                   pl.BlockSpec((1, MLA_HEADS, tm, MLA_QK_PAD), head),
                   pl.BlockSpec((1, MLA_HEADS, tm, MLA_QK_PAD), head),
                   pl.BlockSpec((1, MLA_HEADS, tm, MLA_VHEAD), head)),
        out_shape=out_shape,
        compiler_params=pltpu.CompilerParams(dimension_semantics=("arbitrary",),
                                             vmem_limit_bytes=VMEM_LIMIT),
        name="proj",
    )(x2, pos2, ln_g, ln_b, w_hg, w_lat, lb_f, lb_b, qan, wqb, kvn, wkvb, invf, sgn)


def _blocks(x):
    return [x[i * SUBLANES:(i + 1) * SUBLANES] for i in range(x.shape[0] // SUBLANES)]


def _rows(blocks):
    return jnp.concatenate(blocks, axis=0)


def _hgrn_chunk(q, f, v, lmap_ref, rev):
    C = HG_CHUNK
    g = jnp.log(f) * LOG2E
    k = 1.0 - f
    row = lax.broadcasted_iota(jnp.int32, (C, HG_DIM), 0)

    def roll8(x, shift):
        return _rows([pltpu.roll(blk, shift, 0) for blk in _blocks(x)])

    P, T = g, g
    A = None
    for lvl in range(3):
        b = 1 << lvl
        bit = ((row >> lvl) & 1) == 1
        later = jnp.logical_not(bit) if rev else bit
        w = jnp.exp2(jnp.where(later, P, T - P))
        a_l = _dot_nt((q * w).astype(BF16), (k * w).astype(BF16))
        hit = lmap_ref[...] == lvl
        A = jnp.where(hit, a_l, 0.0) if A is None else jnp.where(hit, a_l, A)
        sib = roll8(T, b) if 2 * b == SUBLANES else jnp.where(bit, roll8(T, b), roll8(T, SUBLANES - b))
        P = P + jnp.where(later, sib, 0.0)
        T = T + sib

    Pb, Tb, qb, kb, Ab = _blocks(P), _blocks(T), _blocks(q), _blocks(k), _blocks(A)
    zero = jnp.zeros((SUBLANES, HG_DIM), F32)
    lvl, m = 3, 1
    while m < HG_BLOCKS:
        later = [(((i // m) & 1) == 1) != rev for i in range(HG_BLOCKS)]
        wb = [jnp.exp2(Pb[i] if later[i] else Tb[i] - Pb[i]) for i in range(HG_BLOCKS)]
        qt = _rows([qb[i] * wb[i] for i in range(HG_BLOCKS) if later[i]]).astype(BF16)
        kt = _rows([zero if later[i] else kb[i] * wb[i] for i in range(HG_BLOCKS)]).astype(BF16)
        a_l = _blocks(_dot_nt(qt, kt))
        n_later = 0
        for i in range(HG_BLOCKS):
            if later[i]:
                hit = lmap_ref[i * SUBLANES:(i + 1) * SUBLANES, :] == lvl
                Ab[i] = jnp.where(hit, a_l[n_later], Ab[i])
                n_later += 1
        sib = [Tb[i ^ m] for i in range(HG_BLOCKS)]
        Pb = [Pb[i] + sib[i] if later[i] else Pb[i] for i in range(HG_BLOCKS)]
        tot = {}
        for i in range(HG_BLOCKS):
            key = i // (2 * m)
            if key not in tot:
                tot[key] = Tb[i] + sib[i]
        Tb = [tot[i // (2 * m)] for i in range(HG_BLOCKS)]
        lvl, m = lvl + 1, 2 * m

    vf = v.astype(F32)
    o = _dot(_rows(Ab).astype(BF16), v) + jnp.sum(q * k, axis=1, keepdims=True) * vf
    qt = _rows([qb[i] * jnp.exp2(Pb[i]) for i in range(HG_BLOCKS)]).astype(BF16)
    kt = _rows([kb[i] * jnp.exp2(Tb[i] - Pb[i]) for i in range(HG_BLOCKS)]).astype(BF16)
    return o, qt, kt, jnp.exp2(Tb[0])


def _hgrn_kernel(q_ref, ff_ref, fb_ref, v_ref, gate_ref, ng_ref, out_ref, o_acc, lmap_s):
    S = q_ref.shape[0]
    C = HG_CHUNK
    n = S // C

    ri = lax.broadcasted_iota(jnp.int32, (C, C), 0)
    ci = lax.broadcasted_iota(jnp.int32, (C, C), 1)
    x = ri ^ ci
    level = jnp.full((C, C), -1, jnp.int32)
    b = 1
    while b < C:
        level = level + jnp.where(x >= b, 1, 0)
        b *= 2
    lmap_s[0] = jnp.where(ci < ri, level, -1)
    lmap_s[1] = jnp.where(ci > ri, level, -1)

    def visit(c, f_ref, st, rev, first):
        r = pl.multiple_of(c * C, C)
        q = q_ref[pl.ds(r, C), :]
        v = v_ref[pl.ds(r, C), :]
        o, qt, kt, d = _hgrn_chunk(q, f_ref[pl.ds(r, C), :], v, lmap_s.at[1 if rev else 0], rev)
        o = o + _dot_nt(qt, st.astype(BF16))
        st = st * d[0:1, :] + _dot(v.astype(F32).T.astype(BF16), kt)
        if first:
            o_acc[pl.ds(r, C), :] = o
        else:
            o = _rms(o + o_acc[pl.ds(r, C), :], ng_ref[...])
            out_ref[pl.ds(r, C), :] = (o * jax.nn.silu(gate_ref[pl.ds(r, C), :])).astype(BF16)
        return st

    def step(first):
        def body(i, carry):
            st_f, st_b = carry
            return (visit(i, ff_ref, st_f, False, first),
                    visit(n - 1 - i, fb_ref, st_b, True, first))
        return body

    z = jnp.zeros((HG_DIM, HG_DIM), F32)
    carry = lax.fori_loop(0, n // 2, step(True), (z, z), unroll=HG_UNROLL)
    lax.fori_loop(n // 2, n, step(False), carry, unroll=HG_UNROLL)


def _hgrn(q, ff, fb, v, gate, ng, B, S):
    N = B * S
    C = HG_CHUNK
    assert (S // C) % 2 == 0
    blk = pl.BlockSpec((S, HG_DIM), lambda b, h: (b, h))
    return pl.pallas_call(
        _hgrn_kernel,
        grid=(B, HG_HEADS),
        in_specs=[blk, blk, blk, blk, blk, pl.BlockSpec((1, HG_DIM), lambda b, h: (0, 0))],
        out_specs=blk,
        out_shape=jax.ShapeDtypeStruct((N, HG_W), BF16),
        scratch_shapes=[pltpu.VMEM((S, HG_DIM), F32),
                        pltpu.VMEM((2, C, C), jnp.int32)],
        compiler_params=pltpu.CompilerParams(dimension_semantics=("arbitrary", "arbitrary"),
                                             vmem_limit_bytes=VMEM_LIMIT),
        name="hgrn",
    )(q, ff, fb, v, gate, ng)


def _attn_kernel(q_ref, k_ref, v_ref, ng_ref, out_ref, s_s, p_s, l_s):
    S = k_ref.shape[2]
    tq = ATT_TQ
    nq = S // tq

    def scores(i, slot):
        s_s[slot] = _dot_nt(q_ref[0, 0, i * tq:(i + 1) * tq, :], k_ref[0, 0])

    def softmax(slot):
        for r in range(0, tq, ATT_SUB):
            s = s_s[slot, r:r + ATT_SUB, :]
            p = jnp.exp2(s - jnp.max(s, axis=-1, keepdims=True))
            l_s[slot, r:r + ATT_SUB, :] = jnp.sum(p, axis=-1, keepdims=True)
            p_s[slot, r:r + ATT_SUB, :] = p.astype(BF16)

    def values(i, slot):
        o = _dot(p_s[slot], v_ref[0, 0]) / l_s[slot]
        out_ref[i * tq:(i + 1) * tq, :] = _rms(o, ng_ref[...]).astype(BF16)

    scores(0, 0)
    for i in range(nq):
        if i + 1 < nq:
            scores(i + 1, (i + 1) % 2)
        softmax(i % 2)
        values(i, i % 2)


def _attn(qa, ka, va, ng, B, S):
    tq = ATT_TQ
    return pl.pallas_call(
        _attn_kernel,
        grid=(B, MLA_HEADS),
        in_specs=[pl.BlockSpec((1, 1, S, MLA_QK_PAD), lambda b, h: (b, h, 0, 0)),
                  pl.BlockSpec((1, 1, S, MLA_QK_PAD), lambda b, h: (b, h, 0, 0)),
                  pl.BlockSpec((1, 1, S, MLA_VHEAD), lambda b, h: (b, h, 0, 0)),
                  pl.BlockSpec((1, MLA_VHEAD), lambda b, h: (0, 0))],
        out_specs=pl.BlockSpec((S, MLA_VHEAD), lambda b, h: (b, h)),
        out_shape=jax.ShapeDtypeStruct((B * S, MLA_HEADS * MLA_VHEAD), BF16),
        scratch_shapes=[pltpu.VMEM((2, tq, S), F32),
                        pltpu.VMEM((2, tq, S), BF16),
                        pltpu.VMEM((2, tq, 1), F32)],
        compiler_params=pltpu.CompilerParams(
            dimension_semantics=("arbitrary", "arbitrary"),
            vmem_limit_bytes=VMEM_LIMIT),
        name="attn",
    )(qa, ka, va, ng)


def _ffn_kernel(x_ref, xp_ref, xn_ref, hg_ref, hgp_ref, hgn_ref, at_ref, atp_ref, atn_ref,
                lng_ref, lnb_ref, wo_ref, g1_ref, b1_ref,
                wu_ref, cw_ref, cb_ref, wd_ref, g2_ref, b2_ref, out_ref,
                ext_s, u_s, act_s, acc_s, *, tiles_per_seq):
    tm = FFN_TM
    hrows = 2 * SUBLANES
    grp = tm // SUBLANES
    n = D_FF // FFN_FC
    i = pl.program_id(0)
    has_prev = jnp.where(i % tiles_per_seq != 0, 1.0, 0.0)
    has_next = jnp.where(i % tiles_per_seq != tiles_per_seq - 1, 1.0, 0.0)

    def with_halo(main, before, after):
        sub = lax.broadcasted_iota(jnp.int32, (hrows, main.shape[1]), 0)
        halo = jnp.where(sub == 0, before.astype(F32), jnp.where(sub == 1, after.astype(F32), 0.0))
        return jnp.concatenate([main, halo.astype(main.dtype)], axis=0)

    def token_mix(xx, hgx, atx):
        mix = _dot(hgx, wo_ref[0:HG_W, :]) + _dot(atx, wo_ref[HG_W:, :])
        return _ln(DN_ALPHA * _ln(xx, lng_ref[...], lnb_ref[...]) + mix, g1_ref[...], b1_ref[...])

    for r0 in range(0, tm - FFN_HEAD_ROWS, FFN_HEAD_ROWS):
        rows = slice(r0, r0 + FFN_HEAD_ROWS)
        out_ref[rows, :] = token_mix(x_ref[rows, :], hg_ref[rows, :], at_ref[rows, :])
    rows = slice(tm - FFN_HEAD_ROWS, tm)
    h1x = token_mix(
        with_halo(x_ref[rows, :], xp_ref[SUBLANES - 1:SUBLANES, :], xn_ref[0:1, :]),
        with_halo(hg_ref[rows, :], hgp_ref[...].astype(F32)[hrows - 1:hrows, :],
                  hgn_ref[...].astype(F32)[0:1, :]),
        with_halo(at_ref[rows, :], atp_ref[...].astype(F32)[hrows - 1:hrows, :],
                  atn_ref[...].astype(F32)[0:1, :]))
    out_ref[rows, :] = h1x[0:FFN_HEAD_ROWS, :]
    sub = lax.broadcasted_iota(jnp.int32, (hrows, D_MODEL), 0)
    valid = jnp.where(sub == 0, has_prev, jnp.where(sub == 1, has_next, 0.0))
    ext_s[tm:, :] = (h1x[FFN_HEAD_ROWS:, :] * valid).astype(BF16)

    ngrp = D_MODEL // LANES
    for c in range(ngrp):
        for a in range(SUBLANES):
            acc_s[c, pl.ds(a, grp, stride=SUBLANES), :] = out_ref[a * grp:(a + 1) * grp,
                                                                  c * LANES:(c + 1) * LANES]
        ext_s[0:tm, c * LANES:(c + 1) * LANES] = acc_s[c].astype(BF16)
    acc_s[...] = jnp.zeros_like(acc_s)

    half = tm // 2

    def chunk_cols(j, part):
        start = part * D_FF + j * FFN_FC
        return pl.ds(start if isinstance(start, int) else pl.multiple_of(start, FFN_FC), FFN_FC)

    def up(j, slot, part):
        u_s[slot, :, part * FFN_FC:(part + 1) * FFN_FC] = _dot(ext_s[...],
                                                               wu_ref[:, chunk_cols(j, part)])

    def glu(j, slot, part):
        r0, r1 = part * half, (part + 1) * half
        cw = jnp.concatenate([cw_ref[:, chunk_cols(j, 0)], cw_ref[:, chunk_cols(j, 1)]], axis=1)
        cb = jnp.concatenate([cb_ref[:, chunk_cols(j, 0)], cb_ref[:, chunk_cols(j, 1)]], axis=1)
        sub8 = lax.broadcasted_iota(jnp.int32, (SUBLANES, 2 * FFN_FC), 0)
        if part == 0:
            first = jnp.where(sub8 == 0, u_s[slot, tm:tm + 1, :],
                              pltpu.roll(u_s[slot, tm - SUBLANES:tm, :], 1, 0))
            prev = jnp.concatenate([first, u_s[slot, 0:r1 - SUBLANES, :]], axis=0)
            nxt = u_s[slot, SUBLANES:r1 + SUBLANES, :]
        else:
            last = jnp.where(sub8 == SUBLANES - 1, u_s[slot, tm + 1:tm + 2, :],
                             pltpu.roll(u_s[slot, 0:SUBLANES, :], SUBLANES - 1, 0))
            prev = u_s[slot, r0 - SUBLANES:r1 - SUBLANES, :]
            nxt = jnp.concatenate([u_s[slot, r0 + SUBLANES:tm, :], last], axis=0)
        c = prev * cw[0:1, :] + u_s[slot, r0:r1, :] * cw[1:2, :] + nxt * cw[2:3, :] + cb
        act_s[slot, r0:r1, :] = (jax.nn.gelu(c[:, :FFN_FC]) * c[:, FFN_FC:]).astype(BF16)

    def down(j, slot, part):
        c0 = part * 2 * LANES
        k0 = j * FFN_FC
        rows = pl.ds(k0 if isinstance(k0, int) else pl.multiple_of(k0, FFN_FC), FFN_FC)
        res = _dot(act_s[slot], wd_ref[rows, c0:c0 + 2 * LANES])
        acc_s[2 * part] += res[:, :LANES]
        acc_s[2 * part + 1] += res[:, LANES:]

    def stage(t, par):
        static = isinstance(t, int)
        do_up = not static or t + 1 < n
        do_glu = not static or 0 <= t < n
        do_down = not static or 1 <= t <= n
        for part in range(2):
            if do_up:
                up(t + 1, 1 - par, part)
            if do_glu:
                glu(t, par, part)
            if do_down:
                down(t - 1, 1 - par, 2 * part)
                down(t - 1, 1 - par, 2 * part + 1)

    stage(-1, 1)
    stage(0, 0)
    npairs = (n - 2) // 2

    def pair(jj, carry):
        t = 2 * jj + 1
        stage(t, 1)
        stage(t + 1, 0)
        return carry

    lax.fori_loop(0, npairs, pair, 0)
    for t in range(2 * npairs + 1, n + 1):
        stage(t, t % 2)

    for a in range(SUBLANES):
        ffn = jnp.concatenate([acc_s[c, pl.ds(a, grp, stride=SUBLANES), :] for c in range(ngrp)],
                              axis=1)
        rows = slice(a * grp, (a + 1) * grp)
        out_ref[rows, :] = _ln(DN_ALPHA * out_ref[rows, :] + ffn, g2_ref[...], b2_ref[...])


def _ffn(x2, hg, at, ln_g, ln_b, wo, g1, b1, wu, cw, cb, wd, g2, b2, S):
    N = x2.shape[0]
    tm = FFN_TM
    hrows = 2 * SUBLANES
    row = lambda i: (i, 0)
    full = lambda a: pl.BlockSpec(a.shape, lambda i: (0,) * a.ndim)

    def halo_specs(rows, width):
        per_tile = tm // rows
        last = N // rows - 1
        return [pl.BlockSpec((rows, width), lambda i: (jnp.maximum(i * per_tile - 1, 0), 0)),
                pl.BlockSpec((rows, width), lambda i: (jnp.minimum((i + 1) * per_tile, last), 0))]

    return pl.pallas_call(
        functools.partial(_ffn_kernel, tiles_per_seq=S // tm),
        grid=(N // tm,),
        in_specs=([pl.BlockSpec((tm, D_MODEL), row)] + halo_specs(SUBLANES, D_MODEL)
                  + [pl.BlockSpec((tm, HG_W), row)] + halo_specs(hrows, HG_W)
                  + [pl.BlockSpec((tm, HG_W), row)] + halo_specs(hrows, HG_W)
                  + [full(a) for a in (ln_g, ln_b, wo, g1, b1, wu, cw, cb, wd, g2, b2)]),
        out_specs=pl.BlockSpec((tm, D_MODEL), row),
        out_shape=jax.ShapeDtypeStruct((N, D_MODEL), F32),
        scratch_shapes=[pltpu.VMEM((tm + hrows, D_MODEL), BF16),
                        pltpu.VMEM((2, tm + hrows, 2 * FFN_FC), F32),
                        pltpu.VMEM((2, tm, FFN_FC), BF16),
                        pltpu.VMEM((D_MODEL // LANES, tm, LANES), F32)],
        compiler_params=pltpu.CompilerParams(dimension_semantics=("arbitrary",),
                                             vmem_limit_bytes=VMEM_LIMIT),
        name="ffn",
    )(x2, x2, x2, hg, hg, hg, at, at, at, ln_g, ln_b, wo, g1, b1, wu, cw, cb, wd, g2, b2)


def _swap_halves(w):
    half = MLA_ROPE // 2
    return jnp.concatenate([w[..., half:], w[..., :half]], axis=-1)


def kernel(x, positions, ln_in_g, ln_in_b, w_in, lb_fwd, lb_bwd, hg_norm_g, q_a_norm_g, w_q_b,
           kv_a_norm_g, w_kv_b, attn_norm_g, w_out, ln1_g, ln1_b, w_up, conv_w, conv_b, w_down,
           ln2_g, ln2_b):
    B, S, D = x.shape
    assert D == D_MODEL and w_in.shape[0] == 1 and lb_fwd.shape == (2, HG_W)
    assert S % HG_CHUNK == 0 and S % ATT_TQ == 0 and S % FFN_TM == 0 and D_FF % FFN_FC == 0
    N = B * S
    r1 = lambda a: a.reshape(1, -1)

    w_in0 = w_in[0]
    c_lat = 5 * HG_W
    c_kr = c_lat + MLA_Q_RANK + MLA_KV_RANK
    w_hg = w_in0[:, :c_lat].astype(BF16)
    w_kr = w_in0[:, c_kr:]
    w_lat = jnp.concatenate([w_in0[:, c_lat:c_kr], w_kr, _swap_halves(w_kr)], axis=1).astype(BF16)
    wq = w_q_b[0].reshape(MLA_Q_RANK, MLA_HEADS, MLA_NOPE + MLA_ROPE)
    wq_pe = wq[:, :, MLA_NOPE:]
    wqb = jnp.concatenate(
        [wq[:, :, :MLA_NOPE].reshape(MLA_Q_RANK, -1),
         jnp.concatenate([wq_pe, _swap_halves(wq_pe)], axis=-1).reshape(MLA_Q_RANK, -1)],
        axis=1).astype(BF16)
    wkvb = w_kv_b[0].astype(BF16)
    half = MLA_ROPE // 2
    inv_freq = 1.0 / (ROPE_THETA ** (jnp.arange(half, dtype=F32) / half))
    zeros = jnp.zeros((LANES - MLA_ROPE,), F32)
    invf = r1(jnp.tile(inv_freq, LANES // half))
    sgn = r1(jnp.concatenate([-jnp.ones((half,), F32), jnp.ones((half,), F32), zeros]))
    wu = w_up[0].astype(BF16)
    cw = conv_w[0]
    cb = r1(conv_b[0])
    wd = w_down[0].astype(BF16)

    x2 = x.reshape(N, D)
    pos2 = positions.reshape(N, 1)
    q, ff, fb, gate, v, qa, ka, va = _proj(
        x2, pos2, r1(ln_in_g), r1(ln_in_b), w_hg, w_lat, lb_fwd, lb_bwd,
        r1(q_a_norm_g[0]), wqb, r1(kv_a_norm_g[0]), wkvb, invf, sgn, B, S)
    hg = _hgrn(q, ff, fb, v, gate, r1(hg_norm_g[0]), B, S)
    at = _attn(qa, ka, va, r1(attn_norm_g[0]), B, S)
    out = _ffn(x2, hg, at, r1(ln_in_g), r1(ln_in_b), w_out[0].astype(BF16), r1(ln1_g[0]), r1(ln1_b[0]),
               wu, cw, cb, wd, r1(ln2_g[0]), r1(ln2_b[0]), S)
    return out.reshape(B, S, D)
```

```python
import functools

import jax
import jax.numpy as jnp
from jax import lax
from jax.experimental import pallas as pl
from jax.experimental.pallas import tpu as pltpu

D_MODEL = 1024
HG_HEADS = 4
HG_DIM = 128
HG_W = HG_HEADS * HG_DIM
MLA_HEADS = 4
MLA_Q_RANK = 384
MLA_KV_RANK = 128
MLA_NOPE = 128
MLA_ROPE = 64
MLA_VHEAD = 128
MLA_QK_PAD = 256
ROPE_THETA = 10000.0
D_FF = 2816
DN_ALPHA = 2.0 ** 0.25
NORM_EPS = 1e-5
LOG2E = 1.4426950408889634

LANES = 128
SUBLANES = 8
VMEM_LIMIT = 56 * 1024 * 1024

PROJ_TM = 1024
HG_CHUNK = 256
HG_BLOCKS = HG_CHUNK // SUBLANES
HG_UNROLL = 8
ATT_TQ = 256
ATT_SUB = 32
FFN_TM = 512
FFN_FC = 256

F32 = jnp.float32
BF16 = jnp.bfloat16


def _ln(x, g, b):
    mu = jnp.mean(x, axis=-1, keepdims=True)
    xc = x - mu
    var = jnp.mean(xc * xc, axis=-1, keepdims=True)
    return xc * lax.rsqrt(var + NORM_EPS) * g + b


def _rms(x, g):
    return x * lax.rsqrt(jnp.mean(x * x, axis=-1, keepdims=True) + NORM_EPS) * g


def _dot(a, b):
    return jnp.dot(a, b, preferred_element_type=F32)


def _dot_nt(a, b):
    return lax.dot_general(a, b, (((1,), (1,)), ((), ())), preferred_element_type=F32)


def _proj_kernel(x_ref, pos_ref, lng_ref, lnb_ref, w_hg_ref, w_lat_ref, lbf_ref, lbb_ref,
                 qan_ref, wqb_ref, kvn_ref, wkvb_ref, invf_ref, sgn_ref,
                 q_ref, ff_ref, fb_ref, gate_ref, v_ref, qa_ref, ka_ref, va_ref):
    h = _ln(x_ref[...], lng_ref[...], lnb_ref[...]).astype(BF16)

    def lower_bound(ref):
        r0, r1 = ref[0:1, :], ref[1:2, :]
        m = jnp.maximum(r0, r1)
        e0, e1 = jnp.exp(r0 - m), jnp.exp(r1 - m)
        return e0 / (e0 + e1)

    def hg_cols(j):
        return _dot(h, w_hg_ref[:, j * HG_W:(j + 1) * HG_W])

    q_ref[...] = jax.nn.silu(hg_cols(0))
    v_ref[...] = hg_cols(1).astype(BF16)
    lbf = lower_bound(lbf_ref)
    ff_ref[...] = lbf + (1.0 - lbf) * jax.nn.sigmoid(hg_cols(2))
    lbb = lower_bound(lbb_ref)
    fb_ref[...] = lbb + (1.0 - lbb) * jax.nn.sigmoid(hg_cols(3))
    gate_ref[...] = hg_cols(4)

    hr = x_ref.shape[0] // 2
    pos_t = pos_ref[...].astype(F32).T
    pos = jnp.concatenate([pos_t[:, j:j + 1] for j in range(pos_t.shape[1])], axis=0)
    lane = lax.broadcasted_iota(jnp.int32, (hr, LANES), 1)
    ang = jnp.where(lane < MLA_ROPE, pos[0:hr, :], pos[hr:, :]) * invf_ref[...]
    cs, sn = jnp.cos(ang), jnp.sin(ang)
    sgn = sgn_ref[...]
    keep = jnp.abs(sgn)
    cos_t = jnp.concatenate([cs * keep, pltpu.roll(cs, MLA_ROPE, 1) * keep], axis=0)
    sin_t = jnp.concatenate([sn * sgn, pltpu.roll(sn, MLA_ROPE, 1) * sgn], axis=0)

    def rope(grp):
        return grp * cos_t + pltpu.roll(grp, MLA_ROPE, 1) * sin_t

    lat = _dot(h, w_lat_ref[...])
    qa = _rms(lat[:, :MLA_Q_RANK], qan_ref[...]).astype(BF16)
    kva = _rms(lat[:, MLA_Q_RANK:MLA_Q_RANK + MLA_KV_RANK], kvn_ref[...]).astype(BF16)
    k_pe = rope(lat[:, MLA_Q_RANK + MLA_KV_RANK:]).astype(BF16)

    scale = (MLA_NOPE + MLA_ROPE) ** -0.5 * LOG2E
    qh = _dot(qa, wqb_ref[...])
    kvh = _dot(kva, wkvb_ref[...])
    for hd in range(MLA_HEADS):
        qa_ref[0, hd, :, 0:LANES] = (qh[:, hd * LANES:(hd + 1) * LANES] * scale).astype(BF16)
        pe = rope(qh[:, (MLA_HEADS + hd) * LANES:(MLA_HEADS + hd + 1) * LANES])
        qa_ref[0, hd, :, LANES:2 * LANES] = (pe * scale).astype(BF16)
        ka_ref[0, hd, :, 0:LANES] = kvh[:, 2 * hd * LANES:(2 * hd + 1) * LANES].astype(BF16)
        ka_ref[0, hd, :, LANES:2 * LANES] = k_pe
        va_ref[0, hd, :, :] = kvh[:, (2 * hd + 1) * LANES:(2 * hd + 2) * LANES].astype(BF16)


def _proj(x2, pos2, ln_g, ln_b, w_hg, w_lat, lb_f, lb_b, qan, wqb, kvn, wkvb, invf, sgn, B, S):
    N = B * S
    tm = PROJ_TM
    spb = S // tm
    row = lambda i: (i, 0)
    const = lambda i: (0, 0)
    head = lambda i: (i // spb, 0, i % spb, 0)
    full = lambda a: pl.BlockSpec(a.shape, const)
    out_shape = (
        jax.ShapeDtypeStruct((N, HG_W), F32),
        jax.ShapeDtypeStruct((N, HG_W), F32),
        jax.ShapeDtypeStruct((N, HG_W), F32),
        jax.ShapeDtypeStruct((N, HG_W), F32),
        jax.ShapeDtypeStruct((N, HG_W), BF16),
        jax.ShapeDtypeStruct((B, MLA_HEADS, S, MLA_QK_PAD), BF16),
        jax.ShapeDtypeStruct((B, MLA_HEADS, S, MLA_QK_PAD), BF16),
        jax.ShapeDtypeStruct((B, MLA_HEADS, S, MLA_VHEAD), BF16),
    )
    tok = pl.BlockSpec((tm, HG_W), row)
    return pl.pallas_call(
        _proj_kernel,
        grid=(N // tm,),
        in_specs=[pl.BlockSpec((tm, D_MODEL), row), pl.BlockSpec((tm // LANES, LANES), row),
                  full(ln_g), full(ln_b), full(w_hg), full(w_lat), full(lb_f), full(lb_b),
                  full(qan), full(wqb), full(kvn), full(wkvb), full(invf), full(sgn)],
        out_specs=(tok, tok, tok, tok, tok,
                   pl.BlockSpec((1, MLA_HEADS, tm, MLA_QK_PAD), head),
                   pl.BlockSpec((1, MLA_HEADS, tm, MLA_QK_PAD), head),
                   pl.BlockSpec((1, MLA_HEADS, tm, MLA_VHEAD), head)),
        out_shape=out_shape,
        compiler_params=pltpu.CompilerParams(dimension_semantics=("arbitrary",),
                                             vmem_limit_bytes=VMEM_LIMIT),
        name="proj",
    )(x2, pos2, ln_g, ln_b, w_hg, w_lat, lb_f, lb_b, qan, wqb, kvn, wkvb, invf, sgn)


def _blocks(x):
    return [x[i * SUBLANES:(i + 1) * SUBLANES] for i in range(x.shape[0] // SUBLANES)]


def _rows(blocks):
    return jnp.concatenate(blocks, axis=0)


def _hgrn_chunk(q, f, v, lmap_ref, rev):
    C = HG_CHUNK
    g = jnp.log(f) * LOG2E
    k = 1.0 - f
    row = lax.broadcasted_iota(jnp.int32, (C, HG_DIM), 0)

    def roll8(x, shift):
        return _rows([pltpu.roll(blk, shift, 0) for blk in _blocks(x)])

    P, T = g, g
    A = None
    for lvl in range(3):
        b = 1 << lvl
        bit = ((row >> lvl) & 1) == 1
        later = jnp.logical_not(bit) if rev else bit
        w = jnp.exp2(jnp.where(later, P, T - P))
        a_l = _dot_nt((q * w).astype(BF16), (k * w).astype(BF16))
        hit = lmap_ref[...] == lvl
        A = jnp.where(hit, a_l, 0.0) if A is None else jnp.where(hit, a_l, A)
        sib = roll8(T, b) if 2 * b == SUBLANES else jnp.where(bit, roll8(T, b), roll8(T, SUBLANES - b))
        P = P + jnp.where(later, sib, 0.0)
        T = T + sib

    Pb, Tb, qb, kb, Ab = _blocks(P), _blocks(T), _blocks(q), _blocks(k), _blocks(A)
    zero = jnp.zeros((SUBLANES, HG_DIM), F32)
    lvl, m = 3, 1
    while m < HG_BLOCKS:
        later = [(((i // m) & 1) == 1) != rev for i in range(HG_BLOCKS)]
        wb = [jnp.exp2(Pb[i] if later[i] else Tb[i] - Pb[i]) for i in range(HG_BLOCKS)]
        qt = _rows([qb[i] * wb[i] for i in range(HG_BLOCKS) if later[i]]).astype(BF16)
        kt = _rows([zero if later[i] else kb[i] * wb[i] for i in range(HG_BLOCKS)]).astype(BF16)
        a_l = _blocks(_dot_nt(qt, kt))
        n_later = 0
        for i in range(HG_BLOCKS):
            if later[i]:
                hit = lmap_ref[i * SUBLANES:(i + 1) * SUBLANES, :] == lvl
                Ab[i] = jnp.where(hit, a_l[n_later], Ab[i])
                n_later += 1
        sib = [Tb[i ^ m] for i in range(HG_BLOCKS)]
        Pb = [Pb[i] + sib[i] if later[i] else Pb[i] for i in range(HG_BLOCKS)]
        tot = {}
        for i in range(HG_BLOCKS):
            key = i // (2 * m)
            if key not in tot:
                tot[key] = Tb[i] + sib[i]
        Tb = [tot[i // (2 * m)] for i in range(HG_BLOCKS)]
        lvl, m = lvl + 1, 2 * m

    vf = v.astype(F32)
    o = _dot(_rows(Ab).astype(BF16), v) + jnp.sum(q * k, axis=1, keepdims=True) * vf
    qt = _rows([qb[i] * jnp.exp2(Pb[i]) for i in range(HG_BLOCKS)]).astype(BF16)
    kt = _rows([kb[i] * jnp.exp2(Tb[i] - Pb[i]) for i in range(HG_BLOCKS)]).astype(BF16)
    return o, qt, kt, jnp.exp2(Tb[0])


def _hgrn_kernel(q_ref, ff_ref, fb_ref, v_ref, gate_ref, ng_ref, out_ref, o_acc, lmap_s):
    S = q_ref.shape[0]
    C = HG_CHUNK
    n = S // C

    ri = lax.broadcasted_iota(jnp.int32, (C, C), 0)
    ci = lax.broadcasted_iota(jnp.int32, (C, C), 1)
    x = ri ^ ci
    level = jnp.full((C, C), -1, jnp.int32)
    b = 1
    while b < C:
        level = level + jnp.where(x >= b, 1, 0)
        b *= 2
    lmap_s[0] = jnp.where(ci < ri, level, -1)
    lmap_s[1] = jnp.where(ci > ri, level, -1)

    def visit(c, f_ref, st, rev, first):
        r = pl.multiple_of(c * C, C)
        q = q_ref[pl.ds(r, C), :]
        v = v_ref[pl.ds(r, C), :]
        o, qt, kt, d = _hgrn_chunk(q, f_ref[pl.ds(r, C), :], v, lmap_s.at[1 if rev else 0], rev)
        o = o + _dot_nt(qt, st.astype(BF16))
        st = st * d[0:1, :] + _dot(v.astype(F32).T.astype(BF16), kt)
        if first:
            o_acc[pl.ds(r, C), :] = o
        else:
            o = _rms(o + o_acc[pl.ds(r, C), :], ng_ref[...])
            out_ref[pl.ds(r, C), :] = (o * jax.nn.silu(gate_ref[pl.ds(r, C), :])).astype(BF16)
        return st

    def step(first):
        def body(i, carry):
            st_f, st_b = carry
            return (visit(i, ff_ref, st_f, False, first),
                    visit(n - 1 - i, fb_ref, st_b, True, first))
        return body

    z = jnp.zeros((HG_DIM, HG_DIM), F32)
    carry = lax.fori_loop(0, n // 2, step(True), (z, z), unroll=HG_UNROLL)
    lax.fori_loop(n // 2, n, step(False), carry, unroll=HG_UNROLL)


def _hgrn(q, ff, fb, v, gate, ng, B, S):
    N = B * S
    C = HG_CHUNK
    assert (S // C) % 2 == 0
    blk = pl.BlockSpec((S, HG_DIM), lambda b, h: (b, h))
    return pl.pallas_call(
        _hgrn_kernel,
        grid=(B, HG_HEADS),
        in_specs=[blk, blk, blk, blk, blk, pl.BlockSpec((1, HG_DIM), lambda b, h: (0, 0))],
        out_specs=blk,
        out_shape=jax.ShapeDtypeStruct((N, HG_W), BF16),
        scratch_shapes=[pltpu.VMEM((S, HG_DIM), F32),
                        pltpu.VMEM((2, C, C), jnp.int32)],
        compiler_params=pltpu.CompilerParams(dimension_semantics=("arbitrary", "arbitrary"),
                                             vmem_limit_bytes=VMEM_LIMIT),
        name="hgrn",
    )(q, ff, fb, v, gate, ng)


def _attn_kernel(q_ref, k_ref, v_ref, ng_ref, out_ref, s_s, p_s, l_s):
    S = k_ref.shape[2]
    tq = ATT_TQ
    nq = S // tq

    def scores(i, slot):
        s_s[slot] = _dot_nt(q_ref[0, 0, i * tq:(i + 1) * tq, :], k_ref[0, 0])

    def softmax(slot):
        for r in range(0, tq, ATT_SUB):
            s = s_s[slot, r:r + ATT_SUB, :]
            p = jnp.exp2(s - jnp.max(s, axis=-1, keepdims=True))
            l_s[slot, r:r + ATT_SUB, :] = jnp.sum(p, axis=-1, keepdims=True)
            p_s[slot, r:r + ATT_SUB, :] = p.astype(BF16)

    def values(i, slot):
        o = _dot(p_s[slot], v_ref[0, 0]) / l_s[slot]
        out_ref[i * tq:(i + 1) * tq, :] = _rms(o, ng_ref[...]).astype(BF16)

    scores(0, 0)
    for i in range(nq):
        if i + 1 < nq:
            scores(i + 1, (i + 1) % 2)
        softmax(i % 2)
        values(i, i % 2)


def _attn(qa, ka, va, ng, B, S):
    tq = ATT_TQ
    return pl.pallas_call(
        _attn_kernel,
        grid=(B, MLA_HEADS),
        in_specs=[pl.BlockSpec((1, 1, S, MLA_QK_PAD), lambda b, h: (b, h, 0, 0)),
                  pl.BlockSpec((1, 1, S, MLA_QK_PAD), lambda b, h: (b, h, 0, 0)),
                  pl.BlockSpec((1, 1, S, MLA_VHEAD), lambda b, h: (b, h, 0, 0)),
                  pl.BlockSpec((1, MLA_VHEAD), lambda b, h: (0, 0))],
        out_specs=pl.BlockSpec((S, MLA_VHEAD), lambda b, h: (b, h)),
        out_shape=jax.ShapeDtypeStruct((B * S, MLA_HEADS * MLA_VHEAD), BF16),
        scratch_shapes=[pltpu.VMEM((2, tq, S), F32),
                        pltpu.VMEM((2, tq, S), BF16),
                        pltpu.VMEM((2, tq, 1), F32)],
        compiler_params=pltpu.CompilerParams(
            dimension_semantics=("arbitrary", "arbitrary"),
            vmem_limit_bytes=VMEM_LIMIT),
        name="attn",
    )(qa, ka, va, ng)


def _ffn_kernel(x_ref, xp_ref, xn_ref, hg_ref, hgp_ref, hgn_ref, at_ref, atp_ref, atn_ref,
                lng_ref, lnb_ref, wo_ref, g1_ref, b1_ref,
                wu_ref, cw_ref, cb_ref, wd_ref, g2_ref, b2_ref, out_ref,
                h1_s, il_s, ext_s, u_s, act_s, acc_s, *, tiles_per_seq):
    tm = FFN_TM
    hrows = 2 * SUBLANES
    grp = tm // SUBLANES
    n = D_FF // FFN_FC
    i = pl.program_id(0)
    has_prev = jnp.where(i % tiles_per_seq != 0, 1.0, 0.0)
    has_next = jnp.where(i % tiles_per_seq != tiles_per_seq - 1, 1.0, 0.0)

    def with_halo(main, before, after):
        sub = lax.broadcasted_iota(jnp.int32, (hrows, main.shape[1]), 0)
        halo = jnp.where(sub == 0, before.astype(F32), jnp.where(sub == 1, after.astype(F32), 0.0))
        return jnp.concatenate([main, halo.astype(main.dtype)], axis=0)

    def token_mix(xx, hgx, atx):
        mix = _dot(hgx, wo_ref[0:HG_W, :]) + _dot(atx, wo_ref[HG_W:, :])
        return _ln(DN_ALPHA * _ln(xx, lng_ref[...], lnb_ref[...]) + mix, g1_ref[...], b1_ref[...])

    h1x = token_mix(
        with_halo(x_ref[...], xp_ref[SUBLANES - 1:SUBLANES, :], xn_ref[0:1, :]),
        with_halo(hg_ref[...], hgp_ref[...].astype(F32)[hrows - 1:hrows, :],
                  hgn_ref[...].astype(F32)[0:1, :]),
        with_halo(at_ref[...], atp_ref[...].astype(F32)[hrows - 1:hrows, :],
                  atn_ref[...].astype(F32)[0:1, :]))
    h1_s[...] = h1x[0:tm, :]
    sub = lax.broadcasted_iota(jnp.int32, (hrows, D_MODEL), 0)
    valid = jnp.where(sub == 0, has_prev, jnp.where(sub == 1, has_next, 0.0))
    ext_s[tm:, :] = (h1x[tm:, :] * valid).astype(BF16)

    ngrp = D_MODEL // LANES
    for c in range(ngrp):
        for a in range(SUBLANES):
            il_s[c, pl.ds(a, grp, stride=SUBLANES), :] = h1_s[a * grp:(a + 1) * grp,
                                                              c * LANES:(c + 1) * LANES]
        ext_s[0:tm, c * LANES:(c + 1) * LANES] = il_s[c].astype(BF16)
    acc_s[...] = jnp.zeros_like(acc_s)

    half = tm // 2

    def chunk_cols(j, part):
        start = part * D_FF + j * FFN_FC
        return pl.ds(start if isinstance(start, int) else pl.multiple_of(start, FFN_FC), FFN_FC)

    def up(j, slot, part):
        u_s[slot, :, part * FFN_FC:(part + 1) * FFN_FC] = _dot(ext_s[...],
                                                               wu_ref[:, chunk_cols(j, part)])

    def glu(j, slot, part):
        r0, r1 = part * half, (part + 1) * half
        cw = jnp.concatenate([cw_ref[:, chunk_cols(j, 0)], cw_ref[:, chunk_cols(j, 1)]], axis=1)
        cb = jnp.concatenate([cb_ref[:, chunk_cols(j, 0)], cb_ref[:, chunk_cols(j, 1)]], axis=1)
        sub8 = lax.broadcasted_iota(jnp.int32, (SUBLANES, 2 * FFN_FC), 0)
        if part == 0:
            first = jnp.where(sub8 == 0, u_s[slot, tm:tm + 1, :],
                              pltpu.roll(u_s[slot, tm - SUBLANES:tm, :], 1, 0))
            prev = jnp.concatenate([first, u_s[slot, 0:r1 - SUBLANES, :]], axis=0)
            nxt = u_s[slot, SUBLANES:r1 + SUBLANES, :]
        else:
            last = jnp.where(sub8 == SUBLANES - 1, u_s[slot, tm + 1:tm + 2, :],
                             pltpu.roll(u_s[slot, 0:SUBLANES, :], SUBLANES - 1, 0))
            prev = u_s[slot, r0 - SUBLANES:r1 - SUBLANES, :]
            nxt = jnp.concatenate([u_s[slot, r0 + SUBLANES:tm, :], last], axis=0)
        c = prev * cw[0:1, :] + u_s[slot, r0:r1, :] * cw[1:2, :] + nxt * cw[2:3, :] + cb
        act_s[slot, r0:r1, :] = (jax.nn.gelu(c[:, :FFN_FC]) * c[:, FFN_FC:]).astype(BF16)

    def down(j, slot, part):
        c0 = part * 2 * LANES
        k0 = j * FFN_FC
        rows = pl.ds(k0 if isinstance(k0, int) else pl.multiple_of(k0, FFN_FC), FFN_FC)
        res = _dot(act_s[slot], wd_ref[rows, c0:c0 + 2 * LANES])
        acc_s[2 * part] += res[:, :LANES]
        acc_s[2 * part + 1] += res[:, LANES:]

    def stage(t, par):
        static = isinstance(t, int)
        do_up = not static or t + 1 < n
        do_glu = not static or 0 <= t < n
        do_down = not static or 1 <= t <= n
        for part in range(2):
            if do_up:
                up(t + 1, 1 - par, part)
            if do_glu:
                glu(t, par, part)
            if do_down:
                down(t - 1, 1 - par, 2 * part)
                down(t - 1, 1 - par, 2 * part + 1)

    stage(-1, 1)
    stage(0, 0)
    npairs = (n - 2) // 2

    def pair(jj, carry):
        t = 2 * jj + 1
        stage(t, 1)
        stage(t + 1, 0)
        return carry

    lax.fori_loop(0, npairs, pair, 0)
    for t in range(2 * npairs + 1, n + 1):
        stage(t, t % 2)

    for a in range(SUBLANES):
        ffn = jnp.concatenate([acc_s[c, pl.ds(a, grp, stride=SUBLANES), :] for c in range(ngrp)],
                              axis=1)
        rows = slice(a * grp, (a + 1) * grp)
        out_ref[rows, :] = _ln(DN_ALPHA * h1_s[rows, :] + ffn, g2_ref[...], b2_ref[...])


def _ffn(x2, hg, at, ln_g, ln_b, wo, g1, b1, wu, cw, cb, wd, g2, b2, S):
    N = x2.shape[0]
    tm = FFN_TM
    hrows = 2 * SUBLANES
    row = lambda i: (i, 0)
    full = lambda a: pl.BlockSpec(a.shape, lambda i: (0,) * a.ndim)

    def halo_specs(rows, width):
        per_tile = tm // rows
        last = N // rows - 1
        return [pl.BlockSpec((rows, width), lambda i: (jnp.maximum(i * per_tile - 1, 0), 0)),
                pl.BlockSpec((rows, width), lambda i: (jnp.minimum((i + 1) * per_tile, last), 0))]

    return pl.pallas_call(
        functools.partial(_ffn_kernel, tiles_per_seq=S // tm),
        grid=(N // tm,),
        in_specs=([pl.BlockSpec((tm, D_MODEL), row)] + halo_specs(SUBLANES, D_MODEL)
                  + [pl.BlockSpec((tm, HG_W), row)] + halo_specs(hrows, HG_W)
                  + [pl.BlockSpec((tm, HG_W), row)] + halo_specs(hrows, HG_W)
                  + [full(a) for a in (ln_g, ln_b, wo, g1, b1, wu, cw, cb, wd, g2, b2)]),
        out_specs=pl.BlockSpec((tm, D_MODEL), row),
        out_shape=jax.ShapeDtypeStruct((N, D_MODEL), F32),
        scratch_shapes=[pltpu.VMEM((tm, D_MODEL), F32),
                        pltpu.VMEM((D_MODEL // LANES, tm, LANES), F32),
                        pltpu.VMEM((tm + hrows, D_MODEL), BF16),
                        pltpu.VMEM((2, tm + hrows, 2 * FFN_FC), F32),
                        pltpu.VMEM((2, tm, FFN_FC), BF16),
                        pltpu.VMEM((D_MODEL // LANES, tm, LANES), F32)],
        compiler_params=pltpu.CompilerParams(dimension_semantics=("arbitrary",),
                                             vmem_limit_bytes=VMEM_LIMIT),
        name="ffn",
    )(x2, x2, x2, hg, hg, hg, at, at, at, ln_g, ln_b, wo, g1, b1, wu, cw, cb, wd, g2, b2)


def _swap_halves(w):
    half = MLA_ROPE // 2
    return jnp.concatenate([w[..., half:], w[..., :half]], axis=-1)


def kernel(x, positions, ln_in_g, ln_in_b, w_in, lb_fwd, lb_bwd, hg_norm_g, q_a_norm_g, w_q_b,
           kv_a_norm_g, w_kv_b, attn_norm_g, w_out, ln1_g, ln1_b, w_up, conv_w, conv_b, w_down,
           ln2_g, ln2_b):
    B, S, D = x.shape
    assert D == D_MODEL and w_in.shape[0] == 1 and lb_fwd.shape == (2, HG_W)
    assert S % HG_CHUNK == 0 and S % ATT_TQ == 0 and S % FFN_TM == 0 and D_FF % FFN_FC == 0
    N = B * S
    r1 = lambda a: a.reshape(1, -1)

    w_in0 = w_in[0]
    c_lat = 5 * HG_W
    c_kr = c_lat + MLA_Q_RANK + MLA_KV_RANK
    w_hg = w_in0[:, :c_lat].astype(BF16)
    w_kr = w_in0[:, c_kr:]
    w_lat = jnp.concatenate([w_in0[:, c_lat:c_kr], w_kr, _swap_halves(w_kr)], axis=1).astype(BF16)
    wq = w_q_b[0].reshape(MLA_Q_RANK, MLA_HEADS, MLA_NOPE + MLA_ROPE)
    wq_pe = wq[:, :, MLA_NOPE:]
    wqb = jnp.concatenate(
        [wq[:, :, :MLA_NOPE].reshape(MLA_Q_RANK, -1),
         jnp.concatenate([wq_pe, _swap_halves(wq_pe)], axis=-1).reshape(MLA_Q_RANK, -1)],
        axis=1).astype(BF16)
    wkvb = w_kv_b[0].astype(BF16)
    half = MLA_ROPE // 2
    inv_freq = 1.0 / (ROPE_THETA ** (jnp.arange(half, dtype=F32) / half))
    zeros = jnp.zeros((LANES - MLA_ROPE,), F32)
    invf = r1(jnp.tile(inv_freq, LANES // half))
    sgn = r1(jnp.concatenate([-jnp.ones((half,), F32), jnp.ones((half,), F32), zeros]))
    wu = w_up[0].astype(BF16)
    cw = conv_w[0]
    cb = r1(conv_b[0])
    wd = w_down[0].astype(BF16)

    x2 = x.reshape(N, D)
    pos2 = positions.reshape(N // LANES, LANES)
    q, ff, fb, gate, v, qa, ka, va = _proj(
        x2, pos2, r1(ln_in_g), r1(ln_in_b), w_hg, w_lat, lb_fwd, lb_bwd,
        r1(q_a_norm_g[0]), wqb, r1(kv_a_norm_g[0]), wkvb, invf, sgn, B, S)
    hg = _hgrn(q, ff, fb, v, gate, r1(hg_norm_g[0]), B, S)
    at = _attn(qa, ka, va, r1(attn_norm_g[0]), B, S)
    out = _ffn(x2, hg, at, r1(ln_in_g), r1(ln_in_b), w_out[0].astype(BF16), r1(ln1_g[0]), r1(ln1_b[0]),
               wu, cw, cb, wd, r1(ln2_g[0]), r1(ln2_b[0]), S)
    return out.reshape(B, S, D)
```

```python
import functools

import jax
import jax.numpy as jnp
from jax import lax
from jax.experimental import pallas as pl
from jax.experimental.pallas import tpu as pltpu

D_MODEL = 1024
HG_HEADS = 4
HG_DIM = 128
HG_W = HG_HEADS * HG_DIM
MLA_HEADS = 4
MLA_Q_RANK = 384
MLA_KV_RANK = 128
MLA_NOPE = 128
MLA_ROPE = 64
MLA_VHEAD = 128
MLA_QK_PAD = 256
ROPE_THETA = 10000.0
D_FF = 2816
DN_ALPHA = 2.0 ** 0.25
NORM_EPS = 1e-5
LOG2E = 1.4426950408889634

LANES = 128
SUBLANES = 8
VMEM_LIMIT = 56 * 1024 * 1024

PROJ_TM = 1024
HG_CHUNK = 256
HG_BLOCKS = HG_CHUNK // SUBLANES
HG_UNROLL = 8
ATT_TQ = 256
ATT_SUB = 32
FFN_TM = 512
FFN_FC = 256
FFN_UP_COLS = 512

F32 = jnp.float32
BF16 = jnp.bfloat16


def _ln(x, g, b):
    mu = jnp.mean(x, axis=-1, keepdims=True)
    xc = x - mu
    var = jnp.mean(xc * xc, axis=-1, keepdims=True)
    return xc * lax.rsqrt(var + NORM_EPS) * g + b


def _rms(x, g):
    return x * lax.rsqrt(jnp.mean(x * x, axis=-1, keepdims=True) + NORM_EPS) * g


def _dot(a, b):
    return jnp.dot(a, b, preferred_element_type=F32)


def _dot_nt(a, b):
    return lax.dot_general(a, b, (((1,), (1,)), ((), ())), preferred_element_type=F32)


def _proj_kernel(x_ref, pos_ref, lng_ref, lnb_ref, w_hg_ref, w_lat_ref, lbf_ref, lbb_ref,
                 qan_ref, wqb_ref, kvn_ref, wkvb_ref, invf_ref, sgn_ref,
                 q_ref, ff_ref, fb_ref, gate_ref, v_ref, qa_ref, ka_ref, va_ref):
    h = _ln(x_ref[...], lng_ref[...], lnb_ref[...]).astype(BF16)

    def lower_bound(ref):
        r0, r1 = ref[0:1, :], ref[1:2, :]
        m = jnp.maximum(r0, r1)
        e0, e1 = jnp.exp(r0 - m), jnp.exp(r1 - m)
        return e0 / (e0 + e1)

    def hg_cols(j):
        return _dot(h, w_hg_ref[:, j * HG_W:(j + 1) * HG_W])

    q_ref[...] = jax.nn.silu(hg_cols(0))
    v_ref[...] = hg_cols(1).astype(BF16)
    lbf = lower_bound(lbf_ref)
    ff_ref[...] = lbf + (1.0 - lbf) * jax.nn.sigmoid(hg_cols(2))
    lbb = lower_bound(lbb_ref)
    fb_ref[...] = lbb + (1.0 - lbb) * jax.nn.sigmoid(hg_cols(3))
    gate_ref[...] = hg_cols(4)

    hr = x_ref.shape[0] // 2
    pos_t = pos_ref[...].astype(F32).T
    pos = jnp.concatenate([pos_t[:, j:j + 1] for j in range(pos_t.shape[1])], axis=0)
    lane = lax.broadcasted_iota(jnp.int32, (hr, LANES), 1)
    ang = jnp.where(lane < MLA_ROPE, pos[0:hr, :], pos[hr:, :]) * invf_ref[...]
    cs, sn = jnp.cos(ang), jnp.sin(ang)
    sgn = sgn_ref[...]
    keep = jnp.abs(sgn)
    cos_t = jnp.concatenate([cs * keep, pltpu.roll(cs, MLA_ROPE, 1) * keep], axis=0)
    sin_t = jnp.concatenate([sn * sgn, pltpu.roll(sn, MLA_ROPE, 1) * sgn], axis=0)

    def rope(grp):
        return grp * cos_t + pltpu.roll(grp, MLA_ROPE, 1) * sin_t

    lat = _dot(h, w_lat_ref[...])
    qa = _rms(lat[:, :MLA_Q_RANK], qan_ref[...]).astype(BF16)
    kva = _rms(lat[:, MLA_Q_RANK:MLA_Q_RANK + MLA_KV_RANK], kvn_ref[...]).astype(BF16)
    k_pe = rope(lat[:, MLA_Q_RANK + MLA_KV_RANK:]).astype(BF16)

    scale = (MLA_NOPE + MLA_ROPE) ** -0.5 * LOG2E
    qh = _dot(qa, wqb_ref[...])
    kvh = _dot(kva, wkvb_ref[...])
    for hd in range(MLA_HEADS):
        qa_ref[0, hd, :, 0:LANES] = (qh[:, hd * LANES:(hd + 1) * LANES] * scale).astype(BF16)
        pe = rope(qh[:, (MLA_HEADS + hd) * LANES:(MLA_HEADS + hd + 1) * LANES])
        qa_ref[0, hd, :, LANES:2 * LANES] = (pe * scale).astype(BF16)
        ka_ref[0, hd, :, 0:LANES] = kvh[:, 2 * hd * LANES:(2 * hd + 1) * LANES].astype(BF16)
        ka_ref[0, hd, :, LANES:2 * LANES] = k_pe
        va_ref[0, hd, :, :] = kvh[:, (2 * hd + 1) * LANES:(2 * hd + 2) * LANES].astype(BF16)


def _proj(x2, pos2, ln_g, ln_b, w_hg, w_lat, lb_f, lb_b, qan, wqb, kvn, wkvb, invf, sgn, B, S):
    N = B * S
    tm = PROJ_TM
    spb = S // tm
    row = lambda i: (i, 0)
    const = lambda i: (0, 0)
    head = lambda i: (i // spb, 0, i % spb, 0)
    full = lambda a: pl.BlockSpec(a.shape, const)
    out_shape = (
        jax.ShapeDtypeStruct((N, HG_W), F32),
        jax.ShapeDtypeStruct((N, HG_W), F32),
        jax.ShapeDtypeStruct((N, HG_W), F32),
        jax.ShapeDtypeStruct((N, HG_W), F32),
        jax.ShapeDtypeStruct((N, HG_W), BF16),
        jax.ShapeDtypeStruct((B, MLA_HEADS, S, MLA_QK_PAD), BF16),
        jax.ShapeDtypeStruct((B, MLA_HEADS, S, MLA_QK_PAD), BF16),
        jax.ShapeDtypeStruct((B, MLA_HEADS, S, MLA_VHEAD), BF16),
    )
    tok = pl.BlockSpec((tm, HG_W), row)
    return pl.pallas_call(
        _proj_kernel,
        grid=(N // tm,),
        in_specs=[pl.BlockSpec((tm, D_MODEL), row), pl.BlockSpec((tm // LANES, LANES), row),
                  full(ln_g), full(ln_b), full(w_hg), full(w_lat), full(lb_f), full(lb_b),
                  full(qan), full(wqb), full(kvn), full(wkvb), full(invf), full(sgn)],
        out_specs=(tok, tok, tok, tok, tok,
                   pl.BlockSpec((1, MLA_HEADS, tm, MLA_QK_PAD), head),
                   pl.BlockSpec((1, MLA_HEADS, tm, MLA_QK_PAD), head),
                   pl.BlockSpec((1, MLA_HEADS, tm, MLA_VHEAD), head)),
        out_shape=out_shape,
        compiler_params=pltpu.CompilerParams(dimension_semantics=("arbitrary",),
                                             vmem_limit_bytes=VMEM_LIMIT),
        name="proj",
    )(x2, pos2, ln_g, ln_b, w_hg, w_lat, lb_f, lb_b, qan, wqb, kvn, wkvb, invf, sgn)


def _blocks(x):
    return [x[i * SUBLANES:(i + 1) * SUBLANES] for i in range(x.shape[0] // SUBLANES)]


def _rows(blocks):
    return jnp.concatenate(blocks, axis=0)


def _hgrn_chunk(q, f, v, lmap_ref, rev):
    C = HG_CHUNK
    g = jnp.log(f) * LOG2E
    k = 1.0 - f
    row = lax.broadcasted_iota(jnp.int32, (C, HG_DIM), 0)

    def roll8(x, shift):
        return _rows([pltpu.roll(blk, shift, 0) for blk in _blocks(x)])

    P, T = g, g
    A = None
    for lvl in range(3):
        b = 1 << lvl
        bit = ((row >> lvl) & 1) == 1
        later = jnp.logical_not(bit) if rev else bit
        w = jnp.exp2(jnp.where(later, P, T - P))
        a_l = _dot_nt((q * w).astype(BF16), (k * w).astype(BF16))
        hit = lmap_ref[...] == lvl
        A = jnp.where(hit, a_l, 0.0) if A is None else jnp.where(hit, a_l, A)
        sib = roll8(T, b) if 2 * b == SUBLANES else jnp.where(bit, roll8(T, b), roll8(T, SUBLANES - b))
        P = P + jnp.where(later, sib, 0.0)
        T = T + sib

    Pb, Tb, qb, kb, Ab = _blocks(P), _blocks(T), _blocks(q), _blocks(k), _blocks(A)
    zero = jnp.zeros((SUBLANES, HG_DIM), F32)
    lvl, m = 3, 1
    while m < HG_BLOCKS:
        later = [(((i // m) & 1) == 1) != rev for i in range(HG_BLOCKS)]
        wb = [jnp.exp2(Pb[i] if later[i] else Tb[i] - Pb[i]) for i in range(HG_BLOCKS)]
        qt = _rows([qb[i] * wb[i] for i in range(HG_BLOCKS) if later[i]]).astype(BF16)
        kt = _rows([zero if later[i] else kb[i] * wb[i] for i in range(HG_BLOCKS)]).astype(BF16)
        a_l = _blocks(_dot_nt(qt, kt))
        n_later = 0
        for i in range(HG_BLOCKS):
            if later[i]:
                hit = lmap_ref[i * SUBLANES:(i + 1) * SUBLANES, :] == lvl
                Ab[i] = jnp.where(hit, a_l[n_later], Ab[i])
                n_later += 1
        sib = [Tb[i ^ m] for i in range(HG_BLOCKS)]
        Pb = [Pb[i] + sib[i] if later[i] else Pb[i] for i in range(HG_BLOCKS)]
        tot = {}
        for i in range(HG_BLOCKS):
            key = i // (2 * m)
            if key not in tot:
                tot[key] = Tb[i] + sib[i]
        Tb = [tot[i // (2 * m)] for i in range(HG_BLOCKS)]
        lvl, m = lvl + 1, 2 * m

    vf = v.astype(F32)
    o = _dot(_rows(Ab).astype(BF16), v) + jnp.sum(q * k, axis=1, keepdims=True) * vf
    qt = _rows([qb[i] * jnp.exp2(Pb[i]) for i in range(HG_BLOCKS)]).astype(BF16)
    kt = _rows([kb[i] * jnp.exp2(Tb[i] - Pb[i]) for i in range(HG_BLOCKS)]).astype(BF16)
    return o, qt, kt, jnp.exp2(Tb[0])


def _hgrn_kernel(q_ref, ff_ref, fb_ref, v_ref, gate_ref, ng_ref, out_ref, o_acc, lmap_s):
    S = q_ref.shape[0]
    C = HG_CHUNK
    n = S // C

    ri = lax.broadcasted_iota(jnp.int32, (C, C), 0)
    ci = lax.broadcasted_iota(jnp.int32, (C, C), 1)
    x = ri ^ ci
    level = jnp.full((C, C), -1, jnp.int32)
    b = 1
    while b < C:
        level = level + jnp.where(x >= b, 1, 0)
        b *= 2
    lmap_s[0] = jnp.where(ci < ri, level, -1)
    lmap_s[1] = jnp.where(ci > ri, level, -1)

    def visit(c, f_ref, st, rev, first):
        r = pl.multiple_of(c * C, C)
        q = q_ref[pl.ds(r, C), :]
        v = v_ref[pl.ds(r, C), :]
        o, qt, kt, d = _hgrn_chunk(q, f_ref[pl.ds(r, C), :], v, lmap_s.at[1 if rev else 0], rev)
        o = o + _dot_nt(qt, st.astype(BF16))
        st = st * d[0:1, :] + _dot(v.astype(F32).T.astype(BF16), kt)
        if first:
            o_acc[pl.ds(r, C), :] = o
        else:
            o = _rms(o + o_acc[pl.ds(r, C), :], ng_ref[...])
            out_ref[pl.ds(r, C), :] = (o * jax.nn.silu(gate_ref[pl.ds(r, C), :])).astype(BF16)
        return st

    def step(first):
        def body(i, carry):
            st_f, st_b = carry
            return (visit(i, ff_ref, st_f, False, first),
                    visit(n - 1 - i, fb_ref, st_b, True, first))
        return body

    z = jnp.zeros((HG_DIM, HG_DIM), F32)
    carry = lax.fori_loop(0, n // 2, step(True), (z, z), unroll=HG_UNROLL)
    lax.fori_loop(n // 2, n, step(False), carry, unroll=HG_UNROLL)


def _hgrn(q, ff, fb, v, gate, ng, B, S):
    N = B * S
    C = HG_CHUNK
    assert (S // C) % 2 == 0
    blk = pl.BlockSpec((S, HG_DIM), lambda b, h: (b, h))
    return pl.pallas_call(
        _hgrn_kernel,
        grid=(B, HG_HEADS),
        in_specs=[blk, blk, blk, blk, blk, pl.BlockSpec((1, HG_DIM), lambda b, h: (0, 0))],
        out_specs=blk,
        out_shape=jax.ShapeDtypeStruct((N, HG_W), BF16),
        scratch_shapes=[pltpu.VMEM((S, HG_DIM), F32),
                        pltpu.VMEM((2, C, C), jnp.int32)],
        compiler_params=pltpu.CompilerParams(dimension_semantics=("arbitrary", "arbitrary"),
                                             vmem_limit_bytes=VMEM_LIMIT),
        name="hgrn",
    )(q, ff, fb, v, gate, ng)


def _attn_kernel(q_ref, k_ref, v_ref, ng_ref, out_ref, s_s, p_s, l_s):
    S = k_ref.shape[2]
    tq = ATT_TQ
    nq = S // tq

    def scores(i, slot):
        s_s[slot] = _dot_nt(q_ref[0, 0, i * tq:(i + 1) * tq, :], k_ref[0, 0])

    def softmax(slot):
        for r in range(0, tq, ATT_SUB):
            s = s_s[slot, r:r + ATT_SUB, :]
            p = jnp.exp2(s - jnp.max(s, axis=-1, keepdims=True))
            l_s[slot, r:r + ATT_SUB, :] = jnp.sum(p, axis=-1, keepdims=True)
            p_s[slot, r:r + ATT_SUB, :] = p.astype(BF16)

    def values(i, slot):
        o = _dot(p_s[slot], v_ref[0, 0]) / l_s[slot]
        out_ref[i * tq:(i + 1) * tq, :] = _rms(o, ng_ref[...]).astype(BF16)

    scores(0, 0)
    for i in range(nq):
        if i + 1 < nq:
            scores(i + 1, (i + 1) % 2)
        softmax(i % 2)
        values(i, i % 2)


def _attn(qa, ka, va, ng, B, S):
    tq = ATT_TQ
    return pl.pallas_call(
        _attn_kernel,
        grid=(B, MLA_HEADS),
        in_specs=[pl.BlockSpec((1, 1, S, MLA_QK_PAD), lambda b, h: (b, h, 0, 0)),
                  pl.BlockSpec((1, 1, S, MLA_QK_PAD), lambda b, h: (b, h, 0, 0)),
                  pl.BlockSpec((1, 1, S, MLA_VHEAD), lambda b, h: (b, h, 0, 0)),
                  pl.BlockSpec((1, MLA_VHEAD), lambda b, h: (0, 0))],
        out_specs=pl.BlockSpec((S, MLA_VHEAD), lambda b, h: (b, h)),
        out_shape=jax.ShapeDtypeStruct((B * S, MLA_HEADS * MLA_VHEAD), BF16),
        scratch_shapes=[pltpu.VMEM((2, tq, S), F32),
                        pltpu.VMEM((2, tq, S), BF16),
                        pltpu.VMEM((2, tq, 1), F32)],
        compiler_params=pltpu.CompilerParams(
            dimension_semantics=("arbitrary", "arbitrary"),
            vmem_limit_bytes=VMEM_LIMIT),
        name="attn",
    )(qa, ka, va, ng)


def _ffn_kernel(x_ref, xp_ref, xn_ref, hg_ref, hgp_ref, hgn_ref, at_ref, atp_ref, atn_ref,
                lng_ref, lnb_ref, wo_ref, g1_ref, b1_ref,
                wu_ref, cw_ref, cb_ref, wd_ref, g2_ref, b2_ref, out_ref,
                ext_s, u_s, act_s, acc_s, *, tiles_per_seq):
    tm = FFN_TM
    hrows = 2 * SUBLANES
    grp = tm // SUBLANES
    n = D_FF // FFN_FC
    i = pl.program_id(0)
    has_prev = jnp.where(i % tiles_per_seq != 0, 1.0, 0.0)
    has_next = jnp.where(i % tiles_per_seq != tiles_per_seq - 1, 1.0, 0.0)

    def with_halo(main, before, after):
        sub = lax.broadcasted_iota(jnp.int32, (hrows, main.shape[1]), 0)
        halo = jnp.where(sub == 0, before.astype(F32), jnp.where(sub == 1, after.astype(F32), 0.0))
        return jnp.concatenate([main, halo.astype(main.dtype)], axis=0)

    def token_mix(xx, hgx, atx):
        mix = _dot(hgx, wo_ref[0:HG_W, :]) + _dot(atx, wo_ref[HG_W:, :])
        return _ln(DN_ALPHA * _ln(xx, lng_ref[...], lnb_ref[...]) + mix, g1_ref[...], b1_ref[...])

    h1x = token_mix(
        with_halo(x_ref[...], xp_ref[SUBLANES - 1:SUBLANES, :], xn_ref[0:1, :]),
        with_halo(hg_ref[...], hgp_ref[...].astype(F32)[hrows - 1:hrows, :],
                  hgn_ref[...].astype(F32)[0:1, :]),
        with_halo(at_ref[...], atp_ref[...].astype(F32)[hrows - 1:hrows, :],
                  atn_ref[...].astype(F32)[0:1, :]))
    out_ref[...] = h1x[0:tm, :]
    sub = lax.broadcasted_iota(jnp.int32, (hrows, D_MODEL), 0)
    valid = jnp.where(sub == 0, has_prev, jnp.where(sub == 1, has_next, 0.0))
    ext_s[tm:, :] = (h1x[tm:, :] * valid).astype(BF16)

    ngrp = D_MODEL // LANES
    for c in range(ngrp):
        for a in range(SUBLANES):
            acc_s[c, pl.ds(a, grp, stride=SUBLANES), :] = out_ref[a * grp:(a + 1) * grp,
                                                                  c * LANES:(c + 1) * LANES]
        ext_s[0:tm, c * LANES:(c + 1) * LANES] = acc_s[c].astype(BF16)

    for c0 in range(0, 2 * D_FF, FFN_UP_COLS):
        u_s[:, c0:c0 + FFN_UP_COLS] = _dot(ext_s[...], wu_ref[:, c0:c0 + FFN_UP_COLS])

    half = tm // 2

    def conv(col, r0, r1):
        cw = cw_ref[:, col]
        sub8 = lax.broadcasted_iota(jnp.int32, (SUBLANES, FFN_FC), 0)
        if r0 == 0:
            first = jnp.where(sub8 == 0, u_s[tm:tm + 1, col],
                              pltpu.roll(u_s[tm - SUBLANES:tm, col], 1, 0))
            prev = jnp.concatenate([first, u_s[0:r1 - SUBLANES, col]], axis=0)
        else:
            prev = u_s[r0 - SUBLANES:r1 - SUBLANES, col]
        if r1 == tm:
            last = jnp.where(sub8 == SUBLANES - 1, u_s[tm + 1:tm + 2, col],
                             pltpu.roll(u_s[0:SUBLANES, col], SUBLANES - 1, 0))
            nxt = jnp.concatenate([u_s[r0 + SUBLANES:tm, col], last], axis=0)
        else:
            nxt = u_s[r0 + SUBLANES:r1 + SUBLANES, col]
        return prev * cw[0:1, :] + u_s[r0:r1, col] * cw[1:2, :] + nxt * cw[2:3, :] + cb_ref[:, col]

    def glu(j, carry):
        ca = pl.ds(pl.multiple_of(j * FFN_FC, FFN_FC), FFN_FC)
        cb = pl.ds(pl.multiple_of(D_FF + j * FFN_FC, FFN_FC), FFN_FC)
        for r0 in (0, half):
            act = jax.nn.gelu(conv(ca, r0, r0 + half)) * conv(cb, r0, r0 + half)
            act_s[r0:r0 + half, ca] = act.astype(BF16)
        return carry

    lax.fori_loop(0, n, glu, 0)

    ffn_il = _dot(act_s[...], wd_ref[...])
    for c in range(ngrp):
        acc_s[c] = ffn_il[:, c * LANES:(c + 1) * LANES]

    for a in range(SUBLANES):
        ffn = jnp.concatenate([acc_s[c, pl.ds(a, grp, stride=SUBLANES), :] for c in range(ngrp)],
                              axis=1)
        rows = slice(a * grp, (a + 1) * grp)
        out_ref[rows, :] = _ln(DN_ALPHA * out_ref[rows, :] + ffn, g2_ref[...], b2_ref[...])


def _ffn(x2, hg, at, ln_g, ln_b, wo, g1, b1, wu, cw, cb, wd, g2, b2, S):
    N = x2.shape[0]
    tm = FFN_TM
    hrows = 2 * SUBLANES
    row = lambda i: (i, 0)
    full = lambda a: pl.BlockSpec(a.shape, lambda i: (0,) * a.ndim)

    def halo_specs(rows, width):
        per_tile = tm // rows
        last = N // rows - 1
        return [pl.BlockSpec((rows, width), lambda i: (jnp.maximum(i * per_tile - 1, 0), 0)),
                pl.BlockSpec((rows, width), lambda i: (jnp.minimum((i + 1) * per_tile, last), 0))]

    return pl.pallas_call(
        functools.partial(_ffn_kernel, tiles_per_seq=S // tm),
        grid=(N // tm,),
        in_specs=([pl.BlockSpec((tm, D_MODEL), row)] + halo_specs(SUBLANES, D_MODEL)
                  + [pl.BlockSpec((tm, HG_W), row)] + halo_specs(hrows, HG_W)
                  + [pl.BlockSpec((tm, HG_W), row)] + halo_specs(hrows, HG_W)
                  + [full(a) for a in (ln_g, ln_b, wo, g1, b1, wu, cw, cb, wd, g2, b2)]),
        out_specs=pl.BlockSpec((tm, D_MODEL), row),
        out_shape=jax.ShapeDtypeStruct((N, D_MODEL), F32),
        scratch_shapes=[pltpu.VMEM((tm + hrows, D_MODEL), BF16),
                        pltpu.VMEM((tm + hrows, 2 * D_FF), F32),
                        pltpu.VMEM((tm, D_FF), BF16),
                        pltpu.VMEM((D_MODEL // LANES, tm, LANES), F32)],
        compiler_params=pltpu.CompilerParams(dimension_semantics=("arbitrary",),
                                             vmem_limit_bytes=VMEM_LIMIT),
        name="ffn",
    )(x2, x2, x2, hg, hg, hg, at, at, at, ln_g, ln_b, wo, g1, b1, wu, cw, cb, wd, g2, b2)


def _swap_halves(w):
    half = MLA_ROPE // 2
    return jnp.concatenate([w[..., half:], w[..., :half]], axis=-1)


def kernel(x, positions, ln_in_g, ln_in_b, w_in, lb_fwd, lb_bwd, hg_norm_g, q_a_norm_g, w_q_b,
           kv_a_norm_g, w_kv_b, attn_norm_g, w_out, ln1_g, ln1_b, w_up, conv_w, conv_b, w_down,
           ln2_g, ln2_b):
    B, S, D = x.shape
    assert D == D_MODEL and w_in.shape[0] == 1 and lb_fwd.shape == (2, HG_W)
    assert S % HG_CHUNK == 0 and S % ATT_TQ == 0 and S % FFN_TM == 0 and D_FF % FFN_FC == 0
    N = B * S
    r1 = lambda a: a.reshape(1, -1)

    w_in0 = w_in[0]
    c_lat = 5 * HG_W
    c_kr = c_lat + MLA_Q_RANK + MLA_KV_RANK
    w_hg = w_in0[:, :c_lat].astype(BF16)
    w_kr = w_in0[:, c_kr:]
    w_lat = jnp.concatenate([w_in0[:, c_lat:c_kr], w_kr, _swap_halves(w_kr)], axis=1).astype(BF16)
    wq = w_q_b[0].reshape(MLA_Q_RANK, MLA_HEADS, MLA_NOPE + MLA_ROPE)
    wq_pe = wq[:, :, MLA_NOPE:]
    wqb = jnp.concatenate(
        [wq[:, :, :MLA_NOPE].reshape(MLA_Q_RANK, -1),
         jnp.concatenate([wq_pe, _swap_halves(wq_pe)], axis=-1).reshape(MLA_Q_RANK, -1)],
        axis=1).astype(BF16)
    wkvb = w_kv_b[0].astype(BF16)
    half = MLA_ROPE // 2
    inv_freq = 1.0 / (ROPE_THETA ** (jnp.arange(half, dtype=F32) / half))
    zeros = jnp.zeros((LANES - MLA_ROPE,), F32)
    invf = r1(jnp.tile(inv_freq, LANES // half))
    sgn = r1(jnp.concatenate([-jnp.ones((half,), F32), jnp.ones((half,), F32), zeros]))
    wu = w_up[0].astype(BF16)
    cw = conv_w[0]
    cb = r1(conv_b[0])
    wd = w_down[0].astype(BF16)

    x2 = x.reshape(N, D)
    pos2 = positions.reshape(N // LANES, LANES)
    q, ff, fb, gate, v, qa, ka, va = _proj(
        x2, pos2, r1(ln_in_g), r1(ln_in_b), w_hg, w_lat, lb_fwd, lb_bwd,
        r1(q_a_norm_g[0]), wqb, r1(kv_a_norm_g[0]), wkvb, invf, sgn, B, S)
    hg = _hgrn(q, ff, fb, v, gate, r1(hg_norm_g[0]), B, S)
    at = _attn(qa, ka, va, r1(attn_norm_g[0]), B, S)
    out = _ffn(x2, hg, at, r1(ln_in_g), r1(ln_in_b), w_out[0].astype(BF16), r1(ln1_g[0]), r1(ln1_b[0]),
               wu, cw, cb, wd, r1(ln2_g[0]), r1(ln2_b[0]), S)
    return out.reshape(B, S, D)
```

```python
import functools

import jax
import jax.numpy as jnp
from jax import lax
from jax.experimental import pallas as pl
from jax.experimental.pallas import tpu as pltpu

D_MODEL = 1024
HG_HEADS = 4
HG_DIM = 128
HG_W = HG_HEADS * HG_DIM
MLA_HEADS = 4
MLA_Q_RANK = 384
MLA_KV_RANK = 128
MLA_NOPE = 128
MLA_ROPE = 64
MLA_VHEAD = 128
MLA_QK_PAD = 256
ROPE_THETA = 10000.0
D_FF = 2816
DN_ALPHA = 2.0 ** 0.25
NORM_EPS = 1e-5
LOG2E = 1.4426950408889634

LANES = 128
SUBLANES = 8
VMEM_LIMIT = 56 * 1024 * 1024

PROJ_TM = 1024
HG_CHUNK = 256
HG_BLOCKS = HG_CHUNK // SUBLANES
HG_UNROLL = 8
ATT_TQ = 256
ATT_SUB = 32
FFN_TM = 512
FFN_FC = 256
FFN_UP_COLS = 512

F32 = jnp.float32
BF16 = jnp.bfloat16


def _ln(x, g, b):
    mu = jnp.mean(x, axis=-1, keepdims=True)
    xc = x - mu
    var = jnp.mean(xc * xc, axis=-1, keepdims=True)
    return xc * lax.rsqrt(var + NORM_EPS) * g + b


def _rms(x, g):
    return x * lax.rsqrt(jnp.mean(x * x, axis=-1, keepdims=True) + NORM_EPS) * g


def _gelu_tanh(x):
    c1 = -2.0 * (2.0 / jnp.pi) ** 0.5 * LOG2E
    z = x * (c1 + (c1 * 0.044715) * (x * x))
    return x / (1.0 + jnp.exp2(z))


def _dot(a, b):
    return jnp.dot(a, b, preferred_element_type=F32)


def _dot_nt(a, b):
    return lax.dot_general(a, b, (((1,), (1,)), ((), ())), preferred_element_type=F32)


def _proj_kernel(x_ref, pos_ref, lng_ref, lnb_ref, w_hg_ref, w_lat_ref, lbf_ref, lbb_ref,
                 qan_ref, wqb_ref, kvn_ref, wkvb_ref, invf_ref, sgn_ref,
                 q_ref, ff_ref, fb_ref, gate_ref, v_ref, qa_ref, ka_ref, va_ref):
    h = _ln(x_ref[...], lng_ref[...], lnb_ref[...]).astype(BF16)

    def lower_bound(ref):
        r0, r1 = ref[0:1, :], ref[1:2, :]
        m = jnp.maximum(r0, r1)
        e0, e1 = jnp.exp(r0 - m), jnp.exp(r1 - m)
        return e0 / (e0 + e1)

    def hg_cols(j):
        return _dot(h, w_hg_ref[:, j * HG_W:(j + 1) * HG_W])

    q_ref[...] = jax.nn.silu(hg_cols(0))
    v_ref[...] = hg_cols(1).astype(BF16)
    lbf = lower_bound(lbf_ref)
    ff_ref[...] = lbf + (1.0 - lbf) * jax.nn.sigmoid(hg_cols(2))
    lbb = lower_bound(lbb_ref)
    fb_ref[...] = lbb + (1.0 - lbb) * jax.nn.sigmoid(hg_cols(3))
    gate_ref[...] = hg_cols(4)

    hr = x_ref.shape[0] // 2
    pos_t = pos_ref[...].astype(F32).T
    pos = jnp.concatenate([pos_t[:, j:j + 1] for j in range(pos_t.shape[1])], axis=0)
    lane = lax.broadcasted_iota(jnp.int32, (hr, LANES), 1)
    ang = jnp.where(lane < MLA_ROPE, pos[0:hr, :], pos[hr:, :]) * invf_ref[...]
    cs, sn = jnp.cos(ang), jnp.sin(ang)
    sgn = sgn_ref[...]
    keep = jnp.abs(sgn)
    cos_t = jnp.concatenate([cs * keep, pltpu.roll(cs, MLA_ROPE, 1) * keep], axis=0)
    sin_t = jnp.concatenate([sn * sgn, pltpu.roll(sn, MLA_ROPE, 1) * sgn], axis=0)

    def rope(grp):
        return grp * cos_t + pltpu.roll(grp, MLA_ROPE, 1) * sin_t

    lat = _dot(h, w_lat_ref[...])
    qa = _rms(lat[:, :MLA_Q_RANK], qan_ref[...]).astype(BF16)
    kva = _rms(lat[:, MLA_Q_RANK:MLA_Q_RANK + MLA_KV_RANK], kvn_ref[...]).astype(BF16)
    k_pe = rope(lat[:, MLA_Q_RANK + MLA_KV_RANK:]).astype(BF16)

    scale = (MLA_NOPE + MLA_ROPE) ** -0.5 * LOG2E
    qh = _dot(qa, wqb_ref[...])
    kvh = _dot(kva, wkvb_ref[...])
    for hd in range(MLA_HEADS):
        qa_ref[0, hd, :, 0:LANES] = (qh[:, hd * LANES:(hd + 1) * LANES] * scale).astype(BF16)
        pe = rope(qh[:, (MLA_HEADS + hd) * LANES:(MLA_HEADS + hd + 1) * LANES])
        qa_ref[0, hd, :, LANES:2 * LANES] = (pe * scale).astype(BF16)
        ka_ref[0, hd, :, 0:LANES] = kvh[:, 2 * hd * LANES:(2 * hd + 1) * LANES].astype(BF16)
        ka_ref[0, hd, :, LANES:2 * LANES] = k_pe
        va_ref[0, hd, :, :] = kvh[:, (2 * hd + 1) * LANES:(2 * hd + 2) * LANES].astype(BF16)


def _proj(x2, pos2, ln_g, ln_b, w_hg, w_lat, lb_f, lb_b, qan, wqb, kvn, wkvb, invf, sgn, B, S):
    N = B * S
    tm = PROJ_TM
    spb = S // tm
    row = lambda i: (i, 0)
    const = lambda i: (0, 0)
    head = lambda i: (i // spb, 0, i % spb, 0)
    full = lambda a: pl.BlockSpec(a.shape, const)
    out_shape = (
        jax.ShapeDtypeStruct((N, HG_W), F32),
        jax.ShapeDtypeStruct((N, HG_W), F32),
        jax.ShapeDtypeStruct((N, HG_W), F32),
        jax.ShapeDtypeStruct((N, HG_W), F32),
        jax.ShapeDtypeStruct((N, HG_W), BF16),
        jax.ShapeDtypeStruct((B, MLA_HEADS, S, MLA_QK_PAD), BF16),
        jax.ShapeDtypeStruct((B, MLA_HEADS, S, MLA_QK_PAD), BF16),
        jax.ShapeDtypeStruct((B, MLA_HEADS, S, MLA_VHEAD), BF16),
    )
    tok = pl.BlockSpec((tm, HG_W), row)
    return pl.pallas_call(
        _proj_kernel,
        grid=(N // tm,),
        in_specs=[pl.BlockSpec((tm, D_MODEL), row), pl.BlockSpec((tm // LANES, LANES), row),
                  full(ln_g), full(ln_b), full(w_hg), full(w_lat), full(lb_f), full(lb_b),
                  full(qan), full(wqb), full(kvn), full(wkvb), full(invf), full(sgn)],
        out_specs=(tok, tok, tok, tok, tok,
                   pl.BlockSpec((1, MLA_HEADS, tm, MLA_QK_PAD), head),
                   pl.BlockSpec((1, MLA_HEADS, tm, MLA_QK_PAD), head),
                   pl.BlockSpec((1, MLA_HEADS, tm, MLA_VHEAD), head)),
        out_shape=out_shape,
        compiler_params=pltpu.CompilerParams(dimension_semantics=("arbitrary",),
                                             vmem_limit_bytes=VMEM_LIMIT),
        name="proj",
    )(x2, pos2, ln_g, ln_b, w_hg, w_lat, lb_f, lb_b, qan, wqb, kvn, wkvb, invf, sgn)


def _blocks(x):
    return [x[i * SUBLANES:(i + 1) * SUBLANES] for i in range(x.shape[0] // SUBLANES)]


def _rows(blocks):
    return jnp.concatenate(blocks, axis=0)


def _hgrn_chunk(q, f, v, lmap_ref, rev):
    C = HG_CHUNK
    g = jnp.log(f) * LOG2E
    k = 1.0 - f
    row = lax.broadcasted_iota(jnp.int32, (C, HG_DIM), 0)

    def roll8(x, shift):
        return _rows([pltpu.roll(blk, shift, 0) for blk in _blocks(x)])

    P, T = g, g
    A = None
    for lvl in range(3):
        b = 1 << lvl
        bit = ((row >> lvl) & 1) == 1
        later = jnp.logical_not(bit) if rev else bit
        w = jnp.exp2(jnp.where(later, P, T - P))
        a_l = _dot_nt((q * w).astype(BF16), (k * w).astype(BF16))
        hit = lmap_ref[...] == lvl
        A = jnp.where(hit, a_l, 0.0) if A is None else jnp.where(hit, a_l, A)
        sib = roll8(T, b) if 2 * b == SUBLANES else jnp.where(bit, roll8(T, b), roll8(T, SUBLANES - b))
        P = P + jnp.where(later, sib, 0.0)
        T = T + sib

    Pb, Tb, qb, kb, Ab = _blocks(P), _blocks(T), _blocks(q), _blocks(k), _blocks(A)
    zero = jnp.zeros((SUBLANES, HG_DIM), F32)
    lvl, m = 3, 1
    while m < HG_BLOCKS:
        later = [(((i // m) & 1) == 1) != rev for i in range(HG_BLOCKS)]
        wb = [jnp.exp2(Pb[i] if later[i] else Tb[i] - Pb[i]) for i in range(HG_BLOCKS)]
        qt = _rows([qb[i] * wb[i] for i in range(HG_BLOCKS) if later[i]]).astype(BF16)
        kt = _rows([zero if later[i] else kb[i] * wb[i] for i in range(HG_BLOCKS)]).astype(BF16)
        a_l = _blocks(_dot_nt(qt, kt))
        n_later = 0
        for i in range(HG_BLOCKS):
            if later[i]:
                hit = lmap_ref[i * SUBLANES:(i + 1) * SUBLANES, :] == lvl
                Ab[i] = jnp.where(hit, a_l[n_later], Ab[i])
                n_later += 1
        sib = [Tb[i ^ m] for i in range(HG_BLOCKS)]
        Pb = [Pb[i] + sib[i] if later[i] else Pb[i] for i in range(HG_BLOCKS)]
        tot = {}
        for i in range(HG_BLOCKS):
            key = i // (2 * m)
            if key not in tot:
                tot[key] = Tb[i] + sib[i]
        Tb = [tot[i // (2 * m)] for i in range(HG_BLOCKS)]
        lvl, m = lvl + 1, 2 * m

    vf = v.astype(F32)
    o = _dot(_rows(Ab).astype(BF16), v) + jnp.sum(q * k, axis=1, keepdims=True) * vf
    qt = _rows([qb[i] * jnp.exp2(Pb[i]) for i in range(HG_BLOCKS)]).astype(BF16)
    kt = _rows([kb[i] * jnp.exp2(Tb[i] - Pb[i]) for i in range(HG_BLOCKS)]).astype(BF16)
    return o, qt, kt, jnp.exp2(Tb[0])


def _hgrn_kernel(q_ref, ff_ref, fb_ref, v_ref, gate_ref, ng_ref, out_ref, o_acc, lmap_s):
    S = q_ref.shape[0]
    C = HG_CHUNK
    n = S // C

    ri = lax.broadcasted_iota(jnp.int32, (C, C), 0)
    ci = lax.broadcasted_iota(jnp.int32, (C, C), 1)
    x = ri ^ ci
    level = jnp.full((C, C), -1, jnp.int32)
    b = 1
    while b < C:
        level = level + jnp.where(x >= b, 1, 0)
        b *= 2
    lmap_s[0] = jnp.where(ci < ri, level, -1)
    lmap_s[1] = jnp.where(ci > ri, level, -1)

    def visit(c, f_ref, st, rev, first):
        r = pl.multiple_of(c * C, C)
        q = q_ref[pl.ds(r, C), :]
        v = v_ref[pl.ds(r, C), :]
        o, qt, kt, d = _hgrn_chunk(q, f_ref[pl.ds(r, C), :], v, lmap_s.at[1 if rev else 0], rev)
        o = o + _dot_nt(qt, st.astype(BF16))
        st = st * d[0:1, :] + _dot(v.astype(F32).T.astype(BF16), kt)
        if first:
            o_acc[pl.ds(r, C), :] = o
        else:
            o = _rms(o + o_acc[pl.ds(r, C), :], ng_ref[...])
            out_ref[pl.ds(r, C), :] = (o * jax.nn.silu(gate_ref[pl.ds(r, C), :])).astype(BF16)
        return st

    def step(first):
        def body(i, carry):
            st_f, st_b = carry
            return (visit(i, ff_ref, st_f, False, first),
                    visit(n - 1 - i, fb_ref, st_b, True, first))
        return body

    z = jnp.zeros((HG_DIM, HG_DIM), F32)
    carry = lax.fori_loop(0, n // 2, step(True), (z, z), unroll=HG_UNROLL)
    lax.fori_loop(n // 2, n, step(False), carry, unroll=HG_UNROLL)


def _hgrn(q, ff, fb, v, gate, ng, B, S):
    N = B * S
    C = HG_CHUNK
    assert (S // C) % 2 == 0
    blk = pl.BlockSpec((S, HG_DIM), lambda b, h: (b, h))
    return pl.pallas_call(
        _hgrn_kernel,
        grid=(B, HG_HEADS),
        in_specs=[blk, blk, blk, blk, blk, pl.BlockSpec((1, HG_DIM), lambda b, h: (0, 0))],
        out_specs=blk,
        out_shape=jax.ShapeDtypeStruct((N, HG_W), BF16),
        scratch_shapes=[pltpu.VMEM((S, HG_DIM), F32),
                        pltpu.VMEM((2, C, C), jnp.int32)],
        compiler_params=pltpu.CompilerParams(dimension_semantics=("arbitrary", "arbitrary"),
                                             vmem_limit_bytes=VMEM_LIMIT),
        name="hgrn",
    )(q, ff, fb, v, gate, ng)


def _attn_kernel(q_ref, k_ref, v_ref, ng_ref, out_ref, s_s, p_s, l_s):
    S = k_ref.shape[2]
    tq = ATT_TQ
    nq = S // tq

    def scores(i, slot):
        s_s[slot] = _dot_nt(q_ref[0, 0, i * tq:(i + 1) * tq, :], k_ref[0, 0])

    def softmax(slot):
        for r in range(0, tq, ATT_SUB):
            s = s_s[slot, r:r + ATT_SUB, :]
            p = jnp.exp2(s - jnp.max(s, axis=-1, keepdims=True))
            l_s[slot, r:r + ATT_SUB, :] = jnp.sum(p, axis=-1, keepdims=True)
            p_s[slot, r:r + ATT_SUB, :] = p.astype(BF16)

    def values(i, slot):
        o = _dot(p_s[slot], v_ref[0, 0]) / l_s[slot]
        out_ref[i * tq:(i + 1) * tq, :] = _rms(o, ng_ref[...]).astype(BF16)

    scores(0, 0)
    for i in range(nq):
        if i + 1 < nq:
            scores(i + 1, (i + 1) % 2)
        softmax(i % 2)
        values(i, i % 2)


def _attn(qa, ka, va, ng, B, S):
    tq = ATT_TQ
    return pl.pallas_call(
        _attn_kernel,
        grid=(B, MLA_HEADS),
        in_specs=[pl.BlockSpec((1, 1, S, MLA_QK_PAD), lambda b, h: (b, h, 0, 0)),
                  pl.BlockSpec((1, 1, S, MLA_QK_PAD), lambda b, h: (b, h, 0, 0)),
                  pl.BlockSpec((1, 1, S, MLA_VHEAD), lambda b, h: (b, h, 0, 0)),
                  pl.BlockSpec((1, MLA_VHEAD), lambda b, h: (0, 0))],
        out_specs=pl.BlockSpec((S, MLA_VHEAD), lambda b, h: (b, h)),
        out_shape=jax.ShapeDtypeStruct((B * S, MLA_HEADS * MLA_VHEAD), BF16),
        scratch_shapes=[pltpu.VMEM((2, tq, S), F32),
                        pltpu.VMEM((2, tq, S), BF16),
                        pltpu.VMEM((2, tq, 1), F32)],
        compiler_params=pltpu.CompilerParams(
            dimension_semantics=("arbitrary", "arbitrary"),
            vmem_limit_bytes=VMEM_LIMIT),
        name="attn",
    )(qa, ka, va, ng)


def _ffn_kernel(x_ref, xp_ref, xn_ref, hg_ref, hgp_ref, hgn_ref, at_ref, atp_ref, atn_ref,
                lng_ref, lnb_ref, wo_ref, g1_ref, b1_ref,
                wu_ref, cw_ref, cb_ref, wd_ref, g2_ref, b2_ref, out_ref,
                ext_s, u_s, act_s, acc_s, *, tiles_per_seq):
    tm = FFN_TM
    hrows = 2 * SUBLANES
    grp = tm // SUBLANES
    n = D_FF // FFN_FC
    i = pl.program_id(0)
    has_prev = jnp.where(i % tiles_per_seq != 0, 1.0, 0.0)
    has_next = jnp.where(i % tiles_per_seq != tiles_per_seq - 1, 1.0, 0.0)

    def with_halo(main, before, after):
        sub = lax.broadcasted_iota(jnp.int32, (hrows, main.shape[1]), 0)
        halo = jnp.where(sub == 0, before.astype(F32), jnp.where(sub == 1, after.astype(F32), 0.0))
        return jnp.concatenate([main, halo.astype(main.dtype)], axis=0)

    def token_mix(xx, hgx, atx):
        mix = _dot(hgx, wo_ref[0:HG_W, :]) + _dot(atx, wo_ref[HG_W:, :])
        return _ln(DN_ALPHA * _ln(xx, lng_ref[...], lnb_ref[...]) + mix, g1_ref[...], b1_ref[...])

    h1x = token_mix(
        with_halo(x_ref[...], xp_ref[SUBLANES - 1:SUBLANES, :], xn_ref[0:1, :]),
        with_halo(hg_ref[...], hgp_ref[...].astype(F32)[hrows - 1:hrows, :],
                  hgn_ref[...].astype(F32)[0:1, :]),
        with_halo(at_ref[...], atp_ref[...].astype(F32)[hrows - 1:hrows, :],
                  atn_ref[...].astype(F32)[0:1, :]))
    out_ref[...] = h1x[0:tm, :]
    sub = lax.broadcasted_iota(jnp.int32, (hrows, D_MODEL), 0)
    valid = jnp.where(sub == 0, has_prev, jnp.where(sub == 1, has_next, 0.0))
    ext_s[tm:, :] = (h1x[tm:, :] * valid).astype(BF16)

    ngrp = D_MODEL // LANES
    for c in range(ngrp):
        for a in range(SUBLANES):
            acc_s[c, pl.ds(a, grp, stride=SUBLANES), :] = out_ref[a * grp:(a + 1) * grp,
                                                                  c * LANES:(c + 1) * LANES]
        ext_s[0:tm, c * LANES:(c + 1) * LANES] = acc_s[c].astype(BF16)

    for c0 in range(0, 2 * D_FF, FFN_UP_COLS):
        u_s[:, c0:c0 + FFN_UP_COLS] = _dot(ext_s[...], wu_ref[:, c0:c0 + FFN_UP_COLS])

    half = tm // 2

    def conv(col, r0, r1):
        cw = cw_ref[:, col]
        sub8 = lax.broadcasted_iota(jnp.int32, (SUBLANES, FFN_FC), 0)
        if r0 == 0:
            first = jnp.where(sub8 == 0, u_s[tm:tm + 1, col],
                              pltpu.roll(u_s[tm - SUBLANES:tm, col], 1, 0))
            prev = jnp.concatenate([first, u_s[0:r1 - SUBLANES, col]], axis=0)
        else:
            prev = u_s[r0 - SUBLANES:r1 - SUBLANES, col]
        if r1 == tm:
            last = jnp.where(sub8 == SUBLANES - 1, u_s[tm + 1:tm + 2, col],
                             pltpu.roll(u_s[0:SUBLANES, col], SUBLANES - 1, 0))
            nxt = jnp.concatenate([u_s[r0 + SUBLANES:tm, col], last], axis=0)
        else:
            nxt = u_s[r0 + SUBLANES:r1 + SUBLANES, col]
        return prev * cw[0:1, :] + u_s[r0:r1, col] * cw[1:2, :] + nxt * cw[2:3, :] + cb_ref[:, col]

    def glu(j, carry):
        ca = pl.ds(pl.multiple_of(j * FFN_FC, FFN_FC), FFN_FC)
        cb = pl.ds(pl.multiple_of(D_FF + j * FFN_FC, FFN_FC), FFN_FC)
        for r0 in (0, half):
            act = _gelu_tanh(conv(ca, r0, r0 + half)) * conv(cb, r0, r0 + half)
            act_s[r0:r0 + half, ca] = act.astype(BF16)
        return carry

    lax.fori_loop(0, n, glu, 0)

    ffn_il = _dot(act_s[...], wd_ref[...])
    for c in range(ngrp):
        acc_s[c] = ffn_il[:, c * LANES:(c + 1) * LANES]

    for a in range(SUBLANES):
        ffn = jnp.concatenate([acc_s[c, pl.ds(a, grp, stride=SUBLANES), :] for c in range(ngrp)],
                              axis=1)
        rows = slice(a * grp, (a + 1) * grp)
        out_ref[rows, :] = _ln(DN_ALPHA * out_ref[rows, :] + ffn, g2_ref[...], b2_ref[...])


def _ffn(x2, hg, at, ln_g, ln_b, wo, g1, b1, wu, cw, cb, wd, g2, b2, S):
    N = x2.shape[0]
    tm = FFN_TM
    hrows = 2 * SUBLANES
    row = lambda i: (i, 0)
    full = lambda a: pl.BlockSpec(a.shape, lambda i: (0,) * a.ndim)

    def halo_specs(rows, width):
        per_tile = tm // rows
        last = N // rows - 1
        return [pl.BlockSpec((rows, width), lambda i: (jnp.maximum(i * per_tile - 1, 0), 0)),
                pl.BlockSpec((rows, width), lambda i: (jnp.minimum((i + 1) * per_tile, last), 0))]

    return pl.pallas_call(
        functools.partial(_ffn_kernel, tiles_per_seq=S // tm),
        grid=(N // tm,),
        in_specs=([pl.BlockSpec((tm, D_MODEL), row)] + halo_specs(SUBLANES, D_MODEL)
                  + [pl.BlockSpec((tm, HG_W), row)] + halo_specs(hrows, HG_W)
                  + [pl.BlockSpec((tm, HG_W), row)] + halo_specs(hrows, HG_W)
                  + [full(a) for a in (ln_g, ln_b, wo, g1, b1, wu, cw, cb, wd, g2, b2)]),
        out_specs=pl.BlockSpec((tm, D_MODEL), row),
        out_shape=jax.ShapeDtypeStruct((N, D_MODEL), F32),
        scratch_shapes=[pltpu.VMEM((tm + hrows, D_MODEL), BF16),
                        pltpu.VMEM((tm + hrows, 2 * D_FF), F32),
                        pltpu.VMEM((tm, D_FF), BF16),
                        pltpu.VMEM((D_MODEL // LANES, tm, LANES), F32)],
        compiler_params=pltpu.CompilerParams(dimension_semantics=("arbitrary",),
                                             vmem_limit_bytes=VMEM_LIMIT),
        name="ffn",
    )(x2, x2, x2, hg, hg, hg, at, at, at, ln_g, ln_b, wo, g1, b1, wu, cw, cb, wd, g2, b2)


def _swap_halves(w):
    half = MLA_ROPE // 2
    return jnp.concatenate([w[..., half:], w[..., :half]], axis=-1)


def kernel(x, positions, ln_in_g, ln_in_b, w_in, lb_fwd, lb_bwd, hg_norm_g, q_a_norm_g, w_q_b,
           kv_a_norm_g, w_kv_b, attn_norm_g, w_out, ln1_g, ln1_b, w_up, conv_w, conv_b, w_down,
           ln2_g, ln2_b):
    B, S, D = x.shape
    assert D == D_MODEL and w_in.shape[0] == 1 and lb_fwd.shape == (2, HG_W)
    assert S % HG_CHUNK == 0 and S % ATT_TQ == 0 and S % FFN_TM == 0 and D_FF % FFN_FC == 0
    N = B * S
    r1 = lambda a: a.reshape(1, -1)

    w_in0 = w_in[0]
    c_lat = 5 * HG_W
    c_kr = c_lat + MLA_Q_RANK + MLA_KV_RANK
    w_hg = w_in0[:, :c_lat].astype(BF16)
    w_kr = w_in0[:, c_kr:]
    w_lat = jnp.concatenate([w_in0[:, c_lat:c_kr], w_kr, _swap_halves(w_kr)], axis=1).astype(BF16)
    wq = w_q_b[0].reshape(MLA_Q_RANK, MLA_HEADS, MLA_NOPE + MLA_ROPE)
    wq_pe = wq[:, :, MLA_NOPE:]
    wqb = jnp.concatenate(
        [wq[:, :, :MLA_NOPE].reshape(MLA_Q_RANK, -1),
         jnp.concatenate([wq_pe, _swap_halves(wq_pe)], axis=-1).reshape(MLA_Q_RANK, -1)],
        axis=1).astype(BF16)
    wkvb = w_kv_b[0].astype(BF16)
    half = MLA_ROPE // 2
    inv_freq = 1.0 / (ROPE_THETA ** (jnp.arange(half, dtype=F32) / half))
    zeros = jnp.zeros((LANES - MLA_ROPE,), F32)
    invf = r1(jnp.tile(inv_freq, LANES // half))
    sgn = r1(jnp.concatenate([-jnp.ones((half,), F32), jnp.ones((half,), F32), zeros]))
    wu = w_up[0].astype(BF16)
    cw = conv_w[0]
    cb = r1(conv_b[0])
    wd = w_down[0].astype(BF16)

    x2 = x.reshape(N, D)
    pos2 = positions.reshape(N // LANES, LANES)
    q, ff, fb, gate, v, qa, ka, va = _proj(
        x2, pos2, r1(ln_in_g), r1(ln_in_b), w_hg, w_lat, lb_fwd, lb_bwd,
        r1(q_a_norm_g[0]), wqb, r1(kv_a_norm_g[0]), wkvb, invf, sgn, B, S)
    hg = _hgrn(q, ff, fb, v, gate, r1(hg_norm_g[0]), B, S)
    at = _attn(qa, ka, va, r1(attn_norm_g[0]), B, S)
    out = _ffn(x2, hg, at, r1(ln_in_g), r1(ln_in_b), w_out[0].astype(BF16), r1(ln1_g[0]), r1(ln1_b[0]),
               wu, cw, cb, wd, r1(ln2_g[0]), r1(ln2_b[0]), S)
    return out.reshape(B, S, D)
```

```python
import functools

import jax
import jax.numpy as jnp
from jax import lax
from jax.experimental import pallas as pl
from jax.experimental.pallas import tpu as pltpu

D_MODEL = 1024
HG_HEADS = 4
HG_DIM = 128
HG_W = HG_HEADS * HG_DIM
MLA_HEADS = 4
MLA_Q_RANK = 384
MLA_KV_RANK = 128
MLA_NOPE = 128
MLA_ROPE = 64
MLA_VHEAD = 128
MLA_QK_PAD = 256
ROPE_THETA = 10000.0
D_FF = 2816
DN_ALPHA = 2.0 ** 0.25
NORM_EPS = 1e-5
LOG2E = 1.4426950408889634

LANES = 128
SUBLANES = 8
VMEM_LIMIT = 56 * 1024 * 1024

PROJ_TM = 1024
HG_CHUNK = 256
HG_BLOCKS = HG_CHUNK // SUBLANES
HG_UNROLL = 8
ATT_TQ = 256
ATT_SUB = 32
FFN_TM = 512
FFN_FC = 256

F32 = jnp.float32
BF16 = jnp.bfloat16


def _ln(x, g, b):
    mu = jnp.mean(x, axis=-1, keepdims=True)
    xc = x - mu
    var = jnp.mean(xc * xc, axis=-1, keepdims=True)
    return xc * lax.rsqrt(var + NORM_EPS) * g + b


def _rms(x, g):
    return x * lax.rsqrt(jnp.mean(x * x, axis=-1, keepdims=True) + NORM_EPS) * g


def _dot(a, b):
    return jnp.dot(a, b, preferred_element_type=F32)


def _dot_nt(a, b):
    return lax.dot_general(a, b, (((1,), (1,)), ((), ())), preferred_element_type=F32)


def _proj_kernel(x_ref, pos_ref, lng_ref, lnb_ref, w_hg_ref, w_lat_ref, lbf_ref, lbb_ref,
                 qan_ref, wqb_ref, kvn_ref, wkvb_ref, invf_ref, sgn_ref,
                 q_ref, ff_ref, fb_ref, gate_ref, v_ref, qa_ref, ka_ref, va_ref):
    h = _ln(x_ref[...], lng_ref[...], lnb_ref[...]).astype(BF16)

    def lower_bound(ref):
        r0, r1 = ref[0:1, :], ref[1:2, :]
        m = jnp.maximum(r0, r1)
        e0, e1 = jnp.exp(r0 - m), jnp.exp(r1 - m)
        return e0 / (e0 + e1)

    def hg_cols(j):
        return _dot(h, w_hg_ref[:, j * HG_W:(j + 1) * HG_W])

    q_ref[...] = jax.nn.silu(hg_cols(0))
    v_ref[...] = hg_cols(1).astype(BF16)
    lbf = lower_bound(lbf_ref)
    ff_ref[...] = lbf + (1.0 - lbf) * jax.nn.sigmoid(hg_cols(2))
    lbb = lower_bound(lbb_ref)
    fb_ref[...] = lbb + (1.0 - lbb) * jax.nn.sigmoid(hg_cols(3))
    gate_ref[...] = hg_cols(4)

    hr = x_ref.shape[0] // 2
    pos_t = pos_ref[...].astype(F32).T
    pos = jnp.concatenate([pos_t[:, j:j + 1] for j in range(pos_t.shape[1])], axis=0)
    lane = lax.broadcasted_iota(jnp.int32, (hr, LANES), 1)
    ang = jnp.where(lane < MLA_ROPE, pos[0:hr, :], pos[hr:, :]) * invf_ref[...]
    cs, sn = jnp.cos(ang), jnp.sin(ang)
    sgn = sgn_ref[...]
    keep = jnp.abs(sgn)
    cos_t = jnp.concatenate([cs * keep, pltpu.roll(cs, MLA_ROPE, 1) * keep], axis=0)
    sin_t = jnp.concatenate([sn * sgn, pltpu.roll(sn, MLA_ROPE, 1) * sgn], axis=0)

    def rope(grp):
        return grp * cos_t + pltpu.roll(grp, MLA_ROPE, 1) * sin_t

    lat = _dot(h, w_lat_ref[...])
    qa = _rms(lat[:, :MLA_Q_RANK], qan_ref[...]).astype(BF16)
    kva = _rms(lat[:, MLA_Q_RANK:MLA_Q_RANK + MLA_KV_RANK], kvn_ref[...]).astype(BF16)
    k_pe = rope(lat[:, MLA_Q_RANK + MLA_KV_RANK:]).astype(BF16)

    scale = (MLA_NOPE + MLA_ROPE) ** -0.5 * LOG2E
    qh = _dot(qa, wqb_ref[...])
    kvh = _dot(kva, wkvb_ref[...])
    for hd in range(MLA_HEADS):
        qa_ref[0, hd, :, 0:LANES] = (qh[:, hd * LANES:(hd + 1) * LANES] * scale).astype(BF16)
        pe = rope(qh[:, (MLA_HEADS + hd) * LANES:(MLA_HEADS + hd + 1) * LANES])
        qa_ref[0, hd, :, LANES:2 * LANES] = (pe * scale).astype(BF16)
        ka_ref[0, hd, :, 0:LANES] = kvh[:, 2 * hd * LANES:(2 * hd + 1) * LANES].astype(BF16)
        ka_ref[0, hd, :, LANES:2 * LANES] = k_pe
        va_ref[0, hd, :, :] = kvh[:, (2 * hd + 1) * LANES:(2 * hd + 2) * LANES].astype(BF16)


def _proj(x2, pos2, ln_g, ln_b, w_hg, w_lat, lb_f, lb_b, qan, wqb, kvn, wkvb, invf, sgn, B, S):
    N = B * S
    tm = PROJ_TM
    spb = S // tm
    row = lambda i: (i, 0)
    const = lambda i: (0, 0)
    head = lambda i: (i // spb, 0, i % spb, 0)
    full = lambda a: pl.BlockSpec(a.shape, const)
    out_shape = (
        jax.ShapeDtypeStruct((N, HG_W), F32),
        jax.ShapeDtypeStruct((N, HG_W), F32),
        jax.ShapeDtypeStruct((N, HG_W), F32),
        jax.ShapeDtypeStruct((N, HG_W), F32),
        jax.ShapeDtypeStruct((N, HG_W), BF16),
        jax.ShapeDtypeStruct((B, MLA_HEADS, S, MLA_QK_PAD), BF16),
        jax.ShapeDtypeStruct((B, MLA_HEADS, S, MLA_QK_PAD), BF16),
        jax.ShapeDtypeStruct((B, MLA_HEADS, S, MLA_VHEAD), BF16),
    )
    tok = pl.BlockSpec((tm, HG_W), row)
    return pl.pallas_call(
        _proj_kernel,
        grid=(N // tm,),
        in_specs=[pl.BlockSpec((tm, D_MODEL), row), pl.BlockSpec((tm // LANES, LANES), row),
                  full(ln_g), full(ln_b), full(w_hg), full(w_lat), full(lb_f), full(lb_b),
                  full(qan), full(wqb), full(kvn), full(wkvb), full(invf), full(sgn)],
        out_specs=(tok, tok, tok, tok, tok,
                   pl.BlockSpec((1, MLA_HEADS, tm, MLA_QK_PAD), head),
                   pl.BlockSpec((1, MLA_HEADS, tm, MLA_QK_PAD), head),
                   pl.BlockSpec((1, MLA_HEADS, tm, MLA_VHEAD), head)),
        out_shape=out_shape,
        compiler_params=pltpu.CompilerParams(dimension_semantics=("arbitrary",),
                                             vmem_limit_bytes=VMEM_LIMIT),
        name="proj",
    )(x2, pos2, ln_g, ln_b, w_hg, w_lat, lb_f, lb_b, qan, wqb, kvn, wkvb, invf, sgn)


def _blocks(x):
    return [x[i * SUBLANES:(i + 1) * SUBLANES] for i in range(x.shape[0] // SUBLANES)]


def _rows(blocks):
    return jnp.concatenate(blocks, axis=0)


def _hgrn_chunk(q, f, v, lmap_ref, rev):
    C = HG_CHUNK
    g = jnp.log(f) * LOG2E
    k = 1.0 - f
    row = lax.broadcasted_iota(jnp.int32, (C, HG_DIM), 0)

    def roll8(x, shift):
        return _rows([pltpu.roll(blk, shift, 0) for blk in _blocks(x)])

    P, T = g, g
    A = None
    for lvl in range(3):
        b = 1 << lvl
        bit = ((row >> lvl) & 1) == 1
        later = jnp.logical_not(bit) if rev else bit
        w = jnp.exp2(jnp.where(later, P, T - P))
        a_l = _dot_nt((q * w).astype(BF16), (k * w).astype(BF16))
        hit = lmap_ref[...] == lvl
        A = jnp.where(hit, a_l, 0.0) if A is None else jnp.where(hit, a_l, A)
        sib = roll8(T, b) if 2 * b == SUBLANES else jnp.where(bit, roll8(T, b), roll8(T, SUBLANES - b))
        P = P + jnp.where(later, sib, 0.0)
        T = T + sib

    Pb, Tb, qb, kb, Ab = _blocks(P), _blocks(T), _blocks(q), _blocks(k), _blocks(A)
    zero = jnp.zeros((SUBLANES, HG_DIM), F32)
    lvl, m = 3, 1
    while m < HG_BLOCKS:
        later = [(((i // m) & 1) == 1) != rev for i in range(HG_BLOCKS)]
        wb = [jnp.exp2(Pb[i] if later[i] else Tb[i] - Pb[i]) for i in range(HG_BLOCKS)]
        qt = _rows([qb[i] * wb[i] for i in range(HG_BLOCKS) if later[i]]).astype(BF16)
        kt = _rows([zero if later[i] else kb[i] * wb[i] for i in range(HG_BLOCKS)]).astype(BF16)
        a_l = _blocks(_dot_nt(qt, kt))
        n_later = 0
        for i in range(HG_BLOCKS):
            if later[i]:
                hit = lmap_ref[i * SUBLANES:(i + 1) * SUBLANES, :] == lvl
                Ab[i] = jnp.where(hit, a_l[n_later], Ab[i])
                n_later += 1
        sib = [Tb[i ^ m] for i in range(HG_BLOCKS)]
        Pb = [Pb[i] + sib[i] if later[i] else Pb[i] for i in range(HG_BLOCKS)]
        tot = {}
        for i in range(HG_BLOCKS):
            key = i // (2 * m)
            if key not in tot:
                tot[key] = Tb[i] + sib[i]
        Tb = [tot[i // (2 * m)] for i in range(HG_BLOCKS)]
        lvl, m = lvl + 1, 2 * m

    vf = v.astype(F32)
    o = _dot(_rows(Ab).astype(BF16), v) + jnp.sum(q * k, axis=1, keepdims=True) * vf
    qt = _rows([qb[i] * jnp.exp2(Pb[i]) for i in range(HG_BLOCKS)]).astype(BF16)
    kt = _rows([kb[i] * jnp.exp2(Tb[i] - Pb[i]) for i in range(HG_BLOCKS)]).astype(BF16)
    return o, qt, kt, jnp.exp2(Tb[0])


def _hgrn_kernel(q_ref, ff_ref, fb_ref, v_ref, gate_ref, ng_ref, out_ref, o_acc, lmap_s):
    S = q_ref.shape[0]
    C = HG_CHUNK
    n = S // C

    ri = lax.broadcasted_iota(jnp.int32, (C, C), 0)
    ci = lax.broadcasted_iota(jnp.int32, (C, C), 1)
    x = ri ^ ci
    level = jnp.full((C, C), -1, jnp.int32)
    b = 1
    while b < C:
        level = level + jnp.where(x >= b, 1, 0)
        b *= 2
    lmap_s[0] = jnp.where(ci < ri, level, -1)
    lmap_s[1] = jnp.where(ci > ri, level, -1)

    def visit(c, f_ref, st, rev, first):
        r = pl.multiple_of(c * C, C)
        q = q_ref[pl.ds(r, C), :]
        v = v_ref[pl.ds(r, C), :]
        o, qt, kt, d = _hgrn_chunk(q, f_ref[pl.ds(r, C), :], v, lmap_s.at[1 if rev else 0], rev)
        o = o + _dot_nt(qt, st.astype(BF16))
        st = st * d[0:1, :] + _dot(v.astype(F32).T.astype(BF16), kt)
        if first:
            o_acc[pl.ds(r, C), :] = o
        else:
            o = _rms(o + o_acc[pl.ds(r, C), :], ng_ref[...])
            out_ref[pl.ds(r, C), :] = (o * jax.nn.silu(gate_ref[pl.ds(r, C), :])).astype(BF16)
        return st

    def step(first):
        def body(i, carry):
            st_f, st_b = carry
            return (visit(i, ff_ref, st_f, False, first),
                    visit(n - 1 - i, fb_ref, st_b, True, first))
        return body

    z = jnp.zeros((HG_DIM, HG_DIM), F32)
    carry = lax.fori_loop(0, n // 2, step(True), (z, z), unroll=HG_UNROLL)
    lax.fori_loop(n // 2, n, step(False), carry, unroll=HG_UNROLL)


def _hgrn(q, ff, fb, v, gate, ng, B, S):
    N = B * S
    C = HG_CHUNK
    assert (S // C) % 2 == 0
    blk = pl.BlockSpec((S, HG_DIM), lambda b, h: (b, h))
    return pl.pallas_call(
        _hgrn_kernel,
        grid=(B, HG_HEADS),
        in_specs=[blk, blk, blk, blk, blk, pl.BlockSpec((1, HG_DIM), lambda b, h: (0, 0))],
        out_specs=blk,
        out_shape=jax.ShapeDtypeStruct((N, HG_W), BF16),
        scratch_shapes=[pltpu.VMEM((S, HG_DIM), F32),
                        pltpu.VMEM((2, C, C), jnp.int32)],
        compiler_params=pltpu.CompilerParams(dimension_semantics=("arbitrary", "arbitrary"),
                                             vmem_limit_bytes=VMEM_LIMIT),
        name="hgrn",
    )(q, ff, fb, v, gate, ng)


def _attn_kernel(q_ref, k_ref, v_ref, ng_ref, out_ref, s_s, p_s, l_s):
    S = k_ref.shape[2]
    tq = ATT_TQ
    nq = S // tq

    def scores(i, slot):
        s_s[slot] = _dot_nt(q_ref[0, 0, i * tq:(i + 1) * tq, :], k_ref[0, 0])

    def softmax(slot):
        for r in range(0, tq, ATT_SUB):
            s = s_s[slot, r:r + ATT_SUB, :]
            p = jnp.exp2(s - jnp.max(s, axis=-1, keepdims=True))
            l_s[slot, r:r + ATT_SUB, :] = jnp.sum(p, axis=-1, keepdims=True)
            p_s[slot, r:r + ATT_SUB, :] = p.astype(BF16)

    def values(i, slot):
        o = _dot(p_s[slot], v_ref[0, 0]) / l_s[slot]
        out_ref[i * tq:(i + 1) * tq, :] = _rms(o, ng_ref[...]).astype(BF16)

    scores(0, 0)
    for i in range(nq):
        if i + 1 < nq:
            scores(i + 1, (i + 1) % 2)
        softmax(i % 2)
        values(i, i % 2)


def _attn(qa, ka, va, ng, B, S):
    tq = ATT_TQ
    return pl.pallas_call(
        _attn_kernel,
        grid=(B, MLA_HEADS),
        in_specs=[pl.BlockSpec((1, 1, S, MLA_QK_PAD), lambda b, h: (b, h, 0, 0)),
                  pl.BlockSpec((1, 1, S, MLA_QK_PAD), lambda b, h: (b, h, 0, 0)),
                  pl.BlockSpec((1, 1, S, MLA_VHEAD), lambda b, h: (b, h, 0, 0)),
                  pl.BlockSpec((1, MLA_VHEAD), lambda b, h: (0, 0))],
        out_specs=pl.BlockSpec((S, MLA_VHEAD), lambda b, h: (b, h)),
        out_shape=jax.ShapeDtypeStruct((B * S, MLA_HEADS * MLA_VHEAD), BF16),
        scratch_shapes=[pltpu.VMEM((2, tq, S), F32),
                        pltpu.VMEM((2, tq, S), BF16),
                        pltpu.VMEM((2, tq, 1), F32)],
        compiler_params=pltpu.CompilerParams(
            dimension_semantics=("arbitrary", "arbitrary"),
            vmem_limit_bytes=VMEM_LIMIT),
        name="attn",
    )(qa, ka, va, ng)


def _ffn_kernel(x_ref, xp_ref, xn_ref, hg_ref, hgp_ref, hgn_ref, at_ref, atp_ref, atn_ref,
                lng_ref, lnb_ref, wo_ref, g1_ref, b1_ref,
                wu_ref, cw_ref, cb_ref, wd_ref, g2_ref, b2_ref, out_ref,
                h1_s, il_s, ext_s, u_s, act_s, acc_s, *, tiles_per_seq):
    tm = FFN_TM
    hrows = 2 * SUBLANES
    grp = tm // SUBLANES
    n = D_FF // FFN_FC
    i = pl.program_id(0)
    has_prev = jnp.where(i % tiles_per_seq != 0, 1.0, 0.0)
    has_next = jnp.where(i % tiles_per_seq != tiles_per_seq - 1, 1.0, 0.0)

    def with_halo(main, before, after):
        sub = lax.broadcasted_iota(jnp.int32, (hrows, main.shape[1]), 0)
        halo = jnp.where(sub == 0, before.astype(F32), jnp.where(sub == 1, after.astype(F32), 0.0))
        return jnp.concatenate([main, halo.astype(main.dtype)], axis=0)

    def token_mix(xx, hgx, atx):
        mix = _dot(hgx, wo_ref[0:HG_W, :]) + _dot(atx, wo_ref[HG_W:, :])
        return _ln(DN_ALPHA * _ln(xx, lng_ref[...], lnb_ref[...]) + mix, g1_ref[...], b1_ref[...])

    h1x = token_mix(
        with_halo(x_ref[...], xp_ref[SUBLANES - 1:SUBLANES, :], xn_ref[0:1, :]),
        with_halo(hg_ref[...], hgp_ref[...].astype(F32)[hrows - 1:hrows, :],
                  hgn_ref[...].astype(F32)[0:1, :]),
        with_halo(at_ref[...], atp_ref[...].astype(F32)[hrows - 1:hrows, :],
                  atn_ref[...].astype(F32)[0:1, :]))
    h1_s[...] = h1x[0:tm, :]
    sub = lax.broadcasted_iota(jnp.int32, (hrows, D_MODEL), 0)
    valid = jnp.where(sub == 0, has_prev, jnp.where(sub == 1, has_next, 0.0))
    ext_s[tm:, :] = (h1x[tm:, :] * valid).astype(BF16)

    ngrp = D_MODEL // LANES
    for c in range(ngrp):
        for a in range(SUBLANES):
            il_s[c, pl.ds(a, grp, stride=SUBLANES), :] = h1_s[a * grp:(a + 1) * grp,
                                                              c * LANES:(c + 1) * LANES]
        ext_s[0:tm, c * LANES:(c + 1) * LANES] = il_s[c].astype(BF16)
    acc_s[...] = jnp.zeros_like(acc_s)

    half = tm // 2

    def chunk_cols(j, part):
        start = part * D_FF + j * FFN_FC
        return pl.ds(start if isinstance(start, int) else pl.multiple_of(start, FFN_FC), FFN_FC)

    def up(j, slot, part):
        u_s[slot, :, part * FFN_FC:(part + 1) * FFN_FC] = _dot(ext_s[...],
                                                               wu_ref[:, chunk_cols(j, part)])

    def glu(j, slot, part):
        r0, r1 = part * half, (part + 1) * half
        cw = jnp.concatenate([cw_ref[:, chunk_cols(j, 0)], cw_ref[:, chunk_cols(j, 1)]], axis=1)
        cb = jnp.concatenate([cb_ref[:, chunk_cols(j, 0)], cb_ref[:, chunk_cols(j, 1)]], axis=1)
        sub8 = lax.broadcasted_iota(jnp.int32, (SUBLANES, 2 * FFN_FC), 0)
        if part == 0:
            first = jnp.where(sub8 == 0, u_s[slot, tm:tm + 1, :],
                              pltpu.roll(u_s[slot, tm - SUBLANES:tm, :], 1, 0))
            prev = jnp.concatenate([first, u_s[slot, 0:r1 - SUBLANES, :]], axis=0)
            nxt = u_s[slot, SUBLANES:r1 + SUBLANES, :]
        else:
            last = jnp.where(sub8 == SUBLANES - 1, u_s[slot, tm + 1:tm + 2, :],
                             pltpu.roll(u_s[slot, 0:SUBLANES, :], SUBLANES - 1, 0))
            prev = u_s[slot, r0 - SUBLANES:r1 - SUBLANES, :]
            nxt = jnp.concatenate([u_s[slot, r0 + SUBLANES:tm, :], last], axis=0)
        c = prev * cw[0:1, :] + u_s[slot, r0:r1, :] * cw[1:2, :] + nxt * cw[2:3, :] + cb
        act_s[slot, r0:r1, :] = (jax.nn.gelu(c[:, :FFN_FC]) * c[:, FFN_FC:]).astype(BF16)

    def down(j, slot, part):
        c0 = part * 2 * LANES
        k0 = j * FFN_FC
        rows = pl.ds(k0 if isinstance(k0, int) else pl.multiple_of(k0, FFN_FC), FFN_FC)
        res = _dot(act_s[slot], wd_ref[rows, c0:c0 + 2 * LANES])
        acc_s[2 * part] += res[:, :LANES]
        acc_s[2 * part + 1] += res[:, LANES:]

    def stage(t, par):
        static = isinstance(t, int)
        do_up = not static or t + 1 < n
        do_glu = not static or 0 <= t < n
        do_down = not static or 1 <= t <= n
        for part in range(2):
            if do_up:
                up(t + 1, 1 - par, part)
            if do_glu:
                glu(t, par, part)
            if do_down:
                down(t - 1, 1 - par, 2 * part)
                down(t - 1, 1 - par, 2 * part + 1)

    stage(-1, 1)
    stage(0, 0)
    npairs = (n - 2) // 2

    def pair(jj, carry):
        t = 2 * jj + 1
        stage(t, 1)
        stage(t + 1, 0)
        return carry

    lax.fori_loop(0, npairs, pair, 0, unroll=True)
    for t in range(2 * npairs + 1, n + 1):
        stage(t, t % 2)

    for a in range(SUBLANES):
        ffn = jnp.concatenate([acc_s[c, pl.ds(a, grp, stride=SUBLANES), :] for c in range(ngrp)],
                              axis=1)
        rows = slice(a * grp, (a + 1) * grp)
        out_ref[rows, :] = _ln(DN_ALPHA * h1_s[rows, :] + ffn, g2_ref[...], b2_ref[...])


def _ffn(x2, hg, at, ln_g, ln_b, wo, g1, b1, wu, cw, cb, wd, g2, b2, S):
    N = x2.shape[0]
    tm = FFN_TM
    hrows = 2 * SUBLANES
    row = lambda i: (i, 0)
    full = lambda a: pl.BlockSpec(a.shape, lambda i: (0,) * a.ndim)

    def halo_specs(rows, width):
        per_tile = tm // rows
        last = N // rows - 1
        return [pl.BlockSpec((rows, width), lambda i: (jnp.maximum(i * per_tile - 1, 0), 0)),
                pl.BlockSpec((rows, width), lambda i: (jnp.minimum((i + 1) * per_tile, last), 0))]

    return pl.pallas_call(
        functools.partial(_ffn_kernel, tiles_per_seq=S // tm),
        grid=(N // tm,),
        in_specs=([pl.BlockSpec((tm, D_MODEL), row)] + halo_specs(SUBLANES, D_MODEL)
                  + [pl.BlockSpec((tm, HG_W), row)] + halo_specs(hrows, HG_W)
                  + [pl.BlockSpec((tm, HG_W), row)] + halo_specs(hrows, HG_W)
                  + [full(a) for a in (ln_g, ln_b, wo, g1, b1, wu, cw, cb, wd, g2, b2)]),
        out_specs=pl.BlockSpec((tm, D_MODEL), row),
        out_shape=jax.ShapeDtypeStruct((N, D_MODEL), F32),
        scratch_shapes=[pltpu.VMEM((tm, D_MODEL), F32),
                        pltpu.VMEM((D_MODEL // LANES, tm, LANES), F32),
                        pltpu.VMEM((tm + hrows, D_MODEL), BF16),
                        pltpu.VMEM((2, tm + hrows, 2 * FFN_FC), F32),
                        pltpu.VMEM((2, tm, FFN_FC), BF16),
                        pltpu.VMEM((D_MODEL // LANES, tm, LANES), F32)],
        compiler_params=pltpu.CompilerParams(dimension_semantics=("arbitrary",),
                                             vmem_limit_bytes=VMEM_LIMIT),
        name="ffn",
    )(x2, x2, x2, hg, hg, hg, at, at, at, ln_g, ln_b, wo, g1, b1, wu, cw, cb, wd, g2, b2)


def _swap_halves(w):
    half = MLA_ROPE // 2
    return jnp.concatenate([w[..., half:], w[..., :half]], axis=-1)


def kernel(x, positions, ln_in_g, ln_in_b, w_in, lb_fwd, lb_bwd, hg_norm_g, q_a_norm_g, w_q_b,
           kv_a_norm_g, w_kv_b, attn_norm_g, w_out, ln1_g, ln1_b, w_up, conv_w, conv_b, w_down,
           ln2_g, ln2_b):
    B, S, D = x.shape
    assert D == D_MODEL and w_in.shape[0] == 1 and lb_fwd.shape == (2, HG_W)
    assert S % HG_CHUNK == 0 and S % ATT_TQ == 0 and S % FFN_TM == 0 and D_FF % FFN_FC == 0
    N = B * S
    r1 = lambda a: a.reshape(1, -1)

    w_in0 = w_in[0]
    c_lat = 5 * HG_W
    c_kr = c_lat + MLA_Q_RANK + MLA_KV_RANK
    w_hg = w_in0[:, :c_lat].astype(BF16)
    w_kr = w_in0[:, c_kr:]
    w_lat = jnp.concatenate([w_in0[:, c_lat:c_kr], w_kr, _swap_halves(w_kr)], axis=1).astype(BF16)
    wq = w_q_b[0].reshape(MLA_Q_RANK, MLA_HEADS, MLA_NOPE + MLA_ROPE)
    wq_pe = wq[:, :, MLA_NOPE:]
    wqb = jnp.concatenate(
        [wq[:, :, :MLA_NOPE].reshape(MLA_Q_RANK, -1),
         jnp.concatenate([wq_pe, _swap_halves(wq_pe)], axis=-1).reshape(MLA_Q_RANK, -1)],
        axis=1).astype(BF16)
    wkvb = w_kv_b[0].astype(BF16)
    half = MLA_ROPE // 2
    inv_freq = 1.0 / (ROPE_THETA ** (jnp.arange(half, dtype=F32) / half))
    zeros = jnp.zeros((LANES - MLA_ROPE,), F32)
    invf = r1(jnp.tile(inv_freq, LANES // half))
    sgn = r1(jnp.concatenate([-jnp.ones((half,), F32), jnp.ones((half,), F32), zeros]))
    wu = w_up[0].astype(BF16)
    cw = conv_w[0]
    cb = r1(conv_b[0])
    wd = w_down[0].astype(BF16)

    x2 = x.reshape(N, D)
    pos2 = positions.reshape(N // LANES, LANES)
    q, ff, fb, gate, v, qa, ka, va = _proj(
        x2, pos2, r1(ln_in_g), r1(ln_in_b), w_hg, w_lat, lb_fwd, lb_bwd,
        r1(q_a_norm_g[0]), wqb, r1(kv_a_norm_g[0]), wkvb, invf, sgn, B, S)
    hg = _hgrn(q, ff, fb, v, gate, r1(hg_norm_g[0]), B, S)
    at = _attn(qa, ka, va, r1(attn_norm_g[0]), B, S)
    out = _ffn(x2, hg, at, r1(ln_in_g), r1(ln_in_b), w_out[0].astype(BF16), r1(ln1_g[0]), r1(ln1_b[0]),
               wu, cw, cb, wd, r1(ln2_g[0]), r1(ln2_b[0]), S)
    return out.reshape(B, S, D)
```

```python
import functools

import jax
import jax.numpy as jnp
from jax import lax
from jax.experimental import pallas as pl
from jax.experimental.pallas import tpu as pltpu

D_MODEL = 1024
HG_HEADS = 4
HG_DIM = 128
HG_W = HG_HEADS * HG_DIM
MLA_HEADS = 4
MLA_Q_RANK = 384
MLA_KV_RANK = 128
MLA_NOPE = 128
MLA_ROPE = 64
MLA_VHEAD = 128
MLA_QK_PAD = 256
ROPE_THETA = 10000.0
D_FF = 2816
DN_ALPHA = 2.0 ** 0.25
NORM_EPS = 1e-5
LOG2E = 1.4426950408889634

LANES = 128
SUBLANES = 8
VMEM_LIMIT = 56 * 1024 * 1024

PROJ_TM = 1024
HG_CHUNK = 256
HG_BLOCKS = HG_CHUNK // SUBLANES
HG_UNROLL = 8
ATT_TQ = 256
ATT_SUB = 32
FFN_TM = 512
FFN_FC = 256

F32 = jnp.float32
BF16 = jnp.bfloat16


def _ln(x, g, b):
    mu = jnp.mean(x, axis=-1, keepdims=True)
    xc = x - mu
    var = jnp.mean(xc * xc, axis=-1, keepdims=True)
    return xc * lax.rsqrt(var + NORM_EPS) * g + b


def _rms(x, g):
    return x * lax.rsqrt(jnp.mean(x * x, axis=-1, keepdims=True) + NORM_EPS) * g


def _gelu_tanh(x):
    c1 = -2.0 * (2.0 / jnp.pi) ** 0.5 * LOG2E
    z = x * (c1 + (c1 * 0.044715) * (x * x))
    return x / (1.0 + jnp.exp2(z))


def _dot(a, b):
    return jnp.dot(a, b, preferred_element_type=F32)


def _dot_nt(a, b):
    return lax.dot_general(a, b, (((1,), (1,)), ((), ())), preferred_element_type=F32)


def _proj_kernel(x_ref, pos_ref, lng_ref, lnb_ref, w_hg_ref, w_lat_ref, lbf_ref, lbb_ref,
                 qan_ref, wqb_ref, kvn_ref, wkvb_ref, invf_ref, sgn_ref,
                 q_ref, ff_ref, fb_ref, gate_ref, v_ref, qa_ref, ka_ref, va_ref):
    h = _ln(x_ref[...], lng_ref[...], lnb_ref[...]).astype(BF16)

    def lower_bound(ref):
        r0, r1 = ref[0:1, :], ref[1:2, :]
        m = jnp.maximum(r0, r1)
        e0, e1 = jnp.exp(r0 - m), jnp.exp(r1 - m)
        return e0 / (e0 + e1)

    def hg_cols(j):
        return _dot(h, w_hg_ref[:, j * HG_W:(j + 1) * HG_W])

    q_ref[...] = jax.nn.silu(hg_cols(0))
    v_ref[...] = hg_cols(1).astype(BF16)
    lbf = lower_bound(lbf_ref)
    ff_ref[...] = lbf + (1.0 - lbf) * jax.nn.sigmoid(hg_cols(2))
    lbb = lower_bound(lbb_ref)
    fb_ref[...] = lbb + (1.0 - lbb) * jax.nn.sigmoid(hg_cols(3))
    gate_ref[...] = hg_cols(4)

    hr = x_ref.shape[0] // 2
    pos_t = pos_ref[...].astype(F32).T
    pos = jnp.concatenate([pos_t[:, j:j + 1] for j in range(pos_t.shape[1])], axis=0)
    lane = lax.broadcasted_iota(jnp.int32, (hr, LANES), 1)
    ang = jnp.where(lane < MLA_ROPE, pos[0:hr, :], pos[hr:, :]) * invf_ref[...]
    cs, sn = jnp.cos(ang), jnp.sin(ang)
    sgn = sgn_ref[...]
    keep = jnp.abs(sgn)
    cos_t = jnp.concatenate([cs * keep, pltpu.roll(cs, MLA_ROPE, 1) * keep], axis=0)
    sin_t = jnp.concatenate([sn * sgn, pltpu.roll(sn, MLA_ROPE, 1) * sgn], axis=0)

    def rope(grp):
        return grp * cos_t + pltpu.roll(grp, MLA_ROPE, 1) * sin_t

    lat = _dot(h, w_lat_ref[...])
    qa = _rms(lat[:, :MLA_Q_RANK], qan_ref[...]).astype(BF16)
    kva = _rms(lat[:, MLA_Q_RANK:MLA_Q_RANK + MLA_KV_RANK], kvn_ref[...]).astype(BF16)
    k_pe = rope(lat[:, MLA_Q_RANK + MLA_KV_RANK:]).astype(BF16)

    scale = (MLA_NOPE + MLA_ROPE) ** -0.5 * LOG2E
    qh = _dot(qa, wqb_ref[...])
    kvh = _dot(kva, wkvb_ref[...])
    for hd in range(MLA_HEADS):
        qa_ref[0, hd, :, 0:LANES] = (qh[:, hd * LANES:(hd + 1) * LANES] * scale).astype(BF16)
        pe = rope(qh[:, (MLA_HEADS + hd) * LANES:(MLA_HEADS + hd + 1) * LANES])
        qa_ref[0, hd, :, LANES:2 * LANES] = (pe * scale).astype(BF16)
        ka_ref[0, hd, :, 0:LANES] = kvh[:, 2 * hd * LANES:(2 * hd + 1) * LANES].astype(BF16)
        ka_ref[0, hd, :, LANES:2 * LANES] = k_pe
        va_ref[0, hd, :, :] = kvh[:, (2 * hd + 1) * LANES:(2 * hd + 2) * LANES].astype(BF16)


def _proj(x2, pos2, ln_g, ln_b, w_hg, w_lat, lb_f, lb_b, qan, wqb, kvn, wkvb, invf, sgn, B, S):
    N = B * S
    tm = PROJ_TM
    spb = S // tm
    row = lambda i: (i, 0)
    const = lambda i: (0, 0)
    head = lambda i: (i // spb, 0, i % spb, 0)
    full = lambda a: pl.BlockSpec(a.shape, const)
    out_shape = (
        jax.ShapeDtypeStruct((N, HG_W), F32),
        jax.ShapeDtypeStruct((N, HG_W), F32),
        jax.ShapeDtypeStruct((N, HG_W), F32),
        jax.ShapeDtypeStruct((N, HG_W), F32),
        jax.ShapeDtypeStruct((N, HG_W), BF16),
        jax.ShapeDtypeStruct((B, MLA_HEADS, S, MLA_QK_PAD), BF16),
        jax.ShapeDtypeStruct((B, MLA_HEADS, S, MLA_QK_PAD), BF16),
        jax.ShapeDtypeStruct((B, MLA_HEADS, S, MLA_VHEAD), BF16),
    )
    tok = pl.BlockSpec((tm, HG_W), row)
    return pl.pallas_call(
        _proj_kernel,
        grid=(N // tm,),
        in_specs=[pl.BlockSpec((tm, D_MODEL), row), pl.BlockSpec((tm // LANES, LANES), row),
                  full(ln_g), full(ln_b), full(w_hg), full(w_lat), full(lb_f), full(lb_b),
                  full(qan), full(wqb), full(kvn), full(wkvb), full(invf), full(sgn)],
        out_specs=(tok, tok, tok, tok, tok,
                   pl.BlockSpec((1, MLA_HEADS, tm, MLA_QK_PAD), head),
                   pl.BlockSpec((1, MLA_HEADS, tm, MLA_QK_PAD), head),
                   pl.BlockSpec((1, MLA_HEADS, tm, MLA_VHEAD), head)),
        out_shape=out_shape,
        compiler_params=pltpu.CompilerParams(dimension_semantics=("arbitrary",),
                                             vmem_limit_bytes=VMEM_LIMIT),
        name="proj",
    )(x2, pos2, ln_g, ln_b, w_hg, w_lat, lb_f, lb_b, qan, wqb, kvn, wkvb, invf, sgn)


def _blocks(x):
    return [x[i * SUBLANES:(i + 1) * SUBLANES] for i in range(x.shape[0] // SUBLANES)]


def _rows(blocks):
    return jnp.concatenate(blocks, axis=0)


def _hgrn_chunk(q, f, v, lmap_ref, rev):
    C = HG_CHUNK
    g = jnp.log(f) * LOG2E
    k = 1.0 - f
    row = lax.broadcasted_iota(jnp.int32, (C, HG_DIM), 0)

    def roll8(x, shift):
        return _rows([pltpu.roll(blk, shift, 0) for blk in _blocks(x)])

    P, T = g, g
    A = None
    for lvl in range(3):
        b = 1 << lvl
        bit = ((row >> lvl) & 1) == 1
        later = jnp.logical_not(bit) if rev else bit
        w = jnp.exp2(jnp.where(later, P, T - P))
        a_l = _dot_nt((q * w).astype(BF16), (k * w).astype(BF16))
        hit = lmap_ref[...] == lvl
        A = jnp.where(hit, a_l, 0.0) if A is None else jnp.where(hit, a_l, A)
        sib = roll8(T, b) if 2 * b == SUBLANES else jnp.where(bit, roll8(T, b), roll8(T, SUBLANES - b))
        P = P + jnp.where(later, sib, 0.0)
        T = T + sib

    Pb, Tb, qb, kb, Ab = _blocks(P), _blocks(T), _blocks(q), _blocks(k), _blocks(A)
    zero = jnp.zeros((SUBLANES, HG_DIM), F32)
    lvl, m = 3, 1
    while m < HG_BLOCKS:
        later = [(((i // m) & 1) == 1) != rev for i in range(HG_BLOCKS)]
        wb = [jnp.exp2(Pb[i] if later[i] else Tb[i] - Pb[i]) for i in range(HG_BLOCKS)]
        qt = _rows([qb[i] * wb[i] for i in range(HG_BLOCKS) if later[i]]).astype(BF16)
        kt = _rows([zero if later[i] else kb[i] * wb[i] for i in range(HG_BLOCKS)]).astype(BF16)
        a_l = _blocks(_dot_nt(qt, kt))
        n_later = 0
        for i in range(HG_BLOCKS):
            if later[i]:
                hit = lmap_ref[i * SUBLANES:(i + 1) * SUBLANES, :] == lvl
                Ab[i] = jnp.where(hit, a_l[n_later], Ab[i])
                n_later += 1
        sib = [Tb[i ^ m] for i in range(HG_BLOCKS)]
        Pb = [Pb[i] + sib[i] if later[i] else Pb[i] for i in range(HG_BLOCKS)]
        tot = {}
        for i in range(HG_BLOCKS):
            key = i // (2 * m)
            if key not in tot:
                tot[key] = Tb[i] + sib[i]
        Tb = [tot[i // (2 * m)] for i in range(HG_BLOCKS)]
        lvl, m = lvl + 1, 2 * m

    vf = v.astype(F32)
    o = _dot(_rows(Ab).astype(BF16), v) + jnp.sum(q * k, axis=1, keepdims=True) * vf
    qt = _rows([qb[i] * jnp.exp2(Pb[i]) for i in range(HG_BLOCKS)]).astype(BF16)
    kt = _rows([kb[i] * jnp.exp2(Tb[i] - Pb[i]) for i in range(HG_BLOCKS)]).astype(BF16)
    return o, qt, kt, jnp.exp2(Tb[0])


def _hgrn_kernel(q_ref, ff_ref, fb_ref, v_ref, gate_ref, ng_ref, out_ref, o_acc, lmap_s):
    S = q_ref.shape[0]
    C = HG_CHUNK
    n = S // C

    ri = lax.broadcasted_iota(jnp.int32, (C, C), 0)
    ci = lax.broadcasted_iota(jnp.int32, (C, C), 1)
    x = ri ^ ci
    level = jnp.full((C, C), -1, jnp.int32)
    b = 1
    while b < C:
        level = level + jnp.where(x >= b, 1, 0)
        b *= 2
    lmap_s[0] = jnp.where(ci < ri, level, -1)
    lmap_s[1] = jnp.where(ci > ri, level, -1)

    def visit(c, f_ref, st, rev, first):
        r = pl.multiple_of(c * C, C)
        q = q_ref[pl.ds(r, C), :]
        v = v_ref[pl.ds(r, C), :]
        o, qt, kt, d = _hgrn_chunk(q, f_ref[pl.ds(r, C), :], v, lmap_s.at[1 if rev else 0], rev)
        o = o + _dot_nt(qt, st.astype(BF16))
        st = st * d[0:1, :] + _dot(v.astype(F32).T.astype(BF16), kt)
        if first:
            o_acc[pl.ds(r, C), :] = o
        else:
            o = _rms(o + o_acc[pl.ds(r, C), :], ng_ref[...])
            out_ref[pl.ds(r, C), :] = (o * jax.nn.silu(gate_ref[pl.ds(r, C), :])).astype(BF16)
        return st

    def step(first):
        def body(i, carry):
            st_f, st_b = carry
            return (visit(i, ff_ref, st_f, False, first),
                    visit(n - 1 - i, fb_ref, st_b, True, first))
        return body

    z = jnp.zeros((HG_DIM, HG_DIM), F32)
    carry = lax.fori_loop(0, n // 2, step(True), (z, z), unroll=HG_UNROLL)
    lax.fori_loop(n // 2, n, step(False), carry, unroll=HG_UNROLL)


def _hgrn(q, ff, fb, v, gate, ng, B, S):
    N = B * S
    C = HG_CHUNK
    assert (S // C) % 2 == 0
    blk = pl.BlockSpec((S, HG_DIM), lambda b, h: (b, h))
    return pl.pallas_call(
        _hgrn_kernel,
        grid=(B, HG_HEADS),
        in_specs=[blk, blk, blk, blk, blk, pl.BlockSpec((1, HG_DIM), lambda b, h: (0, 0))],
        out_specs=blk,
        out_shape=jax.ShapeDtypeStruct((N, HG_W), BF16),
        scratch_shapes=[pltpu.VMEM((S, HG_DIM), F32),
                        pltpu.VMEM((2, C, C), jnp.int32)],
        compiler_params=pltpu.CompilerParams(dimension_semantics=("arbitrary", "arbitrary"),
                                             vmem_limit_bytes=VMEM_LIMIT),
        name="hgrn",
    )(q, ff, fb, v, gate, ng)


def _attn_kernel(q_ref, k_ref, v_ref, ng_ref, out_ref, s_s, p_s, l_s):
    S = k_ref.shape[2]
    tq = ATT_TQ
    nq = S // tq

    def scores(i, slot):
        s_s[slot] = _dot_nt(q_ref[0, 0, i * tq:(i + 1) * tq, :], k_ref[0, 0])

    def softmax(slot):
        for r in range(0, tq, ATT_SUB):
            s = s_s[slot, r:r + ATT_SUB, :]
            p = jnp.exp2(s - jnp.max(s, axis=-1, keepdims=True))
            l_s[slot, r:r + ATT_SUB, :] = jnp.sum(p, axis=-1, keepdims=True)
            p_s[slot, r:r + ATT_SUB, :] = p.astype(BF16)

    def values(i, slot):
        o = _dot(p_s[slot], v_ref[0, 0]) / l_s[slot]
        out_ref[i * tq:(i + 1) * tq, :] = _rms(o, ng_ref[...]).astype(BF16)

    scores(0, 0)
    for i in range(nq):
        if i + 1 < nq:
            scores(i + 1, (i + 1) % 2)
        softmax(i % 2)
        values(i, i % 2)


def _attn(qa, ka, va, ng, B, S):
    tq = ATT_TQ
    return pl.pallas_call(
        _attn_kernel,
        grid=(B, MLA_HEADS),
        in_specs=[pl.BlockSpec((1, 1, S, MLA_QK_PAD), lambda b, h: (b, h, 0, 0)),
                  pl.BlockSpec((1, 1, S, MLA_QK_PAD), lambda b, h: (b, h, 0, 0)),
                  pl.BlockSpec((1, 1, S, MLA_VHEAD), lambda b, h: (b, h, 0, 0)),
                  pl.BlockSpec((1, MLA_VHEAD), lambda b, h: (0, 0))],
        out_specs=pl.BlockSpec((S, MLA_VHEAD), lambda b, h: (b, h)),
        out_shape=jax.ShapeDtypeStruct((B * S, MLA_HEADS * MLA_VHEAD), BF16),
        scratch_shapes=[pltpu.VMEM((2, tq, S), F32),
                        pltpu.VMEM((2, tq, S), BF16),
                        pltpu.VMEM((2, tq, 1), F32)],
        compiler_params=pltpu.CompilerParams(
            dimension_semantics=("arbitrary", "arbitrary"),
            vmem_limit_bytes=VMEM_LIMIT),
        name="attn",
    )(qa, ka, va, ng)


def _ffn_kernel(x_ref, xp_ref, xn_ref, hg_ref, hgp_ref, hgn_ref, at_ref, atp_ref, atn_ref,
                lng_ref, lnb_ref, wo_ref, g1_ref, b1_ref,
                wu_ref, cw_ref, cb_ref, wd_ref, g2_ref, b2_ref, out_ref,
                h1_s, il_s, ext_s, u_s, act_s, acc_s, *, tiles_per_seq):
    tm = FFN_TM
    hrows = 2 * SUBLANES
    grp = tm // SUBLANES
    n = D_FF // FFN_FC
    i = pl.program_id(0)
    has_prev = jnp.where(i % tiles_per_seq != 0, 1.0, 0.0)
    has_next = jnp.where(i % tiles_per_seq != tiles_per_seq - 1, 1.0, 0.0)

    def with_halo(main, before, after):
        sub = lax.broadcasted_iota(jnp.int32, (hrows, main.shape[1]), 0)
        halo = jnp.where(sub == 0, before.astype(F32), jnp.where(sub == 1, after.astype(F32), 0.0))
        return jnp.concatenate([main, halo.astype(main.dtype)], axis=0)

    def token_mix(xx, hgx, atx):
        mix = _dot(hgx, wo_ref[0:HG_W, :]) + _dot(atx, wo_ref[HG_W:, :])
        return _ln(DN_ALPHA * _ln(xx, lng_ref[...], lnb_ref[...]) + mix, g1_ref[...], b1_ref[...])

    h1x = token_mix(
        with_halo(x_ref[...], xp_ref[SUBLANES - 1:SUBLANES, :], xn_ref[0:1, :]),
        with_halo(hg_ref[...], hgp_ref[...].astype(F32)[hrows - 1:hrows, :],
                  hgn_ref[...].astype(F32)[0:1, :]),
        with_halo(at_ref[...], atp_ref[...].astype(F32)[hrows - 1:hrows, :],
                  atn_ref[...].astype(F32)[0:1, :]))
    h1_s[...] = h1x[0:tm, :]
    sub = lax.broadcasted_iota(jnp.int32, (hrows, D_MODEL), 0)
    valid = jnp.where(sub == 0, has_prev, jnp.where(sub == 1, has_next, 0.0))
    ext_s[tm:, :] = (h1x[tm:, :] * valid).astype(BF16)

    ngrp = D_MODEL // LANES
    for c in range(ngrp):
        for a in range(SUBLANES):
            il_s[c, pl.ds(a, grp, stride=SUBLANES), :] = h1_s[a * grp:(a + 1) * grp,
                                                              c * LANES:(c + 1) * LANES]
        ext_s[0:tm, c * LANES:(c + 1) * LANES] = il_s[c].astype(BF16)
    acc_s[...] = jnp.zeros_like(acc_s)

    half = tm // 2

    def chunk_cols(j, part):
        start = part * D_FF + j * FFN_FC
        return pl.ds(start if isinstance(start, int) else pl.multiple_of(start, FFN_FC), FFN_FC)

    def up(j, slot, part):
        u_s[slot, :, part * FFN_FC:(part + 1) * FFN_FC] = _dot(ext_s[...],
                                                               wu_ref[:, chunk_cols(j, part)])

    def glu(j, slot, part):
        r0, r1 = part * half, (part + 1) * half
        cw = jnp.concatenate([cw_ref[:, chunk_cols(j, 0)], cw_ref[:, chunk_cols(j, 1)]], axis=1)
        cb = jnp.concatenate([cb_ref[:, chunk_cols(j, 0)], cb_ref[:, chunk_cols(j, 1)]], axis=1)
        sub8 = lax.broadcasted_iota(jnp.int32, (SUBLANES, 2 * FFN_FC), 0)
        if part == 0:
            first = jnp.where(sub8 == 0, u_s[slot, tm:tm + 1, :],
                              pltpu.roll(u_s[slot, tm - SUBLANES:tm, :], 1, 0))
            prev = jnp.concatenate([first, u_s[slot, 0:r1 - SUBLANES, :]], axis=0)
            nxt = u_s[slot, SUBLANES:r1 + SUBLANES, :]
        else:
            last = jnp.where(sub8 == SUBLANES - 1, u_s[slot, tm + 1:tm + 2, :],
                             pltpu.roll(u_s[slot, 0:SUBLANES, :], SUBLANES - 1, 0))
            prev = u_s[slot, r0 - SUBLANES:r1 - SUBLANES, :]
            nxt = jnp.concatenate([u_s[slot, r0 + SUBLANES:tm, :], last], axis=0)
        c = prev * cw[0:1, :] + u_s[slot, r0:r1, :] * cw[1:2, :] + nxt * cw[2:3, :] + cb
        act_s[slot, r0:r1, :] = (_gelu_tanh(c[:, :FFN_FC]) * c[:, FFN_FC:]).astype(BF16)

    def down(j, slot, part):
        c0 = part * 2 * LANES
        k0 = j * FFN_FC
        rows = pl.ds(k0 if isinstance(k0, int) else pl.multiple_of(k0, FFN_FC), FFN_FC)
        res = _dot(act_s[slot], wd_ref[rows, c0:c0 + 2 * LANES])
        acc_s[2 * part] += res[:, :LANES]
        acc_s[2 * part + 1] += res[:, LANES:]

    def stage(t, par):
        static = isinstance(t, int)
        do_up = not static or t + 1 < n
        do_glu = not static or 0 <= t < n
        do_down = not static or 1 <= t <= n
        for part in range(2):
            if do_up:
                up(t + 1, 1 - par, part)
            if do_glu:
                glu(t, par, part)
            if do_down:
                down(t - 1, 1 - par, 2 * part)
                down(t - 1, 1 - par, 2 * part + 1)

    stage(-1, 1)
    stage(0, 0)
    npairs = (n - 2) // 2

    def pair(jj, carry):
        t = 2 * jj + 1
        stage(t, 1)
        stage(t + 1, 0)
        return carry

    lax.fori_loop(0, npairs, pair, 0, unroll=True)
    for t in range(2 * npairs + 1, n + 1):
        stage(t, t % 2)

    for a in range(SUBLANES):
        ffn = jnp.concatenate([acc_s[c, pl.ds(a, grp, stride=SUBLANES), :] for c in range(ngrp)],
                              axis=1)
        rows = slice(a * grp, (a + 1) * grp)
        out_ref[rows, :] = _ln(DN_ALPHA * h1_s[rows, :] + ffn, g2_ref[...], b2_ref[...])


def _ffn(x2, hg, at, ln_g, ln_b, wo, g1, b1, wu, cw, cb, wd, g2, b2, S):
    N = x2.shape[0]
    tm = FFN_TM
    hrows = 2 * SUBLANES
    row = lambda i: (i, 0)
    full = lambda a: pl.BlockSpec(a.shape, lambda i: (0,) * a.ndim)

    def halo_specs(rows, width):
        per_tile = tm // rows
        last = N // rows - 1
        return [pl.BlockSpec((rows, width), lambda i: (jnp.maximum(i * per_tile - 1, 0), 0)),
                pl.BlockSpec((rows, width), lambda i: (jnp.minimum((i + 1) * per_tile, last), 0))]

    return pl.pallas_call(
        functools.partial(_ffn_kernel, tiles_per_seq=S // tm),
        grid=(N // tm,),
        in_specs=([pl.BlockSpec((tm, D_MODEL), row)] + halo_specs(SUBLANES, D_MODEL)
                  + [pl.BlockSpec((tm, HG_W), row)] + halo_specs(hrows, HG_W)
                  + [pl.BlockSpec((tm, HG_W), row)] + halo_specs(hrows, HG_W)
                  + [full(a) for a in (ln_g, ln_b, wo, g1, b1, wu, cw, cb, wd, g2, b2)]),
        out_specs=pl.BlockSpec((tm, D_MODEL), row),
        out_shape=jax.ShapeDtypeStruct((N, D_MODEL), F32),
        scratch_shapes=[pltpu.VMEM((tm, D_MODEL), F32),
                        pltpu.VMEM((D_MODEL // LANES, tm, LANES), F32),
                        pltpu.VMEM((tm + hrows, D_MODEL), BF16),
                        pltpu.VMEM((2, tm + hrows, 2 * FFN_FC), F32),
                        pltpu.VMEM((2, tm, FFN_FC), BF16),
                        pltpu.VMEM((D_MODEL // LANES, tm, LANES), F32)],
        compiler_params=pltpu.CompilerParams(dimension_semantics=("arbitrary",),
                                             vmem_limit_bytes=VMEM_LIMIT),
        name="ffn",
    )(x2, x2, x2, hg, hg, hg, at, at, at, ln_g, ln_b, wo, g1, b1, wu, cw, cb, wd, g2, b2)


def _swap_halves(w):
    half = MLA_ROPE // 2
    return jnp.concatenate([w[..., half:], w[..., :half]], axis=-1)


def kernel(x, positions, ln_in_g, ln_in_b, w_in, lb_fwd, lb_bwd, hg_norm_g, q_a_norm_g, w_q_b,
           kv_a_norm_g, w_kv_b, attn_norm_g, w_out, ln1_g, ln1_b, w_up, conv_w, conv_b, w_down,
           ln2_g, ln2_b):
    B, S, D = x.shape
    assert D == D_MODEL and w_in.shape[0] == 1 and lb_fwd.shape == (2, HG_W)
    assert S % HG_CHUNK == 0 and S % ATT_TQ == 0 and S % FFN_TM == 0 and D_FF % FFN_FC == 0
    N = B * S
    r1 = lambda a: a.reshape(1, -1)

    w_in0 = w_in[0]
    c_lat = 5 * HG_W
    c_kr = c_lat + MLA_Q_RANK + MLA_KV_RANK
    w_hg = w_in0[:, :c_lat].astype(BF16)
    w_kr = w_in0[:, c_kr:]
    w_lat = jnp.concatenate([w_in0[:, c_lat:c_kr], w_kr, _swap_halves(w_kr)], axis=1).astype(BF16)
    wq = w_q_b[0].reshape(MLA_Q_RANK, MLA_HEADS, MLA_NOPE + MLA_ROPE)
    wq_pe = wq[:, :, MLA_NOPE:]
    wqb = jnp.concatenate(
        [wq[:, :, :MLA_NOPE].reshape(MLA_Q_RANK, -1),
         jnp.concatenate([wq_pe, _swap_halves(wq_pe)], axis=-1).reshape(MLA_Q_RANK, -1)],
        axis=1).astype(BF16)
    wkvb = w_kv_b[0].astype(BF16)
    half = MLA_ROPE // 2
    inv_freq = 1.0 / (ROPE_THETA ** (jnp.arange(half, dtype=F32) / half))
    zeros = jnp.zeros((LANES - MLA_ROPE,), F32)
    invf = r1(jnp.tile(inv_freq, LANES // half))
    sgn = r1(jnp.concatenate([-jnp.ones((half,), F32), jnp.ones((half,), F32), zeros]))
    wu = w_up[0].astype(BF16)
    cw = conv_w[0]
    cb = r1(conv_b[0])
    wd = w_down[0].astype(BF16)

    x2 = x.reshape(N, D)
    pos2 = positions.reshape(N // LANES, LANES)
    q, ff, fb, gate, v, qa, ka, va = _proj(
        x2, pos2, r1(ln_in_g), r1(ln_in_b), w_hg, w_lat, lb_fwd, lb_bwd,
        r1(q_a_norm_g[0]), wqb, r1(kv_a_norm_g[0]), wkvb, invf, sgn, B, S)
    hg = _hgrn(q, ff, fb, v, gate, r1(hg_norm_g[0]), B, S)
    at = _attn(qa, ka, va, r1(attn_norm_g[0]), B, S)
    out = _ffn(x2, hg, at, r1(ln_in_g), r1(ln_in_b), w_out[0].astype(BF16), r1(ln1_g[0]), r1(ln1_b[0]),
               wu, cw, cb, wd, r1(ln2_g[0]), r1(ln2_b[0]), S)
    return out.reshape(B, S, D)
```

```python
import functools

import jax
import jax.numpy as jnp
from jax import lax
from jax.experimental import pallas as pl
from jax.experimental.pallas import tpu as pltpu

D_MODEL = 1024
HG_HEADS = 4
HG_DIM = 128
HG_W = HG_HEADS * HG_DIM
MLA_HEADS = 4
MLA_Q_RANK = 384
MLA_KV_RANK = 128
MLA_NOPE = 128
MLA_ROPE = 64
MLA_VHEAD = 128
MLA_QK_PAD = 256
ROPE_THETA = 10000.0
D_FF = 2816
DN_ALPHA = 2.0 ** 0.25
NORM_EPS = 1e-5
LOG2E = 1.4426950408889634

LANES = 128
SUBLANES = 8
VMEM_LIMIT = 56 * 1024 * 1024

PROJ_TM = 1024
HG_CHUNK = 256
HG_BLOCKS = HG_CHUNK // SUBLANES
HG_UNROLL = 8
ATT_TQ = 256
ATT_SUB = 32
ATT_HEADS = 2
FFN_TM = 512
FFN_FC = 256

F32 = jnp.float32
BF16 = jnp.bfloat16


def _ln(x, g, b):
    mu = jnp.mean(x, axis=-1, keepdims=True)
    xc = x - mu
    var = jnp.mean(xc * xc, axis=-1, keepdims=True)
    return xc * lax.rsqrt(var + NORM_EPS) * g + b


def _rms(x, g):
    return x * lax.rsqrt(jnp.mean(x * x, axis=-1, keepdims=True) + NORM_EPS) * g


def _gelu_tanh(x):
    c1 = -2.0 * (2.0 / jnp.pi) ** 0.5 * LOG2E
    z = x * (c1 + (c1 * 0.044715) * (x * x))
    return x / (1.0 + jnp.exp2(z))


def _dot(a, b):
    return jnp.dot(a, b, preferred_element_type=F32)


def _dot_nt(a, b):
    return lax.dot_general(a, b, (((1,), (1,)), ((), ())), preferred_element_type=F32)


def _proj_kernel(x_ref, pos_ref, lng_ref, lnb_ref, w_hg_ref, w_lat_ref, lbf_ref, lbb_ref,
                 qan_ref, wqb_ref, kvn_ref, wkvb_ref, invf_ref, sgn_ref,
                 q_ref, ff_ref, fb_ref, gate_ref, v_ref, qa_ref, ka_ref, va_ref):
    h = _ln(x_ref[...], lng_ref[...], lnb_ref[...]).astype(BF16)

    def lower_bound(ref):
        r0, r1 = ref[0:1, :], ref[1:2, :]
        m = jnp.maximum(r0, r1)
        e0, e1 = jnp.exp(r0 - m), jnp.exp(r1 - m)
        return e0 / (e0 + e1)

    def hg_cols(j):
        return _dot(h, w_hg_ref[:, j * HG_W:(j + 1) * HG_W])

    q_ref[...] = jax.nn.silu(hg_cols(0))
    v_ref[...] = hg_cols(1).astype(BF16)
    lbf = lower_bound(lbf_ref)
    ff_ref[...] = lbf + (1.0 - lbf) * jax.nn.sigmoid(hg_cols(2))
    lbb = lower_bound(lbb_ref)
    fb_ref[...] = lbb + (1.0 - lbb) * jax.nn.sigmoid(hg_cols(3))
    gate_ref[...] = hg_cols(4)

    hr = x_ref.shape[0] // 2
    pos_t = pos_ref[...].astype(F32).T
    pos = jnp.concatenate([pos_t[:, j:j + 1] for j in range(pos_t.shape[1])], axis=0)
    lane = lax.broadcasted_iota(jnp.int32, (hr, LANES), 1)
    ang = jnp.where(lane < MLA_ROPE, pos[0:hr, :], pos[hr:, :]) * invf_ref[...]
    cs, sn = jnp.cos(ang), jnp.sin(ang)
    sgn = sgn_ref[...]
    keep = jnp.abs(sgn)
    cos_t = jnp.concatenate([cs * keep, pltpu.roll(cs, MLA_ROPE, 1) * keep], axis=0)
    sin_t = jnp.concatenate([sn * sgn, pltpu.roll(sn, MLA_ROPE, 1) * sgn], axis=0)

    def rope(grp):
        return grp * cos_t + pltpu.roll(grp, MLA_ROPE, 1) * sin_t

    lat = _dot(h, w_lat_ref[...])
    qa = _rms(lat[:, :MLA_Q_RANK], qan_ref[...]).astype(BF16)
    kva = _rms(lat[:, MLA_Q_RANK:MLA_Q_RANK + MLA_KV_RANK], kvn_ref[...]).astype(BF16)
    k_pe = rope(lat[:, MLA_Q_RANK + MLA_KV_RANK:]).astype(BF16)

    scale = (MLA_NOPE + MLA_ROPE) ** -0.5 * LOG2E
    qh = _dot(qa, wqb_ref[...])
    kvh = _dot(kva, wkvb_ref[...])
    for hd in range(MLA_HEADS):
        qa_ref[0, hd, :, 0:LANES] = (qh[:, hd * LANES:(hd + 1) * LANES] * scale).astype(BF16)
        pe = rope(qh[:, (MLA_HEADS + hd) * LANES:(MLA_HEADS + hd + 1) * LANES])
        qa_ref[0, hd, :, LANES:2 * LANES] = (pe * scale).astype(BF16)
        ka_ref[0, hd, :, 0:LANES] = kvh[:, 2 * hd * LANES:(2 * hd + 1) * LANES].astype(BF16)
        ka_ref[0, hd, :, LANES:2 * LANES] = k_pe
        va_ref[0, hd, :, :] = kvh[:, (2 * hd + 1) * LANES:(2 * hd + 2) * LANES].astype(BF16)


def _proj(x2, pos2, ln_g, ln_b, w_hg, w_lat, lb_f, lb_b, qan, wqb, kvn, wkvb, invf, sgn, B, S):
    N = B * S
    tm = PROJ_TM
    spb = S // tm
    row = lambda i: (i, 0)
    const = lambda i: (0, 0)
    head = lambda i: (i // spb, 0, i % spb, 0)
    full = lambda a: pl.BlockSpec(a.shape, const)
    out_shape = (
        jax.ShapeDtypeStruct((N, HG_W), F32),
        jax.ShapeDtypeStruct((N, HG_W), F32),
        jax.ShapeDtypeStruct((N, HG_W), F32),
        jax.ShapeDtypeStruct((N, HG_W), F32),
        jax.ShapeDtypeStruct((N, HG_W), BF16),
        jax.ShapeDtypeStruct((B, MLA_HEADS, S, MLA_QK_PAD), BF16),
        jax.ShapeDtypeStruct((B, MLA_HEADS, S, MLA_QK_PAD), BF16),
        jax.ShapeDtypeStruct((B, MLA_HEADS, S, MLA_VHEAD), BF16),
    )
    tok = pl.BlockSpec((tm, HG_W), row)
    return pl.pallas_call(
        _proj_kernel,
        grid=(N // tm,),
        in_specs=[pl.BlockSpec((tm, D_MODEL), row), pl.BlockSpec((tm // LANES, LANES), row),
                  full(ln_g), full(ln_b), full(w_hg), full(w_lat), full(lb_f), full(lb_b),
                  full(qan), full(wqb), full(kvn), full(wkvb), full(invf), full(sgn)],
        out_specs=(tok, tok, tok, tok, tok,
                   pl.BlockSpec((1, MLA_HEADS, tm, MLA_QK_PAD), head),
                   pl.BlockSpec((1, MLA_HEADS, tm, MLA_QK_PAD), head),
                   pl.BlockSpec((1, MLA_HEADS, tm, MLA_VHEAD), head)),
        out_shape=out_shape,
        compiler_params=pltpu.CompilerParams(dimension_semantics=("arbitrary",),
                                             vmem_limit_bytes=VMEM_LIMIT),
        name="proj",
    )(x2, pos2, ln_g, ln_b, w_hg, w_lat, lb_f, lb_b, qan, wqb, kvn, wkvb, invf, sgn)


def _blocks(x):
    return [x[i * SUBLANES:(i + 1) * SUBLANES] for i in range(x.shape[0] // SUBLANES)]


def _rows(blocks):
    return jnp.concatenate(blocks, axis=0)


def _hgrn_chunk(q, f, v, lmap_ref, rev):
    C = HG_CHUNK
    g = jnp.log(f) * LOG2E
    k = 1.0 - f
    row = lax.broadcasted_iota(jnp.int32, (C, HG_DIM), 0)

    def roll8(x, shift):
        return _rows([pltpu.roll(blk, shift, 0) for blk in _blocks(x)])

    P, T = g, g
    A = None
    for lvl in range(3):
        b = 1 << lvl
        bit = ((row >> lvl) & 1) == 1
        later = jnp.logical_not(bit) if rev else bit
        w = jnp.exp2(jnp.where(later, P, T - P))
        a_l = _dot_nt((q * w).astype(BF16), (k * w).astype(BF16))
        hit = lmap_ref[...] == lvl
        A = jnp.where(hit, a_l, 0.0) if A is None else jnp.where(hit, a_l, A)
        sib = roll8(T, b) if 2 * b == SUBLANES else jnp.where(bit, roll8(T, b), roll8(T, SUBLANES - b))
        P = P + jnp.where(later, sib, 0.0)
        T = T + sib

    Pb, Tb, qb, kb, Ab = _blocks(P), _blocks(T), _blocks(q), _blocks(k), _blocks(A)
    zero = jnp.zeros((SUBLANES, HG_DIM), F32)
    lvl, m = 3, 1
    while m < HG_BLOCKS:
        later = [(((i // m) & 1) == 1) != rev for i in range(HG_BLOCKS)]
        wb = [jnp.exp2(Pb[i] if later[i] else Tb[i] - Pb[i]) for i in range(HG_BLOCKS)]
        qt = _rows([qb[i] * wb[i] for i in range(HG_BLOCKS) if later[i]]).astype(BF16)
        kt = _rows([zero if later[i] else kb[i] * wb[i] for i in range(HG_BLOCKS)]).astype(BF16)
        a_l = _blocks(_dot_nt(qt, kt))
        n_later = 0
        for i in range(HG_BLOCKS):
            if later[i]:
                hit = lmap_ref[i * SUBLANES:(i + 1) * SUBLANES, :] == lvl
                Ab[i] = jnp.where(hit, a_l[n_later], Ab[i])
                n_later += 1
        sib = [Tb[i ^ m] for i in range(HG_BLOCKS)]
        Pb = [Pb[i] + sib[i] if later[i] else Pb[i] for i in range(HG_BLOCKS)]
        tot = {}
        for i in range(HG_BLOCKS):
            key = i // (2 * m)
            if key not in tot:
                tot[key] = Tb[i] + sib[i]
        Tb = [tot[i // (2 * m)] for i in range(HG_BLOCKS)]
        lvl, m = lvl + 1, 2 * m

    vf = v.astype(F32)
    o = _dot(_rows(Ab).astype(BF16), v) + jnp.sum(q * k, axis=1, keepdims=True) * vf
    qt = _rows([qb[i] * jnp.exp2(Pb[i]) for i in range(HG_BLOCKS)]).astype(BF16)
    kt = _rows([kb[i] * jnp.exp2(Tb[i] - Pb[i]) for i in range(HG_BLOCKS)]).astype(BF16)
    return o, qt, kt, jnp.exp2(Tb[0])


def _hgrn_kernel(q_ref, ff_ref, fb_ref, v_ref, gate_ref, ng_ref, out_ref, o_acc, lmap_s):
    S = q_ref.shape[0]
    C = HG_CHUNK
    n = S // C

    ri = lax.broadcasted_iota(jnp.int32, (C, C), 0)
    ci = lax.broadcasted_iota(jnp.int32, (C, C), 1)
    x = ri ^ ci
    level = jnp.full((C, C), -1, jnp.int32)
    b = 1
    while b < C:
        level = level + jnp.where(x >= b, 1, 0)
        b *= 2
    lmap_s[0] = jnp.where(ci < ri, level, -1)
    lmap_s[1] = jnp.where(ci > ri, level, -1)

    def visit(c, f_ref, st, rev, first):
        r = pl.multiple_of(c * C, C)
        q = q_ref[pl.ds(r, C), :]
        v = v_ref[pl.ds(r, C), :]
        o, qt, kt, d = _hgrn_chunk(q, f_ref[pl.ds(r, C), :], v, lmap_s.at[1 if rev else 0], rev)
        o = o + _dot_nt(qt, st.astype(BF16))
        st = st * d[0:1, :] + _dot(v.astype(F32).T.astype(BF16), kt)
        if first:
            o_acc[pl.ds(r, C), :] = o
        else:
            o = _rms(o + o_acc[pl.ds(r, C), :], ng_ref[...])
            out_ref[pl.ds(r, C), :] = (o * jax.nn.silu(gate_ref[pl.ds(r, C), :])).astype(BF16)
        return st

    def step(first):
        def body(i, carry):
            st_f, st_b = carry
            return (visit(i, ff_ref, st_f, False, first),
                    visit(n - 1 - i, fb_ref, st_b, True, first))
        return body

    z = jnp.zeros((HG_DIM, HG_DIM), F32)
    carry = lax.fori_loop(0, n // 2, step(True), (z, z), unroll=HG_UNROLL)
    lax.fori_loop(n // 2, n, step(False), carry, unroll=HG_UNROLL)


def _hgrn(q, ff, fb, v, gate, ng, B, S):
    N = B * S
    C = HG_CHUNK
    assert (S // C) % 2 == 0
    blk = pl.BlockSpec((S, HG_DIM), lambda b, h: (b, h))
    return pl.pallas_call(
        _hgrn_kernel,
        grid=(B, HG_HEADS),
        in_specs=[blk, blk, blk, blk, blk, pl.BlockSpec((1, HG_DIM), lambda b, h: (0, 0))],
        out_specs=blk,
        out_shape=jax.ShapeDtypeStruct((N, HG_W), BF16),
        scratch_shapes=[pltpu.VMEM((S, HG_DIM), F32),
                        pltpu.VMEM((2, C, C), jnp.int32)],
        compiler_params=pltpu.CompilerParams(dimension_semantics=("arbitrary", "arbitrary"),
                                             vmem_limit_bytes=VMEM_LIMIT),
        name="hgrn",
    )(q, ff, fb, v, gate, ng)


def _attn_kernel(q_ref, k_ref, v_ref, ng_ref, out_ref, s_s, p_s, l_s):
    S = k_ref.shape[2]
    tq = ATT_TQ
    nq = S // tq

    nh = q_ref.shape[1]

    def scores(t, slot):
        h, i = divmod(t, nq)
        s_s[slot] = _dot_nt(q_ref[0, h, i * tq:(i + 1) * tq, :], k_ref[0, h])

    def softmax(slot):
        for r in range(0, tq, ATT_SUB):
            s = s_s[slot, r:r + ATT_SUB, :]
            p = jnp.exp2(s - jnp.max(s, axis=-1, keepdims=True))
            l_s[slot, r:r + ATT_SUB, :] = jnp.sum(p, axis=-1, keepdims=True)
            p_s[slot, r:r + ATT_SUB, :] = p.astype(BF16)

    def values(t, slot):
        h, i = divmod(t, nq)
        o = _dot(p_s[slot], v_ref[0, h]) / l_s[slot]
        out_ref[i * tq:(i + 1) * tq, h * MLA_VHEAD:(h + 1) * MLA_VHEAD] = _rms(
            o, ng_ref[...]).astype(BF16)

    scores(0, 0)
    for t in range(nh * nq):
        if t + 1 < nh * nq:
            scores(t + 1, (t + 1) % 2)
        softmax(t % 2)
        values(t, t % 2)


def _attn(qa, ka, va, ng, B, S):
    tq = ATT_TQ
    nh = ATT_HEADS
    return pl.pallas_call(
        _attn_kernel,
        grid=(B, MLA_HEADS // nh),
        in_specs=[pl.BlockSpec((1, nh, S, MLA_QK_PAD), lambda b, g: (b, g, 0, 0)),
                  pl.BlockSpec((1, nh, S, MLA_QK_PAD), lambda b, g: (b, g, 0, 0)),
                  pl.BlockSpec((1, nh, S, MLA_VHEAD), lambda b, g: (b, g, 0, 0)),
                  pl.BlockSpec((1, MLA_VHEAD), lambda b, g: (0, 0))],
        out_specs=pl.BlockSpec((S, nh * MLA_VHEAD), lambda b, g: (b, g)),
        out_shape=jax.ShapeDtypeStruct((B * S, MLA_HEADS * MLA_VHEAD), BF16),
        scratch_shapes=[pltpu.VMEM((2, tq, S), F32),
                        pltpu.VMEM((2, tq, S), BF16),
                        pltpu.VMEM((2, tq, 1), F32)],
        compiler_params=pltpu.CompilerParams(
            dimension_semantics=("arbitrary", "arbitrary"),
            vmem_limit_bytes=VMEM_LIMIT),
        name="attn",
    )(qa, ka, va, ng)


def _ffn_kernel(x_ref, xp_ref, xn_ref, hg_ref, hgp_ref, hgn_ref, at_ref, atp_ref, atn_ref,
                lng_ref, lnb_ref, wo_ref, g1_ref, b1_ref,
                wu_ref, cw_ref, cb_ref, wd_ref, g2_ref, b2_ref, out_ref,
                h1_s, il_s, ext_s, u_s, act_s, acc_s, *, tiles_per_seq):
    tm = FFN_TM
    hrows = 2 * SUBLANES
    grp = tm // SUBLANES
    n = D_FF // FFN_FC
    i = pl.program_id(0)
    has_prev = jnp.where(i % tiles_per_seq != 0, 1.0, 0.0)
    has_next = jnp.where(i % tiles_per_seq != tiles_per_seq - 1, 1.0, 0.0)

    def with_halo(main, before, after):
        sub = lax.broadcasted_iota(jnp.int32, (hrows, main.shape[1]), 0)
        halo = jnp.where(sub == 0, before.astype(F32), jnp.where(sub == 1, after.astype(F32), 0.0))
        return jnp.concatenate([main, halo.astype(main.dtype)], axis=0)

    def token_mix(xx, hgx, atx):
        mix = _dot(hgx, wo_ref[0:HG_W, :]) + _dot(atx, wo_ref[HG_W:, :])
        return _ln(DN_ALPHA * _ln(xx, lng_ref[...], lnb_ref[...]) + mix, g1_ref[...], b1_ref[...])

    h1x = token_mix(
        with_halo(x_ref[...], xp_ref[SUBLANES - 1:SUBLANES, :], xn_ref[0:1, :]),
        with_halo(hg_ref[...], hgp_ref[...].astype(F32)[hrows - 1:hrows, :],
                  hgn_ref[...].astype(F32)[0:1, :]),
        with_halo(at_ref[...], atp_ref[...].astype(F32)[hrows - 1:hrows, :],
                  atn_ref[...].astype(F32)[0:1, :]))
    h1_s[...] = h1x[0:tm, :]
    sub = lax.broadcasted_iota(jnp.int32, (hrows, D_MODEL), 0)
    valid = jnp.where(sub == 0, has_prev, jnp.where(sub == 1, has_next, 0.0))
    ext_s[tm:, :] = (h1x[tm:, :] * valid).astype(BF16)

    ngrp = D_MODEL // LANES
    for c in range(ngrp):
        for a in range(SUBLANES):
            il_s[c, pl.ds(a, grp, stride=SUBLANES), :] = h1_s[a * grp:(a + 1) * grp,
                                                              c * LANES:(c + 1) * LANES]
        ext_s[0:tm, c * LANES:(c + 1) * LANES] = il_s[c].astype(BF16)
    acc_s[...] = jnp.zeros_like(acc_s)

    half = tm // 2

    def chunk_cols(j, part):
        start = part * D_FF + j * FFN_FC
        return pl.ds(start if isinstance(start, int) else pl.multiple_of(start, FFN_FC), FFN_FC)

    def up(j, slot, part):
        u_s[slot, :, part * FFN_FC:(part + 1) * FFN_FC] = _dot(ext_s[...],
                                                               wu_ref[:, chunk_cols(j, part)])

    def glu(j, slot, part):
        r0, r1 = part * half, (part + 1) * half
        cw = jnp.concatenate([cw_ref[:, chunk_cols(j, 0)], cw_ref[:, chunk_cols(j, 1)]], axis=1)
        cb = jnp.concatenate([cb_ref[:, chunk_cols(j, 0)], cb_ref[:, chunk_cols(j, 1)]], axis=1)
        sub8 = lax.broadcasted_iota(jnp.int32, (SUBLANES, 2 * FFN_FC), 0)
        if part == 0:
            first = jnp.where(sub8 == 0, u_s[slot, tm:tm + 1, :],
                              pltpu.roll(u_s[slot, tm - SUBLANES:tm, :], 1, 0))
            prev = jnp.concatenate([first, u_s[slot, 0:r1 - SUBLANES, :]], axis=0)
            nxt = u_s[slot, SUBLANES:r1 + SUBLANES, :]
        else:
            last = jnp.where(sub8 == SUBLANES - 1, u_s[slot, tm + 1:tm + 2, :],
                             pltpu.roll(u_s[slot, 0:SUBLANES, :], SUBLANES - 1, 0))
            prev = u_s[slot, r0 - SUBLANES:r1 - SUBLANES, :]
            nxt = jnp.concatenate([u_s[slot, r0 + SUBLANES:tm, :], last], axis=0)
        c = prev * cw[0:1, :] + u_s[slot, r0:r1, :] * cw[1:2, :] + nxt * cw[2:3, :] + cb
        act_s[slot, r0:r1, :] = (_gelu_tanh(c[:, :FFN_FC]) * c[:, FFN_FC:]).astype(BF16)

    def down(j, slot, part):
        c0 = part * 2 * LANES
        k0 = j * FFN_FC
        rows = pl.ds(k0 if isinstance(k0, int) else pl.multiple_of(k0, FFN_FC), FFN_FC)
        res = _dot(act_s[slot], wd_ref[rows, c0:c0 + 2 * LANES])
        acc_s[2 * part] += res[:, :LANES]
        acc_s[2 * part + 1] += res[:, LANES:]

    def stage(t, par):
        static = isinstance(t, int)
        do_up = not static or t + 1 < n
        do_glu = not static or 0 <= t < n
        do_down = not static or 1 <= t <= n
        for part in range(2):
            if do_up:
                up(t + 1, 1 - par, part)
            if do_glu:
                glu(t, par, part)
            if do_down:
                down(t - 1, 1 - par, 2 * part)
                down(t - 1, 1 - par, 2 * part + 1)

    stage(-1, 1)
    stage(0, 0)
    npairs = (n - 2) // 2

    def pair(jj, carry):
        t = 2 * jj + 1
        stage(t, 1)
        stage(t + 1, 0)
        return carry

    lax.fori_loop(0, npairs, pair, 0, unroll=True)
    for t in range(2 * npairs + 1, n + 1):
        stage(t, t % 2)

    for a in range(SUBLANES):
        ffn = jnp.concatenate([acc_s[c, pl.ds(a, grp, stride=SUBLANES), :] for c in range(ngrp)],
                              axis=1)
        rows = slice(a * grp, (a + 1) * grp)
        out_ref[rows, :] = _ln(DN_ALPHA * h1_s[rows, :] + ffn, g2_ref[...], b2_ref[...])


def _ffn(x2, hg, at, ln_g, ln_b, wo, g1, b1, wu, cw, cb, wd, g2, b2, S):
    N = x2.shape[0]
    tm = FFN_TM
    hrows = 2 * SUBLANES
    row = lambda i: (i, 0)
    full = lambda a: pl.BlockSpec(a.shape, lambda i: (0,) * a.ndim)

    def halo_specs(rows, width):
        per_tile = tm // rows
        last = N // rows - 1
        return [pl.BlockSpec((rows, width), lambda i: (jnp.maximum(i * per_tile - 1, 0), 0)),
                pl.BlockSpec((rows, width), lambda i: (jnp.minimum((i + 1) * per_tile, last), 0))]

    return pl.pallas_call(
        functools.partial(_ffn_kernel, tiles_per_seq=S // tm),
        grid=(N // tm,),
        in_specs=([pl.BlockSpec((tm, D_MODEL), row)] + halo_specs(SUBLANES, D_MODEL)
                  + [pl.BlockSpec((tm, HG_W), row)] + halo_specs(hrows, HG_W)
                  + [pl.BlockSpec((tm, HG_W), row)] + halo_specs(hrows, HG_W)
                  + [full(a) for a in (ln_g, ln_b, wo, g1, b1, wu, cw, cb, wd, g2, b2)]),
        out_specs=pl.BlockSpec((tm, D_MODEL), row),
        out_shape=jax.ShapeDtypeStruct((N, D_MODEL), F32),
        scratch_shapes=[pltpu.VMEM((tm, D_MODEL), F32),
                        pltpu.VMEM((D_MODEL // LANES, tm, LANES), F32),
                        pltpu.VMEM((tm + hrows, D_MODEL), BF16),
                        pltpu.VMEM((2, tm + hrows, 2 * FFN_FC), F32),
                        pltpu.VMEM((2, tm, FFN_FC), BF16),
                        pltpu.VMEM((D_MODEL // LANES, tm, LANES), F32)],
        compiler_params=pltpu.CompilerParams(dimension_semantics=("arbitrary",),
                                             vmem_limit_bytes=VMEM_LIMIT),
        name="ffn",
    )(x2, x2, x2, hg, hg, hg, at, at, at, ln_g, ln_b, wo, g1, b1, wu, cw, cb, wd, g2, b2)


def _swap_halves(w):
    half = MLA_ROPE // 2
    return jnp.concatenate([w[..., half:], w[..., :half]], axis=-1)


def kernel(x, positions, ln_in_g, ln_in_b, w_in, lb_fwd, lb_bwd, hg_norm_g, q_a_norm_g, w_q_b,
           kv_a_norm_g, w_kv_b, attn_norm_g, w_out, ln1_g, ln1_b, w_up, conv_w, conv_b, w_down,
           ln2_g, ln2_b):
    B, S, D = x.shape
    assert D == D_MODEL and w_in.shape[0] == 1 and lb_fwd.shape == (2, HG_W)
    assert S % HG_CHUNK == 0 and S % ATT_TQ == 0 and S % FFN_TM == 0 and D_FF % FFN_FC == 0
    N = B * S
    r1 = lambda a: a.reshape(1, -1)

    w_in0 = w_in[0]
    c_lat = 5 * HG_W
    c_kr = c_lat + MLA_Q_RANK + MLA_KV_RANK
    w_hg = w_in0[:, :c_lat].astype(BF16)
    w_kr = w_in0[:, c_kr:]
    w_lat = jnp.concatenate([w_in0[:, c_lat:c_kr], w_kr, _swap_halves(w_kr)], axis=1).astype(BF16)
    wq = w_q_b[0].reshape(MLA_Q_RANK, MLA_HEADS, MLA_NOPE + MLA_ROPE)
    wq_pe = wq[:, :, MLA_NOPE:]
    wqb = jnp.concatenate(
        [wq[:, :, :MLA_NOPE].reshape(MLA_Q_RANK, -1),
         jnp.concatenate([wq_pe, _swap_halves(wq_pe)], axis=-1).reshape(MLA_Q_RANK, -1)],
        axis=1).astype(BF16)
    wkvb = w_kv_b[0].astype(BF16)
    half = MLA_ROPE // 2
    inv_freq = 1.0 / (ROPE_THETA ** (jnp.arange(half, dtype=F32) / half))
    zeros = jnp.zeros((LANES - MLA_ROPE,), F32)
    invf = r1(jnp.tile(inv_freq, LANES // half))
    sgn = r1(jnp.concatenate([-jnp.ones((half,), F32), jnp.ones((half,), F32), zeros]))
    wu = w_up[0].astype(BF16)
    cw = conv_w[0]
    cb = r1(conv_b[0])
    wd = w_down[0].astype(BF16)

    x2 = x.reshape(N, D)
    pos2 = positions.reshape(N // LANES, LANES)
    q, ff, fb, gate, v, qa, ka, va = _proj(
        x2, pos2, r1(ln_in_g), r1(ln_in_b), w_hg, w_lat, lb_fwd, lb_bwd,
        r1(q_a_norm_g[0]), wqb, r1(kv_a_norm_g[0]), wkvb, invf, sgn, B, S)
    hg = _hgrn(q, ff, fb, v, gate, r1(hg_norm_g[0]), B, S)
    at = _attn(qa, ka, va, r1(attn_norm_g[0]), B, S)
    out = _ffn(x2, hg, at, r1(ln_in_g), r1(ln_in_b), w_out[0].astype(BF16), r1(ln1_g[0]), r1(ln1_b[0]),
               wu, cw, cb, wd, r1(ln2_g[0]), r1(ln2_b[0]), S)
    return out.reshape(B, S, D)
```

```python
import functools

import jax
import jax.numpy as jnp
from jax import lax
from jax.experimental import pallas as pl
from jax.experimental.pallas import tpu as pltpu

D_MODEL = 1024
HG_HEADS = 4
HG_DIM = 128
HG_W = HG_HEADS * HG_DIM
MLA_HEADS = 4
MLA_Q_RANK = 384
MLA_KV_RANK = 128
MLA_NOPE = 128
MLA_ROPE = 64
MLA_VHEAD = 128
MLA_QK_PAD = 256
ROPE_THETA = 10000.0
D_FF = 2816
DN_ALPHA = 2.0 ** 0.25
NORM_EPS = 1e-5
LOG2E = 1.4426950408889634

LANES = 128
SUBLANES = 8
VMEM_LIMIT = 56 * 1024 * 1024

PROJ_TM = 1024
HG_CHUNK = 256
HG_BLOCKS = HG_CHUNK // SUBLANES
HG_STEP_HEADS = 2
HG_UNROLL = 8
ATT_TQ = 256
ATT_SUB = 32
ATT_HEADS = 2
FFN_TM = 512
FFN_FC = 256

F32 = jnp.float32
BF16 = jnp.bfloat16


def _ln(x, g, b):
    mu = jnp.mean(x, axis=-1, keepdims=True)
    xc = x - mu
    var = jnp.mean(xc * xc, axis=-1, keepdims=True)
    return xc * lax.rsqrt(var + NORM_EPS) * g + b


def _rms(x, g):
    return x * lax.rsqrt(jnp.mean(x * x, axis=-1, keepdims=True) + NORM_EPS) * g


def _gelu_tanh(x):
    c1 = -2.0 * (2.0 / jnp.pi) ** 0.5 * LOG2E
    z = x * (c1 + (c1 * 0.044715) * (x * x))
    return x / (1.0 + jnp.exp2(z))


def _dot(a, b):
    return jnp.dot(a, b, preferred_element_type=F32)


def _dot_nt(a, b):
    return lax.dot_general(a, b, (((1,), (1,)), ((), ())), preferred_element_type=F32)


def _proj_kernel(x_ref, pos_ref, lng_ref, lnb_ref, w_hg_ref, w_lat_ref, lbf_ref, lbb_ref,
                 qan_ref, wqb_ref, kvn_ref, wkvb_ref, invf_ref, sgn_ref,
                 q_ref, ff_ref, fb_ref, gate_ref, v_ref, qa_ref, ka_ref, va_ref):
    h = _ln(x_ref[...], lng_ref[...], lnb_ref[...]).astype(BF16)

    def lower_bound(ref):
        r0, r1 = ref[0:1, :], ref[1:2, :]
        m = jnp.maximum(r0, r1)
        e0, e1 = jnp.exp(r0 - m), jnp.exp(r1 - m)
        return e0 / (e0 + e1)

    def hg_cols(j):
        return _dot(h, w_hg_ref[:, j * HG_W:(j + 1) * HG_W])

    q_ref[...] = jax.nn.silu(hg_cols(0))
    v_ref[...] = hg_cols(1).astype(BF16)
    lbf = lower_bound(lbf_ref)
    ff_ref[...] = lbf + (1.0 - lbf) * jax.nn.sigmoid(hg_cols(2))
    lbb = lower_bound(lbb_ref)
    fb_ref[...] = lbb + (1.0 - lbb) * jax.nn.sigmoid(hg_cols(3))
    gate_ref[...] = hg_cols(4)

    hr = x_ref.shape[0] // 2
    pos_t = pos_ref[...].astype(F32).T
    pos = jnp.concatenate([pos_t[:, j:j + 1] for j in range(pos_t.shape[1])], axis=0)
    lane = lax.broadcasted_iota(jnp.int32, (hr, LANES), 1)
    ang = jnp.where(lane < MLA_ROPE, pos[0:hr, :], pos[hr:, :]) * invf_ref[...]
    cs, sn = jnp.cos(ang), jnp.sin(ang)
    sgn = sgn_ref[...]
    keep = jnp.abs(sgn)
    cos_t = jnp.concatenate([cs * keep, pltpu.roll(cs, MLA_ROPE, 1) * keep], axis=0)
    sin_t = jnp.concatenate([sn * sgn, pltpu.roll(sn, MLA_ROPE, 1) * sgn], axis=0)

    def rope(grp):
        return grp * cos_t + pltpu.roll(grp, MLA_ROPE, 1) * sin_t

    lat = _dot(h, w_lat_ref[...])
    qa = _rms(lat[:, :MLA_Q_RANK], qan_ref[...]).astype(BF16)
    kva = _rms(lat[:, MLA_Q_RANK:MLA_Q_RANK + MLA_KV_RANK], kvn_ref[...]).astype(BF16)
    k_pe = rope(lat[:, MLA_Q_RANK + MLA_KV_RANK:]).astype(BF16)

    scale = (MLA_NOPE + MLA_ROPE) ** -0.5 * LOG2E
    qh = _dot(qa, wqb_ref[...])
    kvh = _dot(kva, wkvb_ref[...])
    for hd in range(MLA_HEADS):
        qa_ref[0, hd, :, 0:LANES] = (qh[:, hd * LANES:(hd + 1) * LANES] * scale).astype(BF16)
        pe = rope(qh[:, (MLA_HEADS + hd) * LANES:(MLA_HEADS + hd + 1) * LANES])
        qa_ref[0, hd, :, LANES:2 * LANES] = (pe * scale).astype(BF16)
        ka_ref[0, hd, :, 0:LANES] = kvh[:, 2 * hd * LANES:(2 * hd + 1) * LANES].astype(BF16)
        ka_ref[0, hd, :, LANES:2 * LANES] = k_pe
        va_ref[0, hd, :, :] = kvh[:, (2 * hd + 1) * LANES:(2 * hd + 2) * LANES].astype(BF16)


def _proj(x2, pos2, ln_g, ln_b, w_hg, w_lat, lb_f, lb_b, qan, wqb, kvn, wkvb, invf, sgn, B, S):
    N = B * S
    tm = PROJ_TM
    spb = S // tm
    row = lambda i: (i, 0)
    const = lambda i: (0, 0)
    head = lambda i: (i // spb, 0, i % spb, 0)
    full = lambda a: pl.BlockSpec(a.shape, const)
    out_shape = (
        jax.ShapeDtypeStruct((N, HG_W), F32),
        jax.ShapeDtypeStruct((N, HG_W), F32),
        jax.ShapeDtypeStruct((N, HG_W), F32),
        jax.ShapeDtypeStruct((N, HG_W), F32),
        jax.ShapeDtypeStruct((N, HG_W), BF16),
        jax.ShapeDtypeStruct((B, MLA_HEADS, S, MLA_QK_PAD), BF16),
        jax.ShapeDtypeStruct((B, MLA_HEADS, S, MLA_QK_PAD), BF16),
        jax.ShapeDtypeStruct((B, MLA_HEADS, S, MLA_VHEAD), BF16),
    )
    tok = pl.BlockSpec((tm, HG_W), row)
    return pl.pallas_call(
        _proj_kernel,
        grid=(N // tm,),
        in_specs=[pl.BlockSpec((tm, D_MODEL), row), pl.BlockSpec((tm // LANES, LANES), row),
                  full(ln_g), full(ln_b), full(w_hg), full(w_lat), full(lb_f), full(lb_b),
                  full(qan), full(wqb), full(kvn), full(wkvb), full(invf), full(sgn)],
        out_specs=(tok, tok, tok, tok, tok,
                   pl.BlockSpec((1, MLA_HEADS, tm, MLA_QK_PAD), head),
                   pl.BlockSpec((1, MLA_HEADS, tm, MLA_QK_PAD), head),
                   pl.BlockSpec((1, MLA_HEADS, tm, MLA_VHEAD), head)),
        out_shape=out_shape,
        compiler_params=pltpu.CompilerParams(dimension_semantics=("arbitrary",),
                                             vmem_limit_bytes=VMEM_LIMIT),
        name="proj",
    )(x2, pos2, ln_g, ln_b, w_hg, w_lat, lb_f, lb_b, qan, wqb, kvn, wkvb, invf, sgn)


def _blocks(x):
    return [x[i * SUBLANES:(i + 1) * SUBLANES] for i in range(x.shape[0] // SUBLANES)]


def _rows(blocks):
    return jnp.concatenate(blocks, axis=0)


def _hgrn_chunk(q, f, v, lmap_ref, rev):
    C = HG_CHUNK
    g = jnp.log(f) * LOG2E
    k = 1.0 - f
    row = lax.broadcasted_iota(jnp.int32, (C, HG_DIM), 0)

    def roll8(x, shift):
        return _rows([pltpu.roll(blk, shift, 0) for blk in _blocks(x)])

    P, T = g, g
    A = None
    for lvl in range(3):
        b = 1 << lvl
        bit = ((row >> lvl) & 1) == 1
        later = jnp.logical_not(bit) if rev else bit
        w = jnp.exp2(jnp.where(later, P, T - P))
        a_l = _dot_nt((q * w).astype(BF16), (k * w).astype(BF16))
        hit = lmap_ref[...] == lvl
        A = jnp.where(hit, a_l, 0.0) if A is None else jnp.where(hit, a_l, A)
        sib = roll8(T, b) if 2 * b == SUBLANES else jnp.where(bit, roll8(T, b), roll8(T, SUBLANES - b))
        P = P + jnp.where(later, sib, 0.0)
        T = T + sib

    Pb, Tb, qb, kb, Ab = _blocks(P), _blocks(T), _blocks(q), _blocks(k), _blocks(A)
    zero = jnp.zeros((SUBLANES, HG_DIM), F32)
    lvl, m = 3, 1
    while m < HG_BLOCKS:
        later = [(((i // m) & 1) == 1) != rev for i in range(HG_BLOCKS)]
        wb = [jnp.exp2(Pb[i] if later[i] else Tb[i] - Pb[i]) for i in range(HG_BLOCKS)]
        qt = _rows([qb[i] * wb[i] for i in range(HG_BLOCKS) if later[i]]).astype(BF16)
        kt = _rows([zero if later[i] else kb[i] * wb[i] for i in range(HG_BLOCKS)]).astype(BF16)
        a_l = _blocks(_dot_nt(qt, kt))
        n_later = 0
        for i in range(HG_BLOCKS):
            if later[i]:
                hit = lmap_ref[i * SUBLANES:(i + 1) * SUBLANES, :] == lvl
                Ab[i] = jnp.where(hit, a_l[n_later], Ab[i])
                n_later += 1
        sib = [Tb[i ^ m] for i in range(HG_BLOCKS)]
        Pb = [Pb[i] + sib[i] if later[i] else Pb[i] for i in range(HG_BLOCKS)]
        tot = {}
        for i in range(HG_BLOCKS):
            key = i // (2 * m)
            if key not in tot:
                tot[key] = Tb[i] + sib[i]
        Tb = [tot[i // (2 * m)] for i in range(HG_BLOCKS)]
        lvl, m = lvl + 1, 2 * m

    vf = v.astype(F32)
    o = _dot(_rows(Ab).astype(BF16), v) + jnp.sum(q * k, axis=1, keepdims=True) * vf
    qt = _rows([qb[i] * jnp.exp2(Pb[i]) for i in range(HG_BLOCKS)]).astype(BF16)
    kt = _rows([kb[i] * jnp.exp2(Tb[i] - Pb[i]) for i in range(HG_BLOCKS)]).astype(BF16)
    return o, qt, kt, jnp.exp2(Tb[0])


def _hgrn_kernel(q_ref, ff_ref, fb_ref, v_ref, gate_ref, ng_ref, out_ref, o_acc, lmap_s):
    S = q_ref.shape[0]
    C = HG_CHUNK
    n = S // C

    ri = lax.broadcasted_iota(jnp.int32, (C, C), 0)
    ci = lax.broadcasted_iota(jnp.int32, (C, C), 1)
    x = ri ^ ci
    level = jnp.full((C, C), -1, jnp.int32)
    b = 1
    while b < C:
        level = level + jnp.where(x >= b, 1, 0)
        b *= 2
    lmap_s[0] = jnp.where(ci < ri, level, -1)
    lmap_s[1] = jnp.where(ci > ri, level, -1)

    nh = q_ref.shape[1] // HG_DIM

    def visit(c, hd, f_ref, st, rev, first):
        r = pl.multiple_of(c * C, C)
        cols = slice(hd * HG_DIM, (hd + 1) * HG_DIM)
        q = q_ref[pl.ds(r, C), cols]
        v = v_ref[pl.ds(r, C), cols]
        o, qt, kt, d = _hgrn_chunk(q, f_ref[pl.ds(r, C), cols], v, lmap_s.at[1 if rev else 0], rev)
        o = o + _dot_nt(qt, st.astype(BF16))
        st = st * d[0:1, :] + _dot(v.astype(F32).T.astype(BF16), kt)
        if first:
            o_acc[pl.ds(r, C), cols] = o
        else:
            o = _rms(o + o_acc[pl.ds(r, C), cols], ng_ref[...])
            out_ref[pl.ds(r, C), cols] = (o * jax.nn.silu(gate_ref[pl.ds(r, C), cols])).astype(BF16)
        return st

    def step(first):
        def body(i, carry):
            new = []
            for hd in range(nh):
                st_f, st_b = carry[2 * hd], carry[2 * hd + 1]
                new.append(visit(i, hd, ff_ref, st_f, False, first))
                new.append(visit(n - 1 - i, hd, fb_ref, st_b, True, first))
            return tuple(new)
        return body

    z = jnp.zeros((HG_DIM, HG_DIM), F32)
    carry = lax.fori_loop(0, n // 2, step(True), (z,) * (2 * nh), unroll=HG_UNROLL)
    lax.fori_loop(n // 2, n, step(False), carry, unroll=HG_UNROLL)


def _hgrn(q, ff, fb, v, gate, ng, B, S):
    N = B * S
    C = HG_CHUNK
    assert (S // C) % 2 == 0
    width = HG_STEP_HEADS * HG_DIM
    blk = pl.BlockSpec((S, width), lambda b, h: (b, h))
    return pl.pallas_call(
        _hgrn_kernel,
        grid=(B, HG_HEADS // HG_STEP_HEADS),
        in_specs=[blk, blk, blk, blk, blk, pl.BlockSpec((1, HG_DIM), lambda b, h: (0, 0))],
        out_specs=blk,
        out_shape=jax.ShapeDtypeStruct((N, HG_W), BF16),
        scratch_shapes=[pltpu.VMEM((S, width), F32),
                        pltpu.VMEM((2, C, C), jnp.int32)],
        compiler_params=pltpu.CompilerParams(dimension_semantics=("arbitrary", "arbitrary"),
                                             vmem_limit_bytes=VMEM_LIMIT),
        name="hgrn",
    )(q, ff, fb, v, gate, ng)


def _attn_kernel(q_ref, k_ref, v_ref, ng_ref, out_ref, s_s, p_s, l_s):
    S = k_ref.shape[2]
    tq = ATT_TQ
    nq = S // tq

    nh = q_ref.shape[1]

    def scores(t, slot):
        h, i = divmod(t, nq)
        s_s[slot] = _dot_nt(q_ref[0, h, i * tq:(i + 1) * tq, :], k_ref[0, h])

    def softmax(slot):
        for r in range(0, tq, ATT_SUB):
            s = s_s[slot, r:r + ATT_SUB, :]
            p = jnp.exp2(s - jnp.max(s, axis=-1, keepdims=True))
            l_s[slot, r:r + ATT_SUB, :] = jnp.sum(p, axis=-1, keepdims=True)
            p_s[slot, r:r + ATT_SUB, :] = p.astype(BF16)

    def values(t, slot):
        h, i = divmod(t, nq)
        o = _dot(p_s[slot], v_ref[0, h]) / l_s[slot]
        out_ref[i * tq:(i + 1) * tq, h * MLA_VHEAD:(h + 1) * MLA_VHEAD] = _rms(
            o, ng_ref[...]).astype(BF16)

    scores(0, 0)
    for t in range(nh * nq):
        if t + 1 < nh * nq:
            scores(t + 1, (t + 1) % 2)
        softmax(t % 2)
        values(t, t % 2)


def _attn(qa, ka, va, ng, B, S):
    tq = ATT_TQ
    nh = ATT_HEADS
    return pl.pallas_call(
        _attn_kernel,
        grid=(B, MLA_HEADS // nh),
        in_specs=[pl.BlockSpec((1, nh, S, MLA_QK_PAD), lambda b, g: (b, g, 0, 0)),
                  pl.BlockSpec((1, nh, S, MLA_QK_PAD), lambda b, g: (b, g, 0, 0)),
                  pl.BlockSpec((1, nh, S, MLA_VHEAD), lambda b, g: (b, g, 0, 0)),
                  pl.BlockSpec((1, MLA_VHEAD), lambda b, g: (0, 0))],
        out_specs=pl.BlockSpec((S, nh * MLA_VHEAD), lambda b, g: (b, g)),
        out_shape=jax.ShapeDtypeStruct((B * S, MLA_HEADS * MLA_VHEAD), BF16),
        scratch_shapes=[pltpu.VMEM((2, tq, S), F32),
                        pltpu.VMEM((2, tq, S), BF16),
                        pltpu.VMEM((2, tq, 1), F32)],
        compiler_params=pltpu.CompilerParams(
            dimension_semantics=("arbitrary", "arbitrary"),
            vmem_limit_bytes=VMEM_LIMIT),
        name="attn",
    )(qa, ka, va, ng)


def _ffn_kernel(x_ref, xp_ref, xn_ref, hg_ref, hgp_ref, hgn_ref, at_ref, atp_ref, atn_ref,
                lng_ref, lnb_ref, wo_ref, g1_ref, b1_ref,
                wu_ref, cw_ref, cb_ref, wd_ref, g2_ref, b2_ref, out_ref,
                h1_s, il_s, ext_s, u_s, act_s, acc_s, *, tiles_per_seq):
    tm = FFN_TM
    hrows = 2 * SUBLANES
    grp = tm // SUBLANES
    n = D_FF // FFN_FC
    i = pl.program_id(0)
    has_prev = jnp.where(i % tiles_per_seq != 0, 1.0, 0.0)
    has_next = jnp.where(i % tiles_per_seq != tiles_per_seq - 1, 1.0, 0.0)

    def with_halo(main, before, after):
        sub = lax.broadcasted_iota(jnp.int32, (hrows, main.shape[1]), 0)
        halo = jnp.where(sub == 0, before.astype(F32), jnp.where(sub == 1, after.astype(F32), 0.0))
        return jnp.concatenate([main, halo.astype(main.dtype)], axis=0)

    def token_mix(xx, hgx, atx):
        mix = _dot(hgx, wo_ref[0:HG_W, :]) + _dot(atx, wo_ref[HG_W:, :])
        return _ln(DN_ALPHA * _ln(xx, lng_ref[...], lnb_ref[...]) + mix, g1_ref[...], b1_ref[...])

    h1x = token_mix(
        with_halo(x_ref[...], xp_ref[SUBLANES - 1:SUBLANES, :], xn_ref[0:1, :]),
        with_halo(hg_ref[...], hgp_ref[...].astype(F32)[hrows - 1:hrows, :],
                  hgn_ref[...].astype(F32)[0:1, :]),
        with_halo(at_ref[...], atp_ref[...].astype(F32)[hrows - 1:hrows, :],
                  atn_ref[...].astype(F32)[0:1, :]))
    h1_s[...] = h1x[0:tm, :]
    sub = lax.broadcasted_iota(jnp.int32, (hrows, D_MODEL), 0)
    valid = jnp.where(sub == 0, has_prev, jnp.where(sub == 1, has_next, 0.0))
    ext_s[tm:, :] = (h1x[tm:, :] * valid).astype(BF16)

    ngrp = D_MODEL // LANES
    for c in range(ngrp):
        for a in range(SUBLANES):
            il_s[c, pl.ds(a, grp, stride=SUBLANES), :] = h1_s[a * grp:(a + 1) * grp,
                                                              c * LANES:(c + 1) * LANES]
        ext_s[0:tm, c * LANES:(c + 1) * LANES] = il_s[c].astype(BF16)
    acc_s[...] = jnp.zeros_like(acc_s)

    half = tm // 2

    def chunk_cols(j, part):
        start = part * D_FF + j * FFN_FC
        return pl.ds(start if isinstance(start, int) else pl.multiple_of(start, FFN_FC), FFN_FC)

    def up(j, slot, part):
        u_s[slot, :, part * FFN_FC:(part + 1) * FFN_FC] = _dot(ext_s[...],
                                                               wu_ref[:, chunk_cols(j, part)])

    def glu(j, slot, part):
        r0, r1 = part * half, (part + 1) * half
        cw = jnp.concatenate([cw_ref[:, chunk_cols(j, 0)], cw_ref[:, chunk_cols(j, 1)]], axis=1)
        cb = jnp.concatenate([cb_ref[:, chunk_cols(j, 0)], cb_ref[:, chunk_cols(j, 1)]], axis=1)
        sub8 = lax.broadcasted_iota(jnp.int32, (SUBLANES, 2 * FFN_FC), 0)
        if part == 0:
            first = jnp.where(sub8 == 0, u_s[slot, tm:tm + 1, :],
                              pltpu.roll(u_s[slot, tm - SUBLANES:tm, :], 1, 0))
            prev = jnp.concatenate([first, u_s[slot, 0:r1 - SUBLANES, :]], axis=0)
            nxt = u_s[slot, SUBLANES:r1 + SUBLANES, :]
        else:
            last = jnp.where(sub8 == SUBLANES - 1, u_s[slot, tm + 1:tm + 2, :],
                             pltpu.roll(u_s[slot, 0:SUBLANES, :], SUBLANES - 1, 0))
            prev = u_s[slot, r0 - SUBLANES:r1 - SUBLANES, :]
            nxt = jnp.concatenate([u_s[slot, r0 + SUBLANES:tm, :], last], axis=0)
        c = prev * cw[0:1, :] + u_s[slot, r0:r1, :] * cw[1:2, :] + nxt * cw[2:3, :] + cb
        act_s[slot, r0:r1, :] = (_gelu_tanh(c[:, :FFN_FC]) * c[:, FFN_FC:]).astype(BF16)

    def down(j, slot, part):
        c0 = part * 2 * LANES
        k0 = j * FFN_FC
        rows = pl.ds(k0 if isinstance(k0, int) else pl.multiple_of(k0, FFN_FC), FFN_FC)
        res = _dot(act_s[slot], wd_ref[rows, c0:c0 + 2 * LANES])
        acc_s[2 * part] += res[:, :LANES]
        acc_s[2 * part + 1] += res[:, LANES:]

    def stage(t, par):
        static = isinstance(t, int)
        do_up = not static or t + 1 < n
        do_glu = not static or 0 <= t < n
        do_down = not static or 1 <= t <= n
        for part in range(2):
            if do_up:
                up(t + 1, 1 - par, part)
            if do_glu:
                glu(t, par, part)
            if do_down:
                down(t - 1, 1 - par, 2 * part)
                down(t - 1, 1 - par, 2 * part + 1)

    stage(-1, 1)
    stage(0, 0)
    npairs = (n - 2) // 2

    def pair(jj, carry):
        t = 2 * jj + 1
        stage(t, 1)
        stage(t + 1, 0)
        return carry

    lax.fori_loop(0, npairs, pair, 0, unroll=True)
    for t in range(2 * npairs + 1, n + 1):
        stage(t, t % 2)

    for a in range(SUBLANES):
        ffn = jnp.concatenate([acc_s[c, pl.ds(a, grp, stride=SUBLANES), :] for c in range(ngrp)],
                              axis=1)
        rows = slice(a * grp, (a + 1) * grp)
        out_ref[rows, :] = _ln(DN_ALPHA * h1_s[rows, :] + ffn, g2_ref[...], b2_ref[...])


def _ffn(x2, hg, at, ln_g, ln_b, wo, g1, b1, wu, cw, cb, wd, g2, b2, S):
    N = x2.shape[0]
    tm = FFN_TM
    hrows = 2 * SUBLANES
    row = lambda i: (i, 0)
    full = lambda a: pl.BlockSpec(a.shape, lambda i: (0,) * a.ndim)

    def halo_specs(rows, width):
        per_tile = tm // rows
        last = N // rows - 1
        return [pl.BlockSpec((rows, width), lambda i: (jnp.maximum(i * per_tile - 1, 0), 0)),
                pl.BlockSpec((rows, width), lambda i: (jnp.minimum((i + 1) * per_tile, last), 0))]

    return pl.pallas_call(
        functools.partial(_ffn_kernel, tiles_per_seq=S // tm),
        grid=(N // tm,),
        in_specs=([pl.BlockSpec((tm, D_MODEL), row)] + halo_specs(SUBLANES, D_MODEL)
                  + [pl.BlockSpec((tm, HG_W), row)] + halo_specs(hrows, HG_W)
                  + [pl.BlockSpec((tm, HG_W), row)] + halo_specs(hrows, HG_W)
                  + [full(a) for a in (ln_g, ln_b, wo, g1, b1, wu, cw, cb, wd, g2, b2)]),
        out_specs=pl.BlockSpec((tm, D_MODEL), row),
        out_shape=jax.ShapeDtypeStruct((N, D_MODEL), F32),
        scratch_shapes=[pltpu.VMEM((tm, D_MODEL), F32),
                        pltpu.VMEM((D_MODEL // LANES, tm, LANES), F32),
                        pltpu.VMEM((tm + hrows, D_MODEL), BF16),
                        pltpu.VMEM((2, tm + hrows, 2 * FFN_FC), F32),
                        pltpu.VMEM((2, tm, FFN_FC), BF16),
                        pltpu.VMEM((D_MODEL // LANES, tm, LANES), F32)],
        compiler_params=pltpu.CompilerParams(dimension_semantics=("arbitrary",),
                                             vmem_limit_bytes=VMEM_LIMIT),
        name="ffn",
    )(x2, x2, x2, hg, hg, hg, at, at, at, ln_g, ln_b, wo, g1, b1, wu, cw, cb, wd, g2, b2)


def _swap_halves(w):
    half = MLA_ROPE // 2
    return jnp.concatenate([w[..., half:], w[..., :half]], axis=-1)


def kernel(x, positions, ln_in_g, ln_in_b, w_in, lb_fwd, lb_bwd, hg_norm_g, q_a_norm_g, w_q_b,
           kv_a_norm_g, w_kv_b, attn_norm_g, w_out, ln1_g, ln1_b, w_up, conv_w, conv_b, w_down,
           ln2_g, ln2_b):
    B, S, D = x.shape
    assert D == D_MODEL and w_in.shape[0] == 1 and lb_fwd.shape == (2, HG_W)
    assert S % HG_CHUNK == 0 and S % ATT_TQ == 0 and S % FFN_TM == 0 and D_FF % FFN_FC == 0
    N = B * S
    r1 = lambda a: a.reshape(1, -1)

    w_in0 = w_in[0]
    c_lat = 5 * HG_W
    c_kr = c_lat + MLA_Q_RANK + MLA_KV_RANK
    w_hg = w_in0[:, :c_lat].astype(BF16)
    w_kr = w_in0[:, c_kr:]
    w_lat = jnp.concatenate([w_in0[:, c_lat:c_kr], w_kr, _swap_halves(w_kr)], axis=1).astype(BF16)
    wq = w_q_b[0].reshape(MLA_Q_RANK, MLA_HEADS, MLA_NOPE + MLA_ROPE)
    wq_pe = wq[:, :, MLA_NOPE:]
    wqb = jnp.concatenate(
        [wq[:, :, :MLA_NOPE].reshape(MLA_Q_RANK, -1),
         jnp.concatenate([wq_pe, _swap_halves(wq_pe)], axis=-1).reshape(MLA_Q_RANK, -1)],
        axis=1).astype(BF16)
    wkvb = w_kv_b[0].astype(BF16)
    half = MLA_ROPE // 2
    inv_freq = 1.0 / (ROPE_THETA ** (jnp.arange(half, dtype=F32) / half))
    zeros = jnp.zeros((LANES - MLA_ROPE,), F32)
    invf = r1(jnp.tile(inv_freq, LANES // half))
    sgn = r1(jnp.concatenate([-jnp.ones((half,), F32), jnp.ones((half,), F32), zeros]))
    wu = w_up[0].astype(BF16)
    cw = conv_w[0]
    cb = r1(conv_b[0])
    wd = w_down[0].astype(BF16)

    x2 = x.reshape(N, D)
    pos2 = positions.reshape(N // LANES, LANES)
    q, ff, fb, gate, v, qa, ka, va = _proj(
        x2, pos2, r1(ln_in_g), r1(ln_in_b), w_hg, w_lat, lb_fwd, lb_bwd,
        r1(q_a_norm_g[0]), wqb, r1(kv_a_norm_g[0]), wkvb, invf, sgn, B, S)
    hg = _hgrn(q, ff, fb, v, gate, r1(hg_norm_g[0]), B, S)
    at = _attn(qa, ka, va, r1(attn_norm_g[0]), B, S)
    out = _ffn(x2, hg, at, r1(ln_in_g), r1(ln_in_b), w_out[0].astype(BF16), r1(ln1_g[0]), r1(ln1_b[0]),
               wu, cw, cb, wd, r1(ln2_g[0]), r1(ln2_b[0]), S)
    return out.reshape(B, S, D)
```

```python
import functools

import jax
import jax.numpy as jnp
from jax import lax
from jax.experimental import pallas as pl
from jax.experimental.pallas import tpu as pltpu

D_MODEL = 1024
HG_HEADS = 4
HG_DIM = 128
HG_W = HG_HEADS * HG_DIM
MLA_HEADS = 4
MLA_Q_RANK = 384
MLA_KV_RANK = 128
MLA_NOPE = 128
MLA_ROPE = 64
MLA_VHEAD = 128
MLA_QK_PAD = 256
ROPE_THETA = 10000.0
D_FF = 2816
DN_ALPHA = 2.0 ** 0.25
NORM_EPS = 1e-5
LOG2E = 1.4426950408889634

LANES = 128
SUBLANES = 8
VMEM_LIMIT = 56 * 1024 * 1024

PROJ_TM = 1024
HG_CHUNK = 256
HG_BLOCKS = HG_CHUNK // SUBLANES
HG_UNROLL = 8
ATT_TQ = 256
ATT_SUB = 32
ATT_HEADS = 2
FFN_TM = 1024
FFN_FC = 256

F32 = jnp.float32
BF16 = jnp.bfloat16


def _ln(x, g, b):
    mu = jnp.mean(x, axis=-1, keepdims=True)
    xc = x - mu
    var = jnp.mean(xc * xc, axis=-1, keepdims=True)
    return xc * lax.rsqrt(var + NORM_EPS) * g + b


def _rms(x, g):
    return x * lax.rsqrt(jnp.mean(x * x, axis=-1, keepdims=True) + NORM_EPS) * g


def _gelu_tanh(x):
    c1 = -2.0 * (2.0 / jnp.pi) ** 0.5 * LOG2E
    z = x * (c1 + (c1 * 0.044715) * (x * x))
    return x / (1.0 + jnp.exp2(z))


def _dot(a, b):
    return jnp.dot(a, b, preferred_element_type=F32)


def _dot_nt(a, b):
    return lax.dot_general(a, b, (((1,), (1,)), ((), ())), preferred_element_type=F32)


def _proj_kernel(x_ref, pos_ref, lng_ref, lnb_ref, w_hg_ref, w_lat_ref, lbf_ref, lbb_ref,
                 qan_ref, wqb_ref, kvn_ref, wkvb_ref, invf_ref, sgn_ref,
                 q_ref, ff_ref, fb_ref, gate_ref, v_ref, qa_ref, ka_ref, va_ref):
    h = _ln(x_ref[...], lng_ref[...], lnb_ref[...]).astype(BF16)

    def lower_bound(ref):
        r0, r1 = ref[0:1, :], ref[1:2, :]
        m = jnp.maximum(r0, r1)
        e0, e1 = jnp.exp(r0 - m), jnp.exp(r1 - m)
        return e0 / (e0 + e1)

    def hg_cols(j):
        return _dot(h, w_hg_ref[:, j * HG_W:(j + 1) * HG_W])

    q_ref[...] = jax.nn.silu(hg_cols(0))
    v_ref[...] = hg_cols(1).astype(BF16)
    lbf = lower_bound(lbf_ref)
    ff_ref[...] = lbf + (1.0 - lbf) * jax.nn.sigmoid(hg_cols(2))
    lbb = lower_bound(lbb_ref)
    fb_ref[...] = lbb + (1.0 - lbb) * jax.nn.sigmoid(hg_cols(3))
    gate_ref[...] = hg_cols(4)

    hr = x_ref.shape[0] // 2
    pos_t = pos_ref[...].astype(F32).T
    pos = jnp.concatenate([pos_t[:, j:j + 1] for j in range(pos_t.shape[1])], axis=0)
    lane = lax.broadcasted_iota(jnp.int32, (hr, LANES), 1)
    ang = jnp.where(lane < MLA_ROPE, pos[0:hr, :], pos[hr:, :]) * invf_ref[...]
    cs, sn = jnp.cos(ang), jnp.sin(ang)
    sgn = sgn_ref[...]
    keep = jnp.abs(sgn)
    cos_t = jnp.concatenate([cs * keep, pltpu.roll(cs, MLA_ROPE, 1) * keep], axis=0)
    sin_t = jnp.concatenate([sn * sgn, pltpu.roll(sn, MLA_ROPE, 1) * sgn], axis=0)

    def rope(grp):
        return grp * cos_t + pltpu.roll(grp, MLA_ROPE, 1) * sin_t

    lat = _dot(h, w_lat_ref[...])
    qa = _rms(lat[:, :MLA_Q_RANK], qan_ref[...]).astype(BF16)
    kva = _rms(lat[:, MLA_Q_RANK:MLA_Q_RANK + MLA_KV_RANK], kvn_ref[...]).astype(BF16)
    k_pe = rope(lat[:, MLA_Q_RANK + MLA_KV_RANK:]).astype(BF16)

    scale = (MLA_NOPE + MLA_ROPE) ** -0.5 * LOG2E
    qh = _dot(qa, wqb_ref[...])
    kvh = _dot(kva, wkvb_ref[...])
    for hd in range(MLA_HEADS):
        qa_ref[0, hd, :, 0:LANES] = (qh[:, hd * LANES:(hd + 1) * LANES] * scale).astype(BF16)
        pe = rope(qh[:, (MLA_HEADS + hd) * LANES:(MLA_HEADS + hd + 1) * LANES])
        qa_ref[0, hd, :, LANES:2 * LANES] = (pe * scale).astype(BF16)
        ka_ref[0, hd, :, 0:LANES] = kvh[:, 2 * hd * LANES:(2 * hd + 1) * LANES].astype(BF16)
        ka_ref[0, hd, :, LANES:2 * LANES] = k_pe
        va_ref[0, hd, :, :] = kvh[:, (2 * hd + 1) * LANES:(2 * hd + 2) * LANES].astype(BF16)


def _proj(x2, pos2, ln_g, ln_b, w_hg, w_lat, lb_f, lb_b, qan, wqb, kvn, wkvb, invf, sgn, B, S):
    N = B * S
    tm = PROJ_TM
    spb = S // tm
    row = lambda i: (i, 0)
    const = lambda i: (0, 0)
    head = lambda i: (i // spb, 0, i % spb, 0)
    full = lambda a: pl.BlockSpec(a.shape, const)
    out_shape = (
        jax.ShapeDtypeStruct((N, HG_W), F32),
        jax.ShapeDtypeStruct((N, HG_W), F32),
        jax.ShapeDtypeStruct((N, HG_W), F32),
        jax.ShapeDtypeStruct((N, HG_W), F32),
        jax.ShapeDtypeStruct((N, HG_W), BF16),
        jax.ShapeDtypeStruct((B, MLA_HEADS, S, MLA_QK_PAD), BF16),
        jax.ShapeDtypeStruct((B, MLA_HEADS, S, MLA_QK_PAD), BF16),
        jax.ShapeDtypeStruct((B, MLA_HEADS, S, MLA_VHEAD), BF16),
    )
    tok = pl.BlockSpec((tm, HG_W), row)
    return pl.pallas_call(
        _proj_kernel,
        grid=(N // tm,),
        in_specs=[pl.BlockSpec((tm, D_MODEL), row), pl.BlockSpec((tm // LANES, LANES), row),
                  full(ln_g), full(ln_b), full(w_hg), full(w_lat), full(lb_f), full(lb_b),
                  full(qan), full(wqb), full(kvn), full(wkvb), full(invf), full(sgn)],
        out_specs=(tok, tok, tok, tok, tok,
                   pl.BlockSpec((1, MLA_HEADS, tm, MLA_QK_PAD), head),
                   pl.BlockSpec((1, MLA_HEADS, tm, MLA_QK_PAD), head),
                   pl.BlockSpec((1, MLA_HEADS, tm, MLA_VHEAD), head)),
        out_shape=out_shape,
        compiler_params=pltpu.CompilerParams(dimension_semantics=("arbitrary",),
                                             vmem_limit_bytes=VMEM_LIMIT),
        name="proj",
    )(x2, pos2, ln_g, ln_b, w_hg, w_lat, lb_f, lb_b, qan, wqb, kvn, wkvb, invf, sgn)


def _blocks(x):
    return [x[i * SUBLANES:(i + 1) * SUBLANES] for i in range(x.shape[0] // SUBLANES)]


def _rows(blocks):
    return jnp.concatenate(blocks, axis=0)


def _hgrn_chunk(q, f, v, lmap_ref, rev):
    C = HG_CHUNK
    g = jnp.log(f) * LOG2E
    k = 1.0 - f
    row = lax.broadcasted_iota(jnp.int32, (C, HG_DIM), 0)

    def roll8(x, shift):
        return _rows([pltpu.roll(blk, shift, 0) for blk in _blocks(x)])

    P, T = g, g
    A = None
    for lvl in range(3):
        b = 1 << lvl
        bit = ((row >> lvl) & 1) == 1
        later = jnp.logical_not(bit) if rev else bit
        w = jnp.exp2(jnp.where(later, P, T - P))
        a_l = _dot_nt((q * w).astype(BF16), (k * w).astype(BF16))
        hit = lmap_ref[...] == lvl
        A = jnp.where(hit, a_l, 0.0) if A is None else jnp.where(hit, a_l, A)
        sib = roll8(T, b) if 2 * b == SUBLANES else jnp.where(bit, roll8(T, b), roll8(T, SUBLANES - b))
        P = P + jnp.where(later, sib, 0.0)
        T = T + sib

    Pb, Tb, qb, kb, Ab = _blocks(P), _blocks(T), _blocks(q), _blocks(k), _blocks(A)
    zero = jnp.zeros((SUBLANES, HG_DIM), F32)
    lvl, m = 3, 1
    while m < HG_BLOCKS:
        later = [(((i // m) & 1) == 1) != rev for i in range(HG_BLOCKS)]
        wb = [jnp.exp2(Pb[i] if later[i] else Tb[i] - Pb[i]) for i in range(HG_BLOCKS)]
        qt = _rows([qb[i] * wb[i] for i in range(HG_BLOCKS) if later[i]]).astype(BF16)
        kt = _rows([zero if later[i] else kb[i] * wb[i] for i in range(HG_BLOCKS)]).astype(BF16)
        a_l = _blocks(_dot_nt(qt, kt))
        n_later = 0
        for i in range(HG_BLOCKS):
            if later[i]:
                hit = lmap_ref[i * SUBLANES:(i + 1) * SUBLANES, :] == lvl
                Ab[i] = jnp.where(hit, a_l[n_later], Ab[i])
                n_later += 1
        sib = [Tb[i ^ m] for i in range(HG_BLOCKS)]
        Pb = [Pb[i] + sib[i] if later[i] else Pb[i] for i in range(HG_BLOCKS)]
        tot = {}
        for i in range(HG_BLOCKS):
            key = i // (2 * m)
            if key not in tot:
                tot[key] = Tb[i] + sib[i]
        Tb = [tot[i // (2 * m)] for i in range(HG_BLOCKS)]
        lvl, m = lvl + 1, 2 * m

    vf = v.astype(F32)
    o = _dot(_rows(Ab).astype(BF16), v) + jnp.sum(q * k, axis=1, keepdims=True) * vf
    qt = _rows([qb[i] * jnp.exp2(Pb[i]) for i in range(HG_BLOCKS)]).astype(BF16)
    kt = _rows([kb[i] * jnp.exp2(Tb[i] - Pb[i]) for i in range(HG_BLOCKS)]).astype(BF16)
    return o, qt, kt, jnp.exp2(Tb[0])


def _hgrn_kernel(q_ref, ff_ref, fb_ref, v_ref, gate_ref, ng_ref, out_ref, o_acc, lmap_s):
    S = q_ref.shape[0]
    C = HG_CHUNK
    n = S // C

    ri = lax.broadcasted_iota(jnp.int32, (C, C), 0)
    ci = lax.broadcasted_iota(jnp.int32, (C, C), 1)
    x = ri ^ ci
    level = jnp.full((C, C), -1, jnp.int32)
    b = 1
    while b < C:
        level = level + jnp.where(x >= b, 1, 0)
        b *= 2
    lmap_s[0] = jnp.where(ci < ri, level, -1)
    lmap_s[1] = jnp.where(ci > ri, level, -1)

    def visit(c, f_ref, st, rev, first):
        r = pl.multiple_of(c * C, C)
        q = q_ref[pl.ds(r, C), :]
        v = v_ref[pl.ds(r, C), :]
        o, qt, kt, d = _hgrn_chunk(q, f_ref[pl.ds(r, C), :], v, lmap_s.at[1 if rev else 0], rev)
        o = o + _dot_nt(qt, st.astype(BF16))
        st = st * d[0:1, :] + _dot(v.astype(F32).T.astype(BF16), kt)
        if first:
            o_acc[pl.ds(r, C), :] = o
        else:
            o = _rms(o + o_acc[pl.ds(r, C), :], ng_ref[...])
            out_ref[pl.ds(r, C), :] = (o * jax.nn.silu(gate_ref[pl.ds(r, C), :])).astype(BF16)
        return st

    def step(first):
        def body(i, carry):
            st_f, st_b = carry
            return (visit(i, ff_ref, st_f, False, first),
                    visit(n - 1 - i, fb_ref, st_b, True, first))
        return body

    z = jnp.zeros((HG_DIM, HG_DIM), F32)
    carry = lax.fori_loop(0, n // 2, step(True), (z, z), unroll=HG_UNROLL)
    lax.fori_loop(n // 2, n, step(False), carry, unroll=HG_UNROLL)


def _hgrn(q, ff, fb, v, gate, ng, B, S):
    N = B * S
    C = HG_CHUNK
    assert (S // C) % 2 == 0
    blk = pl.BlockSpec((S, HG_DIM), lambda b, h: (b, h))
    return pl.pallas_call(
        _hgrn_kernel,
        grid=(B, HG_HEADS),
        in_specs=[blk, blk, blk, blk, blk, pl.BlockSpec((1, HG_DIM), lambda b, h: (0, 0))],
        out_specs=blk,
        out_shape=jax.ShapeDtypeStruct((N, HG_W), BF16),
        scratch_shapes=[pltpu.VMEM((S, HG_DIM), F32),
                        pltpu.VMEM((2, C, C), jnp.int32)],
        compiler_params=pltpu.CompilerParams(dimension_semantics=("arbitrary", "arbitrary"),
                                             vmem_limit_bytes=VMEM_LIMIT),
        name="hgrn",
    )(q, ff, fb, v, gate, ng)


def _attn_kernel(q_ref, k_ref, v_ref, ng_ref, out_ref, s_s, p_s, l_s):
    S = k_ref.shape[2]
    tq = ATT_TQ
    nq = S // tq

    nh = q_ref.shape[1]

    def scores(t, slot):
        h, i = divmod(t, nq)
        s_s[slot] = _dot_nt(q_ref[0, h, i * tq:(i + 1) * tq, :], k_ref[0, h])

    def softmax(slot):
        for r in range(0, tq, ATT_SUB):
            s = s_s[slot, r:r + ATT_SUB, :]
            p = jnp.exp2(s - jnp.max(s, axis=-1, keepdims=True))
            l_s[slot, r:r + ATT_SUB, :] = jnp.sum(p, axis=-1, keepdims=True)
            p_s[slot, r:r + ATT_SUB, :] = p.astype(BF16)

    def values(t, slot):
        h, i = divmod(t, nq)
        o = _dot(p_s[slot], v_ref[0, h]) / l_s[slot]
        out_ref[i * tq:(i + 1) * tq, h * MLA_VHEAD:(h + 1) * MLA_VHEAD] = _rms(
            o, ng_ref[...]).astype(BF16)

    scores(0, 0)
    for t in range(nh * nq):
        if t + 1 < nh * nq:
            scores(t + 1, (t + 1) % 2)
        softmax(t % 2)
        values(t, t % 2)


def _attn(qa, ka, va, ng, B, S):
    tq = ATT_TQ
    nh = ATT_HEADS
    return pl.pallas_call(
        _attn_kernel,
        grid=(B, MLA_HEADS // nh),
        in_specs=[pl.BlockSpec((1, nh, S, MLA_QK_PAD), lambda b, g: (b, g, 0, 0)),
                  pl.BlockSpec((1, nh, S, MLA_QK_PAD), lambda b, g: (b, g, 0, 0)),
                  pl.BlockSpec((1, nh, S, MLA_VHEAD), lambda b, g: (b, g, 0, 0)),
                  pl.BlockSpec((1, MLA_VHEAD), lambda b, g: (0, 0))],
        out_specs=pl.BlockSpec((S, nh * MLA_VHEAD), lambda b, g: (b, g)),
        out_shape=jax.ShapeDtypeStruct((B * S, MLA_HEADS * MLA_VHEAD), BF16),
        scratch_shapes=[pltpu.VMEM((2, tq, S), F32),
                        pltpu.VMEM((2, tq, S), BF16),
                        pltpu.VMEM((2, tq, 1), F32)],
        compiler_params=pltpu.CompilerParams(
            dimension_semantics=("arbitrary", "arbitrary"),
            vmem_limit_bytes=VMEM_LIMIT),
        name="attn",
    )(qa, ka, va, ng)


def _ffn_kernel(x_ref, xp_ref, xn_ref, hg_ref, hgp_ref, hgn_ref, at_ref, atp_ref, atn_ref,
                lng_ref, lnb_ref, wo_ref, g1_ref, b1_ref,
                wu_ref, cw_ref, cb_ref, wd_ref, g2_ref, b2_ref, out_ref,
                ext_s, u_s, act_s, acc_s, *, tiles_per_seq):
    h1_s = out_ref
    il_s = acc_s
    tm = FFN_TM
    hrows = 2 * SUBLANES
    grp = tm // SUBLANES
    n = D_FF // FFN_FC
    i = pl.program_id(0)
    has_prev = jnp.where(i % tiles_per_seq != 0, 1.0, 0.0)
    has_next = jnp.where(i % tiles_per_seq != tiles_per_seq - 1, 1.0, 0.0)

    def with_halo(main, before, after):
        sub = lax.broadcasted_iota(jnp.int32, (hrows, main.shape[1]), 0)
        halo = jnp.where(sub == 0, before.astype(F32), jnp.where(sub == 1, after.astype(F32), 0.0))
        return jnp.concatenate([main, halo.astype(main.dtype)], axis=0)

    def token_mix(xx, hgx, atx):
        mix = _dot(hgx, wo_ref[0:HG_W, :]) + _dot(atx, wo_ref[HG_W:, :])
        return _ln(DN_ALPHA * _ln(xx, lng_ref[...], lnb_ref[...]) + mix, g1_ref[...], b1_ref[...])

    h1x = token_mix(
        with_halo(x_ref[...], xp_ref[SUBLANES - 1:SUBLANES, :], xn_ref[0:1, :]),
        with_halo(hg_ref[...], hgp_ref[...].astype(F32)[hrows - 1:hrows, :],
                  hgn_ref[...].astype(F32)[0:1, :]),
        with_halo(at_ref[...], atp_ref[...].astype(F32)[hrows - 1:hrows, :],
                  atn_ref[...].astype(F32)[0:1, :]))
    h1_s[...] = h1x[0:tm, :]
    sub = lax.broadcasted_iota(jnp.int32, (hrows, D_MODEL), 0)
    valid = jnp.where(sub == 0, has_prev, jnp.where(sub == 1, has_next, 0.0))
    ext_s[tm:, :] = (h1x[tm:, :] * valid).astype(BF16)

    ngrp = D_MODEL // LANES
    for c in range(ngrp):
        for a in range(SUBLANES):
            il_s[c, pl.ds(a, grp, stride=SUBLANES), :] = h1_s[a * grp:(a + 1) * grp,
                                                              c * LANES:(c + 1) * LANES]
        ext_s[0:tm, c * LANES:(c + 1) * LANES] = il_s[c].astype(BF16)
    acc_s[...] = jnp.zeros_like(acc_s)

    half = tm // 2

    def chunk_cols(j, part):
        start = part * D_FF + j * FFN_FC
        return pl.ds(start if isinstance(start, int) else pl.multiple_of(start, FFN_FC), FFN_FC)

    def up(j, slot, part):
        u_s[slot, :, part * FFN_FC:(part + 1) * FFN_FC] = _dot(ext_s[...],
                                                               wu_ref[:, chunk_cols(j, part)])

    def glu(j, slot, part):
        r0, r1 = part * half, (part + 1) * half
        cw = jnp.concatenate([cw_ref[:, chunk_cols(j, 0)], cw_ref[:, chunk_cols(j, 1)]], axis=1)
        cb = jnp.concatenate([cb_ref[:, chunk_cols(j, 0)], cb_ref[:, chunk_cols(j, 1)]], axis=1)
        sub8 = lax.broadcasted_iota(jnp.int32, (SUBLANES, 2 * FFN_FC), 0)
        if part == 0:
            first = jnp.where(sub8 == 0, u_s[slot, tm:tm + 1, :],
                              pltpu.roll(u_s[slot, tm - SUBLANES:tm, :], 1, 0))
            prev = jnp.concatenate([first, u_s[slot, 0:r1 - SUBLANES, :]], axis=0)
            nxt = u_s[slot, SUBLANES:r1 + SUBLANES, :]
        else:
            last = jnp.where(sub8 == SUBLANES - 1, u_s[slot, tm + 1:tm + 2, :],
                             pltpu.roll(u_s[slot, 0:SUBLANES, :], SUBLANES - 1, 0))
            prev = u_s[slot, r0 - SUBLANES:r1 - SUBLANES, :]
            nxt = jnp.concatenate([u_s[slot, r0 + SUBLANES:tm, :], last], axis=0)
        c = prev * cw[0:1, :] + u_s[slot, r0:r1, :] * cw[1:2, :] + nxt * cw[2:3, :] + cb
        act_s[slot, r0:r1, :] = (_gelu_tanh(c[:, :FFN_FC]) * c[:, FFN_FC:]).astype(BF16)

    def down(j, slot, part):
        c0 = part * 2 * LANES
        k0 = j * FFN_FC
        rows = pl.ds(k0 if isinstance(k0, int) else pl.multiple_of(k0, FFN_FC), FFN_FC)
        res = _dot(act_s[slot], wd_ref[rows, c0:c0 + 2 * LANES])
        acc_s[2 * part] += res[:, :LANES]
        acc_s[2 * part + 1] += res[:, LANES:]

    def stage(t, par):
        static = isinstance(t, int)
        do_up = not static or t + 1 < n
        do_glu = not static or 0 <= t < n
        do_down = not static or 1 <= t <= n
        for part in range(2):
            if do_up:
                up(t + 1, 1 - par, part)
            if do_glu:
                glu(t, par, part)
            if do_down:
                down(t - 1, 1 - par, 2 * part)
                down(t - 1, 1 - par, 2 * part + 1)

    stage(-1, 1)
    stage(0, 0)
    npairs = (n - 2) // 2

    def pair(jj, carry):
        t = 2 * jj + 1
        stage(t, 1)
        stage(t + 1, 0)
        return carry

    lax.fori_loop(0, npairs, pair, 0, unroll=True)
    for t in range(2 * npairs + 1, n + 1):
        stage(t, t % 2)

    for a in range(SUBLANES):
        ffn = jnp.concatenate([acc_s[c, pl.ds(a, grp, stride=SUBLANES), :] for c in range(ngrp)],
                              axis=1)
        rows = slice(a * grp, (a + 1) * grp)
        out_ref[rows, :] = _ln(DN_ALPHA * h1_s[rows, :] + ffn, g2_ref[...], b2_ref[...])


def _ffn(x2, hg, at, ln_g, ln_b, wo, g1, b1, wu, cw, cb, wd, g2, b2, S):
    N = x2.shape[0]
    tm = FFN_TM
    hrows = 2 * SUBLANES
    row = lambda i: (i, 0)
    full = lambda a: pl.BlockSpec(a.shape, lambda i: (0,) * a.ndim, pipeline_mode=pl.Buffered(1))

    def halo_specs(rows, width):
        per_tile = tm // rows
        last = N // rows - 1
        return [pl.BlockSpec((rows, width), lambda i: (jnp.maximum(i * per_tile - 1, 0), 0)),
                pl.BlockSpec((rows, width), lambda i: (jnp.minimum((i + 1) * per_tile, last), 0))]

    return pl.pallas_call(
        functools.partial(_ffn_kernel, tiles_per_seq=S // tm),
        grid=(N // tm,),
        in_specs=([pl.BlockSpec((tm, D_MODEL), row)] + halo_specs(SUBLANES, D_MODEL)
                  + [pl.BlockSpec((tm, HG_W), row)] + halo_specs(hrows, HG_W)
                  + [pl.BlockSpec((tm, HG_W), row)] + halo_specs(hrows, HG_W)
                  + [full(a) for a in (ln_g, ln_b, wo, g1, b1, wu, cw, cb, wd, g2, b2)]),
        out_specs=pl.BlockSpec((tm, D_MODEL), row),
        out_shape=jax.ShapeDtypeStruct((N, D_MODEL), F32),
        scratch_shapes=[pltpu.VMEM((tm + hrows, D_MODEL), BF16),
                        pltpu.VMEM((2, tm + hrows, 2 * FFN_FC), F32),
                        pltpu.VMEM((2, tm, FFN_FC), BF16),
                        pltpu.VMEM((D_MODEL // LANES, tm, LANES), F32)],
        compiler_params=pltpu.CompilerParams(dimension_semantics=("arbitrary",),
                                             vmem_limit_bytes=60 * 1024 * 1024),
        name="ffn",
    )(x2, x2, x2, hg, hg, hg, at, at, at, ln_g, ln_b, wo, g1, b1, wu, cw, cb, wd, g2, b2)


def _swap_halves(w):
    half = MLA_ROPE // 2
    return jnp.concatenate([w[..., half:], w[..., :half]], axis=-1)


def kernel(x, positions, ln_in_g, ln_in_b, w_in, lb_fwd, lb_bwd, hg_norm_g, q_a_norm_g, w_q_b,
           kv_a_norm_g, w_kv_b, attn_norm_g, w_out, ln1_g, ln1_b, w_up, conv_w, conv_b, w_down,
           ln2_g, ln2_b):
    B, S, D = x.shape
    assert D == D_MODEL and w_in.shape[0] == 1 and lb_fwd.shape == (2, HG_W)
    assert S % HG_CHUNK == 0 and S % ATT_TQ == 0 and S % FFN_TM == 0 and D_FF % FFN_FC == 0
    N = B * S
    r1 = lambda a: a.reshape(1, -1)

    w_in0 = w_in[0]
    c_lat = 5 * HG_W
    c_kr = c_lat + MLA_Q_RANK + MLA_KV_RANK
    w_hg = w_in0[:, :c_lat].astype(BF16)
    w_kr = w_in0[:, c_kr:]
    w_lat = jnp.concatenate([w_in0[:, c_lat:c_kr], w_kr, _swap_halves(w_kr)], axis=1).astype(BF16)
    wq = w_q_b[0].reshape(MLA_Q_RANK, MLA_HEADS, MLA_NOPE + MLA_ROPE)
    wq_pe = wq[:, :, MLA_NOPE:]
    wqb = jnp.concatenate(
        [wq[:, :, :MLA_NOPE].reshape(MLA_Q_RANK, -1),
         jnp.concatenate([wq_pe, _swap_halves(wq_pe)], axis=-1).reshape(MLA_Q_RANK, -1)],
        axis=1).astype(BF16)
    wkvb = w_kv_b[0].astype(BF16)
    half = MLA_ROPE // 2
    inv_freq = 1.0 / (ROPE_THETA ** (jnp.arange(half, dtype=F32) / half))
    zeros = jnp.zeros((LANES - MLA_ROPE,), F32)
    invf = r1(jnp.tile(inv_freq, LANES // half))
    sgn = r1(jnp.concatenate([-jnp.ones((half,), F32), jnp.ones((half,), F32), zeros]))
    wu = w_up[0].astype(BF16)
    cw = conv_w[0]
    cb = r1(conv_b[0])
    wd = w_down[0].astype(BF16)

    x2 = x.reshape(N, D)
    pos2 = positions.reshape(N // LANES, LANES)
    q, ff, fb, gate, v, qa, ka, va = _proj(
        x2, pos2, r1(ln_in_g), r1(ln_in_b), w_hg, w_lat, lb_fwd, lb_bwd,
        r1(q_a_norm_g[0]), wqb, r1(kv_a_norm_g[0]), wkvb, invf, sgn, B, S)
    hg = _hgrn(q, ff, fb, v, gate, r1(hg_norm_g[0]), B, S)
    at = _attn(qa, ka, va, r1(attn_norm_g[0]), B, S)
    out = _ffn(x2, hg, at, r1(ln_in_g), r1(ln_in_b), w_out[0].astype(BF16), r1(ln1_g[0]), r1(ln1_b[0]),
               wu, cw, cb, wd, r1(ln2_g[0]), r1(ln2_b[0]), S)
    return out.reshape(B, S, D)
```
